```python
import math
import jax, jax.numpy as jnp
from jax import lax
import numpy as np

D_MODEL = 2048
BATCH = 2
SEQ = 4096
DEPTH = 2

GRID_W = 64
CTX_LEN = 256
HEAD_DIM = 128
N_MOD = 9
FFN_HIDDEN = 5632
RMS_EPS = 1e-6
ROPE_THETA = 10000.0
Q_BLOCK = 128

N_HEADS_TOTAL = D_MODEL // HEAD_DIM
A_HEADS = (3 * N_HEADS_TOTAL) // 8
A_KV_HEADS = A_HEADS // 3
B_HEADS = N_HEADS_TOTAL // 4
C_HEADS = N_HEADS_TOTAL - A_HEADS - B_HEADS
B_SUB_DIM = HEAD_DIM // 2
A_WIDTH = A_HEADS * HEAD_DIM
A_KV_WIDTH = A_KV_HEADS * HEAD_DIM
B_WIDTH = B_HEADS * HEAD_DIM
C_WIDTH = C_HEADS * HEAD_DIM
MIX_WIDTH = A_WIDTH + B_WIDTH + C_WIDTH
IN_SIZES = (A_WIDTH, A_KV_WIDTH, A_KV_WIDTH, B_WIDTH, B_WIDTH, B_WIDTH, C_WIDTH, C_WIDTH, C_WIDTH)
IN_WIDTH = A_WIDTH + 2 * A_KV_WIDTH + 3 * B_WIDTH + 3 * C_WIDTH
NA_ROWS_MAX = 8
NA_COLS = 16

kernel_name = "hybrid_headgroup_diffusion_block"


def rms_norm(x, gain=None):
    xf = x.astype(jnp.float32)
    y = xf * lax.rsqrt(jnp.mean(xf * xf, axis=-1, keepdims=True) + RMS_EPS)
    if gain is not None:
        y = y * gain.astype(jnp.float32)
    return y.astype(x.dtype)


def modulate(xn, shift, scale):
    return xn * (1 + scale) + shift


def swiglu(h, w_gu, w_down):
    gate, up = jnp.split(h @ w_gu, 2, axis=-1)
    return (jax.nn.silu(gate) * up) @ w_down


def rope_1d(x, pos):
    half = x.shape[-1] // 2
    freqs = ROPE_THETA ** (-jnp.arange(half, dtype=jnp.float32) / half)
    ang = pos.astype(jnp.float32)[:, None] * freqs[None, :]
    cos = jnp.cos(ang)[None, :, None, :]
    sin = jnp.sin(ang)[None, :, None, :]
    xf = x.astype(jnp.float32)
    x1, x2 = xf[..., :half], xf[..., half:]
    return jnp.concatenate([x1 * cos - x2 * sin, x1 * sin + x2 * cos], axis=-1).astype(x.dtype)


def rope_2d(x, row, col):
    d = x.shape[-1] // 2
    return jnp.concatenate([rope_1d(x[..., :d], row), rope_1d(x[..., d:], col)], axis=-1)


def sweep_query_blocks(fn, q):
    b, s = q.shape[0], q.shape[1]
    nb = s // Q_BLOCK
    qb = jnp.moveaxis(q.reshape((b, nb, Q_BLOCK) + q.shape[2:]), 1, 0)
    out = lax.map(fn, qb)
    return jnp.moveaxis(out, 0, 1).reshape((b, s) + out.shape[3:])


def gqa_attend(q, k, v):
    b, nq, h, dh = q.shape
    hkv = k.shape[2]
    qg = q.reshape(b, nq, hkv, h // hkv, dh)
    s = jnp.einsum('bqkgd,blkd->bkgql', qg, k, preferred_element_type=jnp.float32) * (dh ** -0.5)
    p = jax.nn.softmax(s, axis=-1).astype(v.dtype)
    o = jnp.einsum('bkgql,blkd->bqkgd', p, v)
    return o.reshape(b, nq, h, dh)


def diff_attend(q, k, v, lam):
    ds = q.shape[-1]
    s = jnp.einsum('bqhmd,blhmd->bhmql', q, k, preferred_element_type=jnp.float32) * (ds ** -0.5)
    p = jax.nn.softmax(s, axis=-1)
    w = p[:, :, 0] - lam * p[:, :, 1]
    return jnp.einsum('bhql,blhd->bqhd', w.astype(v.dtype), v)


def neighborhood_attend(q, k, v, k_ctx, v_ctx, rel_bias):
    b, s, h, dh = q.shape
    rows = s // GRID_W
    wr = min(NA_ROWS_MAX, rows)
    n_win = wr * NA_COLS
    kg = k.reshape(b, rows, GRID_W, h, dh)
    vg = v.reshape(b, rows, GRID_W, h, dh)
    qg = jnp.moveaxis(q.reshape(b, rows, GRID_W, h, dh), 1, 0)
    row_start = jnp.clip(jnp.arange(rows) - wr // 2, 0, rows - wr)
    col_start = jnp.clip(jnp.arange(GRID_W) - NA_COLS // 2, 0, GRID_W - NA_COLS)
    col_idx = col_start[:, None] + jnp.arange(NA_COLS)[None, :]
    off_c = col_idx - jnp.arange(GRID_W)[:, None] + (NA_COLS - 1)
    scale = dh ** -0.5

    def row_block(args):
        r, q_row = args
        rs = row_start[r]
        k_win = lax.dynamic_slice_in_dim(kg, rs, wr, axis=1)[:, :, col_idx]
        v_win = lax.dynamic_slice_in_dim(vg, rs, wr, axis=1)[:, :, col_idx]
        off_r = rs + jnp.arange(wr) - r + (NA_ROWS_MAX - 1)
        bias = rel_bias[:, off_r[:, None, None], off_c[None, :, :]]
        s_win = jnp.einsum('bjhd,bajchd->bhjac', q_row, k_win,
                           preferred_element_type=jnp.float32) * scale
        s_win = s_win + jnp.transpose(bias, (0, 2, 1, 3))[None].astype(jnp.float32)
        s_ctx = jnp.einsum('bjhd,bnhd->bhjn', q_row, k_ctx,
                           preferred_element_type=jnp.float32) * scale
        scores = jnp.concatenate([s_win.reshape(b, h, GRID_W, n_win), s_ctx], axis=-1)
        p = jax.nn.softmax(scores, axis=-1).astype(v.dtype)
        p_win = p[..., :n_win].reshape(b, h, GRID_W, wr, NA_COLS)
        p_ctx = p[..., n_win:]
        return (jnp.einsum('bhjac,bajchd->bjhd', p_win, v_win)
                + jnp.einsum('bhjn,bnhd->bjhd', p_ctx, v_ctx))

    out = lax.map(row_block, (jnp.arange(rows), qg))
    return jnp.moveaxis(out, 0, 1).reshape(b, s, h, dh)


def split_in_proj(p):
    offsets = []
    acc = 0
    for size in IN_SIZES[:-1]:
        acc += size
        offsets.append(acc)
    return jnp.split(p, offsets, axis=-1)


def heads(t, h, d):
    return t.reshape(t.shape[0], t.shape[1], h, d)


def mixer_a(xq, xk, xv, gq, gk, gv, q_gain, k_gain, out_gain, row, col, with_ctx):
    b, s, _ = xq.shape
    n = gq.shape[1]
    lq = rope_2d(rms_norm(heads(xq, A_HEADS, HEAD_DIM), q_gain), row, col)
    lk = rope_2d(rms_norm(heads(xk, A_KV_HEADS, HEAD_DIM), k_gain), row, col)
    ck = rms_norm(heads(gk, A_KV_HEADS, HEAD_DIM), k_gain)
    cv = heads(gv, A_KV_HEADS, HEAD_DIM)
    k_all = jnp.concatenate([ck, lk], axis=1)
    v_all = jnp.concatenate([cv, heads(xv, A_KV_HEADS, HEAD_DIM)], axis=1)
    lat = sweep_query_blocks(lambda qb: gqa_attend(qb, k_all, v_all), lq)
    lat = rms_norm(lat.reshape(b, s, A_WIDTH), out_gain)
    ctx_out = None
    if with_ctx:
        cq = rms_norm(heads(gq, A_HEADS, HEAD_DIM), q_gain)
        ctx_out = rms_norm(gqa_attend(cq, ck, cv).reshape(b, n, A_WIDTH), out_gain)
    return lat, ctx_out


def mixer_b(xq, xk, xv, gq, gk, gv, q_gain, k_gain, lam_vecs, out_gain, lam_init, row, col, with_ctx):
    b, s, _ = xq.shape
    n = gq.shape[1]

    def qk_heads(t):
        return t.reshape(t.shape[0], t.shape[1], B_HEADS, 2, B_SUB_DIM)

    def rope_sub(t):
        return rope_2d(t.reshape(b, s, 2 * B_HEADS, B_SUB_DIM), row, col).reshape(b, s, B_HEADS, 2, B_SUB_DIM)

    lq = rope_sub(rms_norm(qk_heads(xq), q_gain))
    lk = rope_sub(rms_norm(qk_heads(xk), k_gain))
    ck = rms_norm(qk_heads(gk), k_gain)
    cv = heads(gv, B_HEADS, HEAD_DIM)
    k_all = jnp.concatenate([ck, lk], axis=1)
    v_all = jnp.concatenate([cv, heads(xv, B_HEADS, HEAD_DIM)], axis=1)
    lv = lam_vecs.astype(jnp.float32)
    lam = jnp.exp(jnp.sum(lv[0] * lv[1])) - jnp.exp(jnp.sum(lv[2] * lv[3])) + lam_init
    lat = sweep_query_blocks(lambda qb: diff_attend(qb, k_all, v_all, lam), lq)
    lat = (rms_norm(lat, out_gain) * (1.0 - lam_init)).reshape(b, s, B_WIDTH)
    ctx_out = None
    if with_ctx:
        cq = rms_norm(qk_heads(gq), q_gain)
        co = diff_attend(cq, ck, cv, lam)
        ctx_out = (rms_norm(co, out_gain) * (1.0 - lam_init)).reshape(b, n, B_WIDTH)
    return lat, ctx_out


def mixer_c(xq, xk, xv, gq, gk, gv, q_gain, k_gain, rel_bias, out_gain, with_ctx):
    b, s, _ = xq.shape
    n = gq.shape[1]
    lq = rms_norm(heads(xq, C_HEADS, HEAD_DIM), q_gain)
    lk = rms_norm(heads(xk, C_HEADS, HEAD_DIM), k_gain)
    ck = rms_norm(heads(gk, C_HEADS, HEAD_DIM), k_gain)
    cv = heads(gv, C_HEADS, HEAD_DIM)
    lat = neighborhood_attend(lq, lk, heads(xv, C_HEADS, HEAD_DIM), ck, cv, rel_bias)
    lat = rms_norm(lat.reshape(b, s, C_WIDTH), out_gain)
    ctx_out = None
    if with_ctx:
        cq = rms_norm(heads(gq, C_HEADS, HEAD_DIM), q_gain)
        ctx_out = rms_norm(gqa_attend(cq, ck, cv).reshape(b, n, C_WIDTH), out_gain)
    return lat, ctx_out


def setup_inputs(seed: int = 0) -> dict:
    key = jax.random.key(seed)
    ks = jax.random.split(key, 24)
    L, D, F = DEPTH, D_MODEL, FFN_HIDDEN
    nrm = jax.random.normal
    f32 = jnp.float32

    def gain(k, shape):
        return 1.0 + 0.1 * nrm(k, shape, f32)

    return {
        "x": nrm(ks[0], (BATCH, SEQ, D), f32),
        "c": nrm(ks[1], (BATCH, D), f32),
        "ctx": nrm(ks[2], (BATCH, CTX_LEN, D), f32),
        "c_ctx": nrm(ks[3], (D,), f32),
        "w_mod": nrm(ks[4], (L, D, N_MOD * D), f32) * (0.5 * D ** -0.5),
        "b_mod": nrm(ks[5], (L, N_MOD * D), f32) * 0.02,
        "ffn1_w_gu": nrm(ks[6], (L, D, 2 * F), f32) * (D ** -0.5),
        "ffn1_w_down": nrm(ks[7], (L, F, D), f32) * (F ** -0.5),
        "ffn2_w_gu": nrm(ks[8], (L, D, 2 * F), f32) * (D ** -0.5),
        "ffn2_w_down": nrm(ks[9], (L, F, D), f32) * (F ** -0.5),
        "w_in": nrm(ks[10], (L, D, IN_WIDTH), f32) * (D ** -0.5),
        "w_out": nrm(ks[11], (L, MIX_WIDTH, D), f32) * (MIX_WIDTH ** -0.5),
        "a_q_gain": gain(ks[12], (L, HEAD_DIM)),
        "a_k_gain": gain(ks[13], (L, HEAD_DIM)),
        "a_out_gain": gain(ks[14], (L, A_WIDTH)),
        "b_q_gain": gain(ks[15], (L, B_SUB_DIM)),
        "b_k_gain": gain(ks[16], (L, B_SUB_DIM)),
        "b_lambda": nrm(ks[17], (L, 4, B_SUB_DIM), f32) * 0.1,
        "b_out_gain": gain(ks[18], (L, HEAD_DIM)),
        "c_q_gain": gain(ks[19], (L, HEAD_DIM)),
        "c_k_gain": gain(ks[20], (L, HEAD_DIM)),
        "c_rel_bias": nrm(ks[21], (L, C_HEADS, 2 * NA_ROWS_MAX - 1, 2 * NA_COLS - 1), f32) * 0.1,
        "c_out_gain": gain(ks[22], (L, C_WIDTH)),
    }


def reference(x, c, ctx, c_ctx, w_mod, b_mod, ffn1_w_gu, ffn1_w_down, ffn2_w_gu, ffn2_w_down,
              w_in, w_out, a_q_gain, a_k_gain, a_out_gain, b_q_gain, b_k_gain, b_lambda, b_out_gain,
              c_q_gain, c_k_gain, c_rel_bias, c_out_gain):
    b, s, d = x.shape
    t = jnp.arange(s, dtype=jnp.int32)
    row = t // GRID_W
    col = t % GRID_W
    g = ctx
    for l in range(DEPTH):
        with_ctx = l < DEPTH - 1
        mx = (jax.nn.silu(c) @ w_mod[l] + b_mod[l]).reshape(b, N_MOD, 1, d)
        mg = (jax.nn.silu(c_ctx) @ w_mod[l] + b_mod[l]).reshape(N_MOD, 1, 1, d)

        x = x + 0.5 * mx[:, 2] * swiglu(modulate(rms_norm(x), mx[:, 0], mx[:, 1]), ffn1_w_gu[l], ffn1_w_down[l])
        g = g + 0.5 * mg[2] * swiglu(modulate(rms_norm(g), mg[0], mg[1]), ffn1_w_gu[l], ffn1_w_down[l])

        xn = modulate(rms_norm(x), mx[:, 3], mx[:, 4])
        gn = modulate(rms_norm(g), mg[3], mg[4])
        px = split_in_proj(xn @ w_in[l])
        pg = split_in_proj(gn @ w_in[l])
        lam_init = 0.8 - 0.6 * math.exp(-0.3 * l)
        a_lat, a_ctx = mixer_a(px[0], px[1], px[2], pg[0], pg[1], pg[2],
                               a_q_gain[l], a_k_gain[l], a_out_gain[l], row, col, with_ctx)
        b_lat, b_ctx = mixer_b(px[3], px[4], px[5], pg[3], pg[4], pg[5],
                               b_q_gain[l], b_k_gain[l], b_lambda[l], b_out_gain[l], lam_init, row, col, with_ctx)
        c_lat, c_ctx_o = mixer_c(px[6], px[7], px[8], pg[6], pg[7], pg[8],
                                 c_q_gain[l], c_k_gain[l], c_rel_bias[l], c_out_gain[l], with_ctx)
        x = x + mx[:, 5] * (jnp.concatenate([a_lat, b_lat, c_lat], axis=-1) @ w_out[l])
        x = x + 0.5 * mx[:, 8] * swiglu(modulate(rms_norm(x), mx[:, 6], mx[:, 7]), ffn2_w_gu[l], ffn2_w_down[l])
        if with_ctx:
            g = g + mg[5] * (jnp.concatenate([a_ctx, b_ctx, c_ctx_o], axis=-1) @ w_out[l])
            g = g + 0.5 * mg[8] * swiglu(modulate(rms_norm(g), mg[6], mg[7]), ffn2_w_gu[l], ffn2_w_down[l])
    return x
```

```python
import functools
import math

import numpy as np
import jax
import jax.numpy as jnp
from jax import lax
from jax.experimental import pallas as pl
from jax.experimental.pallas import tpu as pltpu

F32 = jnp.float32
BF16 = jnp.bfloat16

D_MODEL = 2048
SEQ = 4096
DEPTH = 2
GRID_W = 64
GRID_H = SEQ // GRID_W
CTX_LEN = 256
HEAD_DIM = 128
N_MOD = 9
FFN_HIDDEN = 5632
RMS_EPS = 1e-6
ROPE_THETA = 10000.0

A_HEADS = 6
A_KV_HEADS = 2
A_GROUP = A_HEADS // A_KV_HEADS
B_HEADS = 4
C_HEADS = 6
B_SUB_DIM = HEAD_DIM // 2
A_WIDTH = A_HEADS * HEAD_DIM
A_KV_WIDTH = A_KV_HEADS * HEAD_DIM
B_WIDTH = B_HEADS * HEAD_DIM
C_WIDTH = C_HEADS * HEAD_DIM
MIX_WIDTH = A_WIDTH + B_WIDTH + C_WIDTH
IN_WIDTH = A_WIDTH + 2 * A_KV_WIDTH + 3 * B_WIDTH + 3 * C_WIDTH
NA_ROWS = 8
NA_COLS = 16

AQ_H = 0
AK_H = AQ_H + A_HEADS
AV_H = AK_H + A_KV_HEADS
BQ_H = AV_H + A_KV_HEADS
BK_H = BQ_H + B_HEADS
BV_H = BK_H + B_HEADS
CQ_H = BV_H + B_HEADS
CK_H = CQ_H + C_HEADS
CV_H = CK_H + C_HEADS

VMEM_LIMIT = 56 * 1024 * 1024
MASK_VALUE = -1e30

TM = 1024
TF = 256
TN = 256
TQ_A = 256
TQ_B = 256
KV_CHUNK = 512
NB_ROWS = 4
NB_Q = NB_ROWS * GRID_W
NB_KROWS = 12
NB_K = NB_KROWS * GRID_W


def _silu(v):
    return v / (1.0 + jnp.exp(-v))


def _rms(v):
    return v * lax.rsqrt(jnp.mean(v * v, axis=-1, keepdims=True) + RMS_EPS)


def _params(*sem):
    return pltpu.CompilerParams(dimension_semantics=sem, vmem_limit_bytes=VMEM_LIMIT)


def _mod_kernel(c_ref, w_ref, b_ref, o_ref):
    s = _silu(c_ref[...]).astype(BF16)
    o_ref[...] = jnp.dot(s, w_ref[...].astype(BF16), preferred_element_type=F32) + b_ref[...]


def _modulation(cvec, w_mod, b_mod):
    depth, d, n = w_mod.shape
    tn = 1024
    return pl.pallas_call(
        _mod_kernel,
        grid=(depth, n // tn),
        in_specs=[
            pl.BlockSpec((8, d), lambda l, j: (0, 0)),
            pl.BlockSpec((None, d, tn), lambda l, j: (l, 0, j)),
            pl.BlockSpec((None, 1, tn), lambda l, j: (l, 0, j)),
        ],
        out_specs=pl.BlockSpec((None, 8, tn), lambda l, j: (l, 0, j)),
        out_shape=jax.ShapeDtypeStruct((depth, 8, n), F32),
        compiler_params=_params("parallel", "parallel"),
        name="modulation",
    )(cvec, w_mod, b_mod.reshape(depth, 1, n))


def _ffn_kernel(mod_row, nf, x_ref, mod_ref, wg_ref, wu_ref, wd_ref, o_ref, xn_ref):
    f = pl.program_id(1)

    @pl.when(f == 0)
    def _():
        shift = mod_ref[mod_row:mod_row + 1, :]
        scale = mod_ref[mod_row + 1:mod_row + 2, :]
        xn_ref[...] = (_rms(x_ref[...]) * (1.0 + scale) + shift).astype(BF16)
        o_ref[...] = jnp.zeros_like(o_ref)

    xn = xn_ref[...]
    g = jnp.dot(xn, wg_ref[...].astype(BF16), preferred_element_type=F32)
    u = jnp.dot(xn, wu_ref[...].astype(BF16), preferred_element_type=F32)
    a = (_silu(g) * u).astype(BF16)
    o_ref[...] += jnp.dot(a, wd_ref[...].astype(BF16), preferred_element_type=F32)

    @pl.when(f == nf - 1)
    def _():
        gate = mod_ref[mod_row + 2:mod_row + 3, :]
        o_ref[...] = x_ref[...] + (0.5 * gate) * o_ref[...]


def _ffn(x2, mod, w_gu, w_down, mod_row, tm, tiles_per_mod):
    m, d = x2.shape
    hidden = w_down.shape[0]
    nf = hidden // TF
    return pl.pallas_call(
        functools.partial(_ffn_kernel, mod_row, nf),
        grid=(m // tm, nf),
        in_specs=[
            pl.BlockSpec((tm, d), lambda i, f: (i, 0), pipeline_mode=pl.Buffered(1)),
            pl.BlockSpec((None, N_MOD, d), lambda i, f: (i // tiles_per_mod, 0, 0)),
            pl.BlockSpec((d, TF), lambda i, f: (0, f)),
            pl.BlockSpec((d, TF), lambda i, f: (0, nf + f)),
            pl.BlockSpec((TF, d), lambda i, f: (f, 0)),
        ],
        out_specs=pl.BlockSpec((tm, d), lambda i, f: (i, 0)),
        out_shape=jax.ShapeDtypeStruct((m, d), F32),
        scratch_shapes=[pltpu.VMEM((tm, d), BF16)],
        compiler_params=_params("parallel", "arbitrary"),
        name="ffn",
    )(x2, mod, w_gu, w_gu, w_down)


def _norm_lanes(y, width):
    y2 = y * y
    if width == HEAD_DIM:
        ms = jnp.mean(y2, axis=-1, keepdims=True)
    else:
        lane = lax.broadcasted_iota(jnp.int32, y.shape, 1)
        low = lane < width
        s_lo = jnp.sum(jnp.where(low, y2, 0.0), axis=-1, keepdims=True)
        s_hi = jnp.sum(jnp.where(low, 0.0, y2), axis=-1, keepdims=True)
        ms = jnp.where(low, s_lo, s_hi) * (1.0 / width)
    return y * lax.rsqrt(ms + RMS_EPS)


def _inproj_kernel(rope, x_ref, mod_ref, w_ref, gain_ref, *rest):
    if rope:
        ca_ref, sa1_ref, sa2_ref, cb_ref, sb1_ref, sb2_ref, o_ref, xn_ref = rest
    else:
        o_ref, xn_ref = rest
    j = pl.program_id(1)

    @pl.when(j == 0)
    def _():
        shift = mod_ref[3:4, :]
        scale = mod_ref[4:5, :]
        xn_ref[...] = (_rms(x_ref[...]) * (1.0 + scale) + shift).astype(BF16)

    y = jnp.dot(xn_ref[...], w_ref[...].astype(BF16), preferred_element_type=F32)
    heads_per_tile = TN // HEAD_DIM

    def emit(width, tables):
        for hh in range(heads_per_tile):
            sl = slice(hh * HEAD_DIM, (hh + 1) * HEAD_DIM)
            v = y[:, sl]
            if width:
                v = _norm_lanes(v, width) * gain_ref[:, sl]
            if tables is not None:
                c_ref, s1_ref, s2_ref, half = tables
                v = (v * c_ref[...]
                     + pltpu.roll(v, HEAD_DIM - half, 1) * s1_ref[...]
                     + pltpu.roll(v, half, 1) * s2_ref[...])
            o_ref[:, sl] = v.astype(o_ref.dtype)

    tile = lambda h: h // heads_per_tile
    in_a = j < tile(AV_H)
    in_b = (j >= tile(BQ_H)) & (j < tile(BV_H))
    in_c = (j >= tile(CQ_H)) & (j < tile(CV_H))
    plain = jnp.logical_not(in_a | in_b | in_c)

    @pl.when(in_a)
    def _():
        emit(HEAD_DIM, (ca_ref, sa1_ref, sa2_ref, HEAD_DIM // 4) if rope else None)

    @pl.when(in_b)
    def _():
        emit(B_SUB_DIM, (cb_ref, sb1_ref, sb2_ref, B_SUB_DIM // 4) if rope else None)

    @pl.when(in_c)
    def _():
        emit(HEAD_DIM, None)

    @pl.when(plain)
    def _():
        emit(0, None)


def _inproj(x2, mod, w_in, gains, tables, tm, tiles_per_mod):
    m, d = x2.shape
    n = w_in.shape[1]
    rope = tables is not None
    in_specs = [
        pl.BlockSpec((tm, d), lambda i, j: (i, 0), pipeline_mode=pl.Buffered(1)),
        pl.BlockSpec((None, N_MOD, d), lambda i, j: (i // tiles_per_mod, 0, 0)),
        pl.BlockSpec((d, TN), lambda i, j: (0, j)),
        pl.BlockSpec((1, TN), lambda i, j: (0, j)),
    ]
    args = [x2, mod, w_in, gains]
    if rope:
        pos_tiles = SEQ // tm
        for t in tables:
            in_specs.append(pl.BlockSpec((tm, HEAD_DIM), lambda i, j: (i % pos_tiles, 0)))
            args.append(t)
    return pl.pallas_call(
        functools.partial(_inproj_kernel, rope),
        grid=(m // tm, n // TN),
        in_specs=in_specs,
        out_specs=pl.BlockSpec((tm, TN), lambda i, j: (i, j)),
        out_shape=jax.ShapeDtypeStruct((m, n), BF16),
        scratch_shapes=[pltpu.VMEM((tm, d), BF16)],
        compiler_params=_params("parallel", "arbitrary"),
        name="inproj",
    )(*args)


def _rope_tables(half):
    t = jnp.arange(SEQ, dtype=jnp.int32)
    row = (t // GRID_W).astype(F32)[:, None]
    col = (t % GRID_W).astype(F32)[:, None]
    lane = np.arange(HEAD_DIM)
    freqs = ROPE_THETA ** (-jnp.asarray(lane % half, dtype=F32) / half)
    is_col = jnp.asarray((lane % (4 * half)) >= 2 * half)[None, :]
    ang = jnp.where(is_col, col, row) * freqs[None, :]
    first = jnp.asarray((lane % (2 * half)) < half)[None, :]
    cos = jnp.cos(ang)
    sin = jnp.sin(ang)
    return cos, jnp.where(first, -sin, 0.0), jnp.where(first, 0.0, sin)


_NT = (((1,), (1,)), ((), ()))


def _softmax_step(qs, k, v, state):
    s = lax.dot_general(qs, k, _NT, preferred_element_type=F32)
    m_cur = jnp.max(s, axis=-1, keepdims=True)
    if state is None:
        m_new = m_cur
        p = jnp.exp(s - m_new)
        l_new = jnp.sum(p, axis=-1, keepdims=True)
        acc = jnp.dot(p.astype(BF16), v, preferred_element_type=F32)
    else:
        m_old, l_old, acc_old = state
        m_new = jnp.maximum(m_old, m_cur)
        alpha = jnp.exp(m_old - m_new)
        p = jnp.exp(s - m_new)
        l_new = alpha * l_old + jnp.sum(p, axis=-1, keepdims=True)
        acc = alpha * acc_old + jnp.dot(p.astype(BF16), v, preferred_element_type=F32)
    return m_new, l_new, acc


def _attend(qs, kc_ref, vc_ref, kl_ref, vl_ref):
    state = _softmax_step(qs, kc_ref[...], vc_ref[...], None)
    if kl_ref is not None:
        n_chunks = kl_ref.shape[0] // KV_CHUNK

        def body(ci, st):
            start = pl.multiple_of(ci * KV_CHUNK, KV_CHUNK)
            return _softmax_step(qs, kl_ref[pl.ds(start, KV_CHUNK), :],
                                 vl_ref[pl.ds(start, KV_CHUNK), :], st)

        state = lax.fori_loop(0, n_chunks, body, state)
    _, l, acc = state
    return acc / l


def _gqa_kernel(group, latent, q_ref, kc_ref, vc_ref, *rest):
    if latent:
        kl_ref, vl_ref, o_ref = rest
    else:
        kl_ref = vl_ref = None
        (o_ref,) = rest
    tq = q_ref.shape[0]
    q = q_ref[...]
    qs = jnp.concatenate([q[:, g * HEAD_DIM:(g + 1) * HEAD_DIM] for g in range(group)], axis=0)
    o = _attend(qs, kc_ref, vc_ref, kl_ref, vl_ref)
    for g in range(group):
        o_ref[:, g * HEAD_DIM:(g + 1) * HEAD_DIM] = o[g * tq:(g + 1) * tq].astype(o_ref.dtype)


def _gqa(pq, pg, px, group, n_kv, q_h, k_h, v_h, tq):
    b, sq, _ = pq.shape
    latent = px is not None
    gw = group * HEAD_DIM
    in_specs = [
        pl.BlockSpec((None, tq, gw), lambda bi, g, qi: (bi, qi, q_h // group + g)),
        pl.BlockSpec((None, CTX_LEN, HEAD_DIM), lambda bi, g, qi: (bi, 0, k_h + g)),
        pl.BlockSpec((None, CTX_LEN, HEAD_DIM), lambda bi, g, qi: (bi, 0, v_h + g)),
    ]
    args = [pq, pg, pg]
    if latent:
        in_specs += [
            pl.BlockSpec((None, SEQ, HEAD_DIM), lambda bi, g, qi: (bi, 0, k_h + g)),
            pl.BlockSpec((None, SEQ, HEAD_DIM), lambda bi, g, qi: (bi, 0, v_h + g)),
        ]
        args += [px, px]
    return pl.pallas_call(
        functools.partial(_gqa_kernel, group, latent),
        grid=(b, n_kv, sq // tq),
        in_specs=in_specs,
        out_specs=pl.BlockSpec((None, tq, gw), lambda bi, g, qi: (bi, qi, g)),
        out_shape=jax.ShapeDtypeStruct((b, sq, n_kv * gw), F32),
        compiler_params=_params("parallel", "parallel", "arbitrary"),
        name="gqa_attention",
    )(*args)


def _diff_kernel(lam_init, latent, q_ref, lam_ref, kc_ref, vc_ref, *rest):
    if latent:
        kl_ref, vl_ref, o_ref = rest
    else:
        kl_ref = vl_ref = None
        (o_ref,) = rest
    tq = q_ref.shape[0]
    q = q_ref[...]
    low = lax.broadcasted_iota(jnp.int32, q.shape, 1) < B_SUB_DIM
    zero = jnp.zeros_like(q)
    qs = jnp.concatenate([jnp.where(low, q, zero), jnp.where(low, zero, q)], axis=0)
    o = _attend(qs, kc_ref, vc_ref, kl_ref, vl_ref)
    lv = lam_ref[...]
    lam = (jnp.exp(jnp.sum(lv[0:1] * lv[1:2], axis=-1, keepdims=True))
           - jnp.exp(jnp.sum(lv[2:3] * lv[3:4], axis=-1, keepdims=True)) + lam_init)
    o_ref[...] = (o[:tq] - lam * o[tq:]).astype(o_ref.dtype)


def _diff(pq, pg, px, lam_vecs, lam_init, tq):
    b, sq, _ = pq.shape
    latent = px is not None
    in_specs = [
        pl.BlockSpec((None, tq, HEAD_DIM), lambda bi, h, qi: (bi, qi, BQ_H + h)),
        pl.BlockSpec((4, B_SUB_DIM), lambda bi, h, qi: (0, 0)),
        pl.BlockSpec((None, CTX_LEN, HEAD_DIM), lambda bi, h, qi: (bi, 0, BK_H + h)),
        pl.BlockSpec((None, CTX_LEN, HEAD_DIM), lambda bi, h, qi: (bi, 0, BV_H + h)),
    ]
    args = [pq, lam_vecs, pg, pg]
    if latent:
        in_specs += [
            pl.BlockSpec((None, SEQ, HEAD_DIM), lambda bi, h, qi: (bi, 0, BK_H + h)),
            pl.BlockSpec((None, SEQ, HEAD_DIM), lambda bi, h, qi: (bi, 0, BV_H + h)),
        ]
        args += [px, px]
    return pl.pallas_call(
        functools.partial(_diff_kernel, lam_init, latent),
        grid=(b, B_HEADS, sq // tq),
        in_specs=in_specs,
        out_specs=pl.BlockSpec((None, tq, HEAD_DIM), lambda bi, h, qi: (bi, qi, h)),
        out_shape=jax.ShapeDtypeStruct((b, sq, B_WIDTH), F32),
        compiler_params=_params("parallel", "parallel", "arbitrary"),
        name="diff_attention",
    )(*args)


def _nbr_key_row(j):
    return jnp.clip(NB_ROWS * j - NA_ROWS // 2, 0, GRID_H - NB_KROWS)


def _nbr_kernel(q_ref, k_ref, v_ref, kc_ref, vc_ref, bias_ref, o_ref):
    j = pl.program_id(2)
    start = pl.multiple_of(_nbr_key_row(j) * GRID_W, NB_ROWS * GRID_W)
    q = q_ref[...]
    kw = k_ref[pl.ds(start, NB_K), :]
    vw = v_ref[pl.ds(start, NB_K), :]
    s_w = lax.dot_general(q, kw, _NT, preferred_element_type=F32) + bias_ref[...]
    s_c = lax.dot_general(q, kc_ref[...], _NT, preferred_element_type=F32)
    m = jnp.maximum(jnp.max(s_w, axis=-1, keepdims=True), jnp.max(s_c, axis=-1, keepdims=True))
    p_w = jnp.exp(s_w - m)
    p_c = jnp.exp(s_c - m)
    l = jnp.sum(p_w, axis=-1, keepdims=True) + jnp.sum(p_c, axis=-1, keepdims=True)
    o = (jnp.dot(p_w.astype(BF16), vw, preferred_element_type=F32)
         + jnp.dot(p_c.astype(BF16), vc_ref[...], preferred_element_type=F32))
    o_ref[...] = (o / l).astype(o_ref.dtype)


def _nbr_bias(rel_bias):
    n_blocks = GRID_H // NB_ROWS
    qq = np.arange(NB_Q)
    kk = np.arange(NB_K)
    idx_r, idx_c, valid = [], [], []
    for j in (0, 1, n_blocks - 1):
        key_row0 = int(np.clip(NB_ROWS * j - NA_ROWS // 2, 0, GRID_H - NB_KROWS))
        r = NB_ROWS * j + qq // GRID_W
        c = qq % GRID_W
        rs = np.clip(r - NA_ROWS // 2, 0, GRID_H - NA_ROWS)
        cs = np.clip(c - NA_COLS // 2, 0, GRID_W - NA_COLS)
        kr = key_row0 + kk // GRID_W
        kc = kk % GRID_W
        ok = ((kr[None, :] >= rs[:, None]) & (kr[None, :] < rs[:, None] + NA_ROWS)
              & (kc[None, :] >= cs[:, None]) & (kc[None, :] < cs[:, None] + NA_COLS))
        idx_r.append(np.clip(kr[None, :] - r[:, None] + NA_ROWS - 1, 0, 2 * NA_ROWS - 2))
        idx_c.append(np.clip(kc[None, :] - c[:, None] + NA_COLS - 1, 0, 2 * NA_COLS - 2))
        valid.append(ok)
    idx_r = np.stack(idx_r)
    idx_c = np.stack(idx_c)
    valid = np.stack(valid)
    dense = rel_bias.astype(F32)[:, idx_r, idx_c]
    dense = jnp.where(valid[None], dense, MASK_VALUE)
    return jnp.transpose(dense, (1, 0, 2, 3))


def _nbr(px, pg, bias_cls):
    b = px.shape[0]
    n_blocks = GRID_H // NB_ROWS

    def cls(j):
        return jnp.where(j == 0, 0, jnp.where(j == n_blocks - 1, 2, 1))

    return pl.pallas_call(
        _nbr_kernel,
        grid=(b, C_HEADS, n_blocks),
        in_specs=[
            pl.BlockSpec((None, NB_Q, HEAD_DIM), lambda bi, h, j: (bi, j, CQ_H + h)),
            pl.BlockSpec((None, SEQ, HEAD_DIM), lambda bi, h, j: (bi, 0, CK_H + h)),
            pl.BlockSpec((None, SEQ, HEAD_DIM), lambda bi, h, j: (bi, 0, CV_H + h)),
            pl.BlockSpec((None, CTX_LEN, HEAD_DIM), lambda bi, h, j: (bi, 0, CK_H + h)),
            pl.BlockSpec((None, CTX_LEN, HEAD_DIM), lambda bi, h, j: (bi, 0, CV_H + h)),
            pl.BlockSpec((None, None, NB_Q, NB_K), lambda bi, h, j: (cls(j), h, 0, 0)),
        ],
        out_specs=pl.BlockSpec((None, NB_Q, HEAD_DIM), lambda bi, h, j: (bi, j, h)),
        out_shape=jax.ShapeDtypeStruct((b, SEQ, C_WIDTH), F32),
        compiler_params=_params("parallel", "parallel", "arbitrary"),
        name="nbr_attention",
    )(px, px, px, pg, pg, bias_cls)


def _outproj_kernel(b_scale, x_ref, mod_ref, a_ref, b_ref, c_ref, ag_ref, bg_ref, cg_ref,
                    w_ref, o_ref, mix_ref):
    j = pl.program_id(1)

    @pl.when(j == 0)
    def _():
        mix_ref[:, :A_WIDTH] = (_rms(a_ref[...]) * ag_ref[...]).astype(BF16)
        for h in range(B_HEADS):
            sl = slice(h * HEAD_DIM, (h + 1) * HEAD_DIM)
            bn = _rms(b_ref[:, sl]) * bg_ref[...] * b_scale
            mix_ref[:, A_WIDTH + h * HEAD_DIM:A_WIDTH + (h + 1) * HEAD_DIM] = bn.astype(BF16)
        mix_ref[:, A_WIDTH + B_WIDTH:] = (_rms(c_ref[...]) * cg_ref[...]).astype(BF16)

    y = jnp.dot(mix_ref[...], w_ref[...].astype(BF16), preferred_element_type=F32)
    o_ref[...] = x_ref[...] + mod_ref[5:6, :] * y


def _outproj(x2, mod, a2, b2, c2, a_gain, b_gain, c_gain, w_out, b_scale, tm, tiles_per_mod):
    m, d = x2.shape
    return pl.pallas_call(
        functools.partial(_outproj_kernel, b_scale),
        grid=(m // tm, d // TN),
        in_specs=[
            pl.BlockSpec((tm, TN), lambda i, j: (i, j)),
            pl.BlockSpec((None, N_MOD, TN), lambda i, j: (i // tiles_per_mod, 0, j)),
            pl.BlockSpec((tm, A_WIDTH), lambda i, j: (i, 0)),
            pl.BlockSpec((tm, B_WIDTH), lambda i, j: (i, 0)),
            pl.BlockSpec((tm, C_WIDTH), lambda i, j: (i, 0)),
            pl.BlockSpec((1, A_WIDTH), lambda i, j: (0, 0)),
            pl.BlockSpec((1, HEAD_DIM), lambda i, j: (0, 0)),
            pl.BlockSpec((1, C_WIDTH), lambda i, j: (0, 0)),
            pl.BlockSpec((MIX_WIDTH, TN), lambda i, j: (0, j)),
        ],
        out_specs=pl.BlockSpec((tm, TN), lambda i, j: (i, j)),
        out_shape=jax.ShapeDtypeStruct((m, d), F32),
        scratch_shapes=[pltpu.VMEM((tm, MIX_WIDTH), BF16)],
        compiler_params=_params("parallel", "arbitrary"),
        name="outproj",
    )(x2, mod, a2, b2, c2, a_gain, b_gain, c_gain, w_out)


def _column_gains(a_q, a_k, b_q, b_k, c_q, c_k):
    ones = jnp.ones((HEAD_DIM,), F32)
    a_scale = HEAD_DIM ** -0.5
    b_scale = B_SUB_DIM ** -0.5
    parts = (
        [a_q * a_scale] * A_HEADS + [a_k] * A_KV_HEADS + [ones] * A_KV_HEADS
        + [jnp.tile(b_q, 2) * b_scale] * B_HEADS + [jnp.tile(b_k, 2)] * B_HEADS + [ones] * B_HEADS
        + [c_q * a_scale] * C_HEADS + [c_k] * C_HEADS + [ones] * C_HEADS
    )
    return jnp.concatenate(parts).reshape(1, IN_WIDTH)


def kernel(x, c, ctx, c_ctx, w_mod, b_mod, ffn1_w_gu, ffn1_w_down, ffn2_w_gu, ffn2_w_down,
           w_in, w_out, a_q_gain, a_k_gain, a_out_gain, b_q_gain, b_k_gain, b_lambda, b_out_gain,
           c_q_gain, c_k_gain, c_rel_bias, c_out_gain):
    b, s, d = x.shape
    n_ctx = ctx.shape[1]
    assert (b, s, d, n_ctx) == (2, SEQ, D_MODEL, CTX_LEN)

    cvec = jnp.zeros((8, d), F32).at[:b].set(c).at[b].set(c_ctx)
    mods = _modulation(cvec, w_mod, b_mod).reshape(DEPTH, 8, N_MOD, d)
    tables = _rope_tables(HEAD_DIM // 4) + _rope_tables(B_SUB_DIM // 4)

    x2 = x.reshape(b * s, d)
    g2 = ctx.reshape(b * n_ctx, d)
    tm_g = b * n_ctx
    lat_tiles = s // TM

    for l in range(DEPTH):
        with_ctx = l < DEPTH - 1
        mod_x = mods[l, :b]
        mod_g = mods[l, b:b + 1]
        lam_init = 0.8 - 0.6 * math.exp(-0.3 * l)
        gains = _column_gains(a_q_gain[l], a_k_gain[l], b_q_gain[l], b_k_gain[l],
                              c_q_gain[l], c_k_gain[l])

        x2 = _ffn(x2, mod_x, ffn1_w_gu[l], ffn1_w_down[l], 0, TM, lat_tiles)
        g2 = _ffn(g2, mod_g, ffn1_w_gu[l], ffn1_w_down[l], 0, tm_g, 1)

        px = _inproj(x2, mod_x, w_in[l], gains, tables, TM, lat_tiles).reshape(b, s, IN_WIDTH)
        pg = _inproj(g2, mod_g, w_in[l], gains, None, tm_g, 1).reshape(b, n_ctx, IN_WIDTH)

        a_lat = _gqa(px, pg, px, A_GROUP, A_KV_HEADS, AQ_H, AK_H, AV_H, TQ_A)
        b_lat = _diff(px, pg, px, b_lambda[l], lam_init, TQ_B)
        c_lat = _nbr(px, pg, _nbr_bias(c_rel_bias[l]))

        out_gains = (a_out_gain[l].reshape(1, A_WIDTH), b_out_gain[l].reshape(1, HEAD_DIM),
                     c_out_gain[l].reshape(1, C_WIDTH))
        x2 = _outproj(x2, mod_x, a_lat.reshape(b * s, A_WIDTH), b_lat.reshape(b * s, B_WIDTH),
                      c_lat.reshape(b * s, C_WIDTH), *out_gains, w_out[l], 1.0 - lam_init,
                      TM, lat_tiles)
        x2 = _ffn(x2, mod_x, ffn2_w_gu[l], ffn2_w_down[l], 6, TM, lat_tiles)

        if with_ctx:
            a_ctx = _gqa(pg, pg, None, A_GROUP, A_KV_HEADS, AQ_H, AK_H, AV_H, n_ctx)
            b_ctx = _diff(pg, pg, None, b_lambda[l], lam_init, n_ctx)
            c_ctx = _gqa(pg, pg, None, 1, C_HEADS, CQ_H, CK_H, CV_H, n_ctx)
            g2 = _outproj(g2, mod_g, a_ctx.reshape(tm_g, A_WIDTH), b_ctx.reshape(tm_g, B_WIDTH),
                          c_ctx.reshape(tm_g, C_WIDTH), *out_gains, w_out[l], 1.0 - lam_init,
                          tm_g, 1)
            g2 = _ffn(g2, mod_g, ffn2_w_gu[l], ffn2_w_down[l], 6, tm_g, 1)

    return x2.reshape(b, s, d)
```

```python
import functools
import math

import numpy as np
import jax
import jax.numpy as jnp
from jax import lax
from jax.experimental import pallas as pl
from jax.experimental.pallas import tpu as pltpu

F32 = jnp.float32
BF16 = jnp.bfloat16

D_MODEL = 2048
SEQ = 4096
DEPTH = 2
GRID_W = 64
GRID_H = SEQ // GRID_W
CTX_LEN = 256
HEAD_DIM = 128
N_MOD = 9
FFN_HIDDEN = 5632
RMS_EPS = 1e-6
ROPE_THETA = 10000.0

A_HEADS = 6
A_KV_HEADS = 2
A_GROUP = A_HEADS // A_KV_HEADS
B_HEADS = 4
C_HEADS = 6
B_SUB_DIM = HEAD_DIM // 2
A_WIDTH = A_HEADS * HEAD_DIM
A_KV_WIDTH = A_KV_HEADS * HEAD_DIM
B_WIDTH = B_HEADS * HEAD_DIM
C_WIDTH = C_HEADS * HEAD_DIM
MIX_WIDTH = A_WIDTH + B_WIDTH + C_WIDTH
IN_WIDTH = A_WIDTH + 2 * A_KV_WIDTH + 3 * B_WIDTH + 3 * C_WIDTH
NA_ROWS = 8
NA_COLS = 16

AQ_H = 0
AK_H = AQ_H + A_HEADS
AV_H = AK_H + A_KV_HEADS
BQ_H = AV_H + A_KV_HEADS
BK_H = BQ_H + B_HEADS
BV_H = BK_H + B_HEADS
CQ_H = BV_H + B_HEADS
CK_H = CQ_H + C_HEADS
CV_H = CK_H + C_HEADS

VMEM_LIMIT = 56 * 1024 * 1024
MASK_VALUE = -1e30

TM = 1024
TF = 256
TN = 256
TQ_A = 256
TQ_B = 256
KV_CHUNK = 512
NB_ROWS = 4
NB_Q = NB_ROWS * GRID_W
NB_KROWS = 12
NB_K = NB_KROWS * GRID_W


def _silu(v):
    return v / (1.0 + jnp.exp(-v))


def _rms(v):
    return v * lax.rsqrt(jnp.mean(v * v, axis=-1, keepdims=True) + RMS_EPS)


def _params(*sem):
    return pltpu.CompilerParams(dimension_semantics=sem, vmem_limit_bytes=VMEM_LIMIT)


def _mod_kernel(c_ref, w_ref, b_ref, o_ref):
    s = _silu(c_ref[...]).astype(BF16)
    o_ref[...] = jnp.dot(s, w_ref[...].astype(BF16), preferred_element_type=F32) + b_ref[...]


def _modulation(cvec, w_mod, b_mod):
    depth, d, n = w_mod.shape
    tn = 1024
    return pl.pallas_call(
        _mod_kernel,
        grid=(depth, n // tn),
        in_specs=[
            pl.BlockSpec((8, d), lambda l, j: (0, 0)),
            pl.BlockSpec((None, d, tn), lambda l, j: (l, 0, j)),
            pl.BlockSpec((None, 1, tn), lambda l, j: (l, 0, j)),
        ],
        out_specs=pl.BlockSpec((None, 8, tn), lambda l, j: (l, 0, j)),
        out_shape=jax.ShapeDtypeStruct((depth, 8, n), F32),
        compiler_params=_params("parallel", "parallel"),
        name="modulation",
    )(cvec, w_mod, b_mod.reshape(depth, 1, n))


def _ffn_kernel(mod_row, nf, x_ref, mod_ref, wg_ref, wu_ref, wd_ref, o_ref, xn_ref):
    f = pl.program_id(1)

    @pl.when(f == 0)
    def _():
        shift = mod_ref[mod_row:mod_row + 1, :]
        scale = mod_ref[mod_row + 1:mod_row + 2, :]
        xn_ref[...] = (_rms(x_ref[...]) * (1.0 + scale) + shift).astype(BF16)
        o_ref[...] = jnp.zeros_like(o_ref)

    xn = xn_ref[...]
    g = jnp.dot(xn, wg_ref[...].astype(BF16), preferred_element_type=F32)
    u = jnp.dot(xn, wu_ref[...].astype(BF16), preferred_element_type=F32)
    a = (_silu(g) * u).astype(BF16)
    o_ref[...] += jnp.dot(a, wd_ref[...].astype(BF16), preferred_element_type=F32)

    @pl.when(f == nf - 1)
    def _():
        gate = mod_ref[mod_row + 2:mod_row + 3, :]
        o_ref[...] = x_ref[...] + (0.5 * gate) * o_ref[...]


def _ffn(x2, mods, mod_at, layer, w_gu, w_down, mod_row, tm):
    m, d = x2.shape
    hidden = w_down.shape[1]
    nf = hidden // TF
    return pl.pallas_call(
        functools.partial(_ffn_kernel, mod_row, nf),
        grid=(m // tm, nf),
        in_specs=[
            pl.BlockSpec((tm, d), lambda i, f: (i, 0), pipeline_mode=pl.Buffered(1)),
            pl.BlockSpec((None, None, N_MOD, d), lambda i, f: (layer, mod_at(i), 0, 0)),
            pl.BlockSpec((None, d, TF), lambda i, f: (layer, 0, f)),
            pl.BlockSpec((None, d, TF), lambda i, f: (layer, 0, nf + f)),
            pl.BlockSpec((None, TF, d), lambda i, f: (layer, f, 0)),
        ],
        out_specs=pl.BlockSpec((tm, d), lambda i, f: (i, 0)),
        out_shape=jax.ShapeDtypeStruct((m, d), F32),
        scratch_shapes=[pltpu.VMEM((tm, d), BF16)],
        compiler_params=_params("parallel", "arbitrary"),
        name="ffn",
    )(x2, mods, w_gu, w_gu, w_down)


def _norm_lanes(y, width):
    y2 = y * y
    if width == HEAD_DIM:
        ms = jnp.mean(y2, axis=-1, keepdims=True)
    else:
        lane = lax.broadcasted_iota(jnp.int32, y.shape, 1)
        low = lane < width
        s_lo = jnp.sum(jnp.where(low, y2, 0.0), axis=-1, keepdims=True)
        s_hi = jnp.sum(jnp.where(low, 0.0, y2), axis=-1, keepdims=True)
        ms = jnp.where(low, s_lo, s_hi) * (1.0 / width)
    return y * lax.rsqrt(ms + RMS_EPS)


def _inproj_kernel(rope, x_ref, mod_ref, w_ref, gain_ref, *rest):
    if rope:
        ca_ref, sa1_ref, sa2_ref, cb_ref, sb1_ref, sb2_ref, o_ref, xn_ref = rest
    else:
        o_ref, xn_ref = rest
    j = pl.program_id(1)

    @pl.when(j == 0)
    def _():
        shift = mod_ref[3:4, :]
        scale = mod_ref[4:5, :]
        xn_ref[...] = (_rms(x_ref[...]) * (1.0 + scale) + shift).astype(BF16)

    y = jnp.dot(xn_ref[...], w_ref[...].astype(BF16), preferred_element_type=F32)
    heads_per_tile = TN // HEAD_DIM

    def emit(width, tables):
        for hh in range(heads_per_tile):
            sl = slice(hh * HEAD_DIM, (hh + 1) * HEAD_DIM)
            v = y[:, sl]
            if width:
                v = _norm_lanes(v, width) * gain_ref[:, sl]
            if tables is not None:
                c_ref, s1_ref, s2_ref, half = tables
                v = (v * c_ref[...]
                     + pltpu.roll(v, HEAD_DIM - half, 1) * s1_ref[...]
                     + pltpu.roll(v, half, 1) * s2_ref[...])
            o_ref[:, sl] = v.astype(o_ref.dtype)

    tile = lambda h: h // heads_per_tile
    in_a = j < tile(AV_H)
    in_b = (j >= tile(BQ_H)) & (j < tile(BV_H))
    in_c = (j >= tile(CQ_H)) & (j < tile(CV_H))
    plain = jnp.logical_not(in_a | in_b | in_c)

    @pl.when(in_a)
    def _():
        emit(HEAD_DIM, (ca_ref, sa1_ref, sa2_ref, HEAD_DIM // 4) if rope else None)

    @pl.when(in_b)
    def _():
        emit(B_SUB_DIM, (cb_ref, sb1_ref, sb2_ref, B_SUB_DIM // 4) if rope else None)

    @pl.when(in_c)
    def _():
        emit(HEAD_DIM, None)

    @pl.when(plain)
    def _():
        emit(0, None)


def _inproj(x2, mods, mod_at, layer, w_in, gains, tables, tm):
    m, d = x2.shape
    n = w_in.shape[2]
    rope = tables is not None
    in_specs = [
        pl.BlockSpec((tm, d), lambda i, j: (i, 0), pipeline_mode=pl.Buffered(1)),
        pl.BlockSpec((None, None, N_MOD, d), lambda i, j: (layer, mod_at(i), 0, 0)),
        pl.BlockSpec((None, d, TN), lambda i, j: (layer, 0, j)),
        pl.BlockSpec((1, TN), lambda i, j: (0, j)),
    ]
    args = [x2, mods, w_in, gains]
    if rope:
        pos_tiles = SEQ // tm
        for t in tables:
            in_specs.append(pl.BlockSpec((tm, HEAD_DIM), lambda i, j: (i % pos_tiles, 0)))
            args.append(t)
    return pl.pallas_call(
        functools.partial(_inproj_kernel, rope),
        grid=(m // tm, n // TN),
        in_specs=in_specs,
        out_specs=pl.BlockSpec((tm, TN), lambda i, j: (i, j)),
        out_shape=jax.ShapeDtypeStruct((m, n), BF16),
        scratch_shapes=[pltpu.VMEM((tm, d), BF16)],
        compiler_params=_params("parallel", "arbitrary"),
        name="inproj",
    )(*args)


def _rope_tables(half):
    t = jnp.arange(SEQ, dtype=jnp.int32)
    row = (t // GRID_W).astype(F32)[:, None]
    col = (t % GRID_W).astype(F32)[:, None]
    lane = np.arange(HEAD_DIM)
    freqs = ROPE_THETA ** (-jnp.asarray(lane % half, dtype=F32) / half)
    is_col = jnp.asarray((lane % (4 * half)) >= 2 * half)[None, :]
    ang = jnp.where(is_col, col, row) * freqs[None, :]
    first = jnp.asarray((lane % (2 * half)) < half)[None, :]
    cos = jnp.cos(ang)
    sin = jnp.sin(ang)
    return cos, jnp.where(first, -sin, 0.0), jnp.where(first, 0.0, sin)


_NT = (((1,), (1,)), ((), ()))


def _softmax_step(qs, k, v, state):
    s = lax.dot_general(qs, k, _NT, preferred_element_type=F32)
    m_cur = jnp.max(s, axis=-1, keepdims=True)
    if state is None:
        m_new = m_cur
        p = jnp.exp(s - m_new)
        l_new = jnp.sum(p, axis=-1, keepdims=True)
        acc = jnp.dot(p.astype(BF16), v, preferred_element_type=F32)
    else:
        m_old, l_old, acc_old = state
        m_new = jnp.maximum(m_old, m_cur)
        alpha = jnp.exp(m_old - m_new)
        p = jnp.exp(s - m_new)
        l_new = alpha * l_old + jnp.sum(p, axis=-1, keepdims=True)
        acc = alpha * acc_old + jnp.dot(p.astype(BF16), v, preferred_element_type=F32)
    return m_new, l_new, acc


def _attend(qs, kc_ref, vc_ref, kl_ref, vl_ref):
    state = _softmax_step(qs, kc_ref[...], vc_ref[...], None)
    if kl_ref is not None:
        n_chunks = kl_ref.shape[0] // KV_CHUNK

        def body(ci, st):
            start = pl.multiple_of(ci * KV_CHUNK, KV_CHUNK)
            return _softmax_step(qs, kl_ref[pl.ds(start, KV_CHUNK), :],
                                 vl_ref[pl.ds(start, KV_CHUNK), :], st)

        state = lax.fori_loop(0, n_chunks, body, state)
    _, l, acc = state
    return acc / l


def _gqa_kernel(group, latent, q_ref, kc_ref, vc_ref, *rest):
    if latent:
        kl_ref, vl_ref, o_ref = rest
    else:
        kl_ref = vl_ref = None
        (o_ref,) = rest
    tq = q_ref.shape[0]
    q = q_ref[...]
    qs = jnp.concatenate([q[:, g * HEAD_DIM:(g + 1) * HEAD_DIM] for g in range(group)], axis=0)
    o = _attend(qs, kc_ref, vc_ref, kl_ref, vl_ref)
    for g in range(group):
        o_ref[:, g * HEAD_DIM:(g + 1) * HEAD_DIM] = o[g * tq:(g + 1) * tq].astype(o_ref.dtype)


def _gqa(pq, pg, px, group, n_kv, q_h, k_h, v_h, tq):
    b, sq, _ = pq.shape
    latent = px is not None
    gw = group * HEAD_DIM
    in_specs = [
        pl.BlockSpec((None, tq, gw), lambda bi, g, qi: (bi, qi, q_h // group + g)),
        pl.BlockSpec((None, CTX_LEN, HEAD_DIM), lambda bi, g, qi: (bi, 0, k_h + g)),
        pl.BlockSpec((None, CTX_LEN, HEAD_DIM), lambda bi, g, qi: (bi, 0, v_h + g)),
    ]
    args = [pq, pg, pg]
    if latent:
        in_specs += [
            pl.BlockSpec((None, SEQ, HEAD_DIM), lambda bi, g, qi: (bi, 0, k_h + g)),
            pl.BlockSpec((None, SEQ, HEAD_DIM), lambda bi, g, qi: (bi, 0, v_h + g)),
        ]
        args += [px, px]
    return pl.pallas_call(
        functools.partial(_gqa_kernel, group, latent),
        grid=(b, n_kv, sq // tq),
        in_specs=in_specs,
        out_specs=pl.BlockSpec((None, tq, gw), lambda bi, g, qi: (bi, qi, g)),
        out_shape=jax.ShapeDtypeStruct((b, sq, n_kv * gw), F32),
        compiler_params=_params("parallel", "parallel", "arbitrary"),
        name="gqa_attention",
    )(*args)


def _diff_kernel(lam_init, latent, q_ref, lam_ref, kc_ref, vc_ref, *rest):
    if latent:
        kl_ref, vl_ref, o_ref = rest
    else:
        kl_ref = vl_ref = None
        (o_ref,) = rest
    tq = q_ref.shape[0]
    q = q_ref[...]
    low = lax.broadcasted_iota(jnp.int32, q.shape, 1) < B_SUB_DIM
    zero = jnp.zeros_like(q)
    qs = jnp.concatenate([jnp.where(low, q, zero), jnp.where(low, zero, q)], axis=0)
    o = _attend(qs, kc_ref, vc_ref, kl_ref, vl_ref)
    lv = lam_ref[...]
    lam = (jnp.exp(jnp.sum(lv[0:1] * lv[1:2], axis=-1, keepdims=True))
           - jnp.exp(jnp.sum(lv[2:3] * lv[3:4], axis=-1, keepdims=True)) + lam_init)
    o_ref[...] = (o[:tq] - lam * o[tq:]).astype(o_ref.dtype)


def _diff(pq, pg, px, lam_vecs, lam_init, tq):
    b, sq, _ = pq.shape
    latent = px is not None
    in_specs = [
        pl.BlockSpec((None, tq, HEAD_DIM), lambda bi, h, qi: (bi, qi, BQ_H + h)),
        pl.BlockSpec((4, B_SUB_DIM), lambda bi, h, qi: (0, 0)),
        pl.BlockSpec((None, CTX_LEN, HEAD_DIM), lambda bi, h, qi: (bi, 0, BK_H + h)),
        pl.BlockSpec((None, CTX_LEN, HEAD_DIM), lambda bi, h, qi: (bi, 0, BV_H + h)),
    ]
    args = [pq, lam_vecs, pg, pg]
    if latent:
        in_specs += [
            pl.BlockSpec((None, SEQ, HEAD_DIM), lambda bi, h, qi: (bi, 0, BK_H + h)),
            pl.BlockSpec((None, SEQ, HEAD_DIM), lambda bi, h, qi: (bi, 0, BV_H + h)),
        ]
        args += [px, px]
    return pl.pallas_call(
        functools.partial(_diff_kernel, lam_init, latent),
        grid=(b, B_HEADS, sq // tq),
        in_specs=in_specs,
        out_specs=pl.BlockSpec((None, tq, HEAD_DIM), lambda bi, h, qi: (bi, qi, h)),
        out_shape=jax.ShapeDtypeStruct((b, sq, B_WIDTH), F32),
        compiler_params=_params("parallel", "parallel", "arbitrary"),
        name="diff_attention",
    )(*args)


def _nbr_key_row(j):
    return jnp.clip(NB_ROWS * j - NA_ROWS // 2, 0, GRID_H - NB_KROWS)


def _nbr_kernel(q_ref, k_ref, v_ref, kc_ref, vc_ref, bias_ref, o_ref):
    j = pl.program_id(2)
    start = pl.multiple_of(_nbr_key_row(j) * GRID_W, NB_ROWS * GRID_W)
    q = q_ref[...]
    kw = k_ref[pl.ds(start, NB_K), :]
    vw = v_ref[pl.ds(start, NB_K), :]
    s_w = lax.dot_general(q, kw, _NT, preferred_element_type=F32) + bias_ref[...]
    s_c = lax.dot_general(q, kc_ref[...], _NT, preferred_element_type=F32)
    m = jnp.maximum(jnp.max(s_w, axis=-1, keepdims=True), jnp.max(s_c, axis=-1, keepdims=True))
    p_w = jnp.exp(s_w - m)
    p_c = jnp.exp(s_c - m)
    l = jnp.sum(p_w, axis=-1, keepdims=True) + jnp.sum(p_c, axis=-1, keepdims=True)
    o = (jnp.dot(p_w.astype(BF16), vw, preferred_element_type=F32)
         + jnp.dot(p_c.astype(BF16), vc_ref[...], preferred_element_type=F32))
    o_ref[...] = (o / l).astype(o_ref.dtype)


def _nbr_bias(rel_bias):
    n_blocks = GRID_H // NB_ROWS
    col = np.arange(GRID_W)
    cs = np.clip(col - NA_COLS // 2, 0, GRID_W - NA_COLS)
    col_ok = (col[None, :] >= cs[:, None]) & (col[None, :] < cs[:, None] + NA_COLS)
    dc = col[None, :] - col[:, None] + NA_COLS - 1
    onehot = (dc[None] == np.arange(2 * NA_COLS - 1)[:, None, None]) & col_ok[None]
    per_row = jnp.einsum('hrt,tck->hrck', rel_bias.astype(F32), jnp.asarray(onehot, F32),
                         precision=lax.Precision.HIGHEST)
    per_row = jnp.where(col_ok, per_row, MASK_VALUE)
    masked = jnp.full((C_HEADS, GRID_W, GRID_W), MASK_VALUE, F32)
    classes = []
    for j in (0, 1, n_blocks - 1):
        key_row0 = int(np.clip(NB_ROWS * j - NA_ROWS // 2, 0, GRID_H - NB_KROWS))
        strips = []
        for i in range(NB_ROWS):
            r = NB_ROWS * j + i
            rs = int(np.clip(r - NA_ROWS // 2, 0, GRID_H - NA_ROWS))
            blocks = []
            for a in range(NB_KROWS):
                kr = key_row0 + a
                in_window = rs <= kr < rs + NA_ROWS
                blocks.append(per_row[:, kr - r + NA_ROWS - 1] if in_window else masked)
            strips.append(jnp.concatenate(blocks, axis=-1))
        classes.append(jnp.concatenate(strips, axis=-2))
    return jnp.stack(classes)


def _nbr(px, pg, bias_cls):
    b = px.shape[0]
    n_blocks = GRID_H // NB_ROWS

    def cls(j):
        return jnp.where(j == 0, 0, jnp.where(j == n_blocks - 1, 2, 1))

    return pl.pallas_call(
        _nbr_kernel,
        grid=(b, C_HEADS, n_blocks),
        in_specs=[
            pl.BlockSpec((None, NB_Q, HEAD_DIM), lambda bi, h, j: (bi, j, CQ_H + h)),
            pl.BlockSpec((None, SEQ, HEAD_DIM), lambda bi, h, j: (bi, 0, CK_H + h)),
            pl.BlockSpec((None, SEQ, HEAD_DIM), lambda bi, h, j: (bi, 0, CV_H + h)),
            pl.BlockSpec((None, CTX_LEN, HEAD_DIM), lambda bi, h, j: (bi, 0, CK_H + h)),
            pl.BlockSpec((None, CTX_LEN, HEAD_DIM), lambda bi, h, j: (bi, 0, CV_H + h)),
            pl.BlockSpec((None, None, NB_Q, NB_K), lambda bi, h, j: (cls(j), h, 0, 0)),
        ],
        out_specs=pl.BlockSpec((None, NB_Q, HEAD_DIM), lambda bi, h, j: (bi, j, h)),
        out_shape=jax.ShapeDtypeStruct((b, SEQ, C_WIDTH), F32),
        compiler_params=_params("parallel", "parallel", "arbitrary"),
        name="nbr_attention",
    )(px, px, px, pg, pg, bias_cls)


def _outproj_kernel(b_scale, x_ref, mod_ref, a_ref, b_ref, c_ref, ag_ref, bg_ref, cg_ref,
                    w_ref, o_ref, mix_ref):
    j = pl.program_id(1)

    @pl.when(j == 0)
    def _():
        mix_ref[:, :A_WIDTH] = (_rms(a_ref[...]) * ag_ref[...]).astype(BF16)
        for h in range(B_HEADS):
            sl = slice(h * HEAD_DIM, (h + 1) * HEAD_DIM)
            bn = _rms(b_ref[:, sl]) * bg_ref[...] * b_scale
            mix_ref[:, A_WIDTH + h * HEAD_DIM:A_WIDTH + (h + 1) * HEAD_DIM] = bn.astype(BF16)
        mix_ref[:, A_WIDTH + B_WIDTH:] = (_rms(c_ref[...]) * cg_ref[...]).astype(BF16)

    y = jnp.dot(mix_ref[...], w_ref[...].astype(BF16), preferred_element_type=F32)
    o_ref[...] = x_ref[...] + mod_ref[5:6, :] * y


def _outproj(x2, mods, mod_at, layer, a2, b2, c2, a_gain, b_gain, c_gain, w_out, b_scale, tm):
    m, d = x2.shape
    return pl.pallas_call(
        functools.partial(_outproj_kernel, b_scale),
        grid=(m // tm, d // TN),
        in_specs=[
            pl.BlockSpec((tm, TN), lambda i, j: (i, j)),
            pl.BlockSpec((None, None, N_MOD, TN), lambda i, j: (layer, mod_at(i), 0, j)),
            pl.BlockSpec((tm, A_WIDTH), lambda i, j: (i, 0)),
            pl.BlockSpec((tm, B_WIDTH), lambda i, j: (i, 0)),
            pl.BlockSpec((tm, C_WIDTH), lambda i, j: (i, 0)),
            pl.BlockSpec((1, A_WIDTH), lambda i, j: (0, 0)),
            pl.BlockSpec((1, HEAD_DIM), lambda i, j: (0, 0)),
            pl.BlockSpec((1, C_WIDTH), lambda i, j: (0, 0)),
            pl.BlockSpec((None, MIX_WIDTH, TN), lambda i, j: (layer, 0, j)),
        ],
        out_specs=pl.BlockSpec((tm, TN), lambda i, j: (i, j)),
        out_shape=jax.ShapeDtypeStruct((m, d), F32),
        scratch_shapes=[pltpu.VMEM((tm, MIX_WIDTH), BF16)],
        compiler_params=_params("parallel", "arbitrary"),
        name="outproj",
    )(x2, mods, a2, b2, c2, a_gain, b_gain, c_gain, w_out)


def _column_gains(a_q, a_k, b_q, b_k, c_q, c_k):
    a_scale = HEAD_DIM ** -0.5
    b_scale = B_SUB_DIM ** -0.5
    parts = (
        jnp.tile(a_q * a_scale, A_HEADS), jnp.tile(a_k, A_KV_HEADS), jnp.ones((A_KV_WIDTH,), F32),
        jnp.tile(b_q * b_scale, 2 * B_HEADS), jnp.tile(b_k, 2 * B_HEADS), jnp.ones((B_WIDTH,), F32),
        jnp.tile(c_q * a_scale, C_HEADS), jnp.tile(c_k, C_HEADS), jnp.ones((C_WIDTH,), F32),
    )
    return jnp.concatenate(parts).reshape(1, IN_WIDTH)


def kernel(x, c, ctx, c_ctx, w_mod, b_mod, ffn1_w_gu, ffn1_w_down, ffn2_w_gu, ffn2_w_down,
           w_in, w_out, a_q_gain, a_k_gain, a_out_gain, b_q_gain, b_k_gain, b_lambda, b_out_gain,
           c_q_gain, c_k_gain, c_rel_bias, c_out_gain):
    b, s, d = x.shape
    n_ctx = ctx.shape[1]
    assert (b, s, d, n_ctx) == (2, SEQ, D_MODEL, CTX_LEN)

    cvec = jnp.zeros((8, d), F32).at[:b].set(c).at[b].set(c_ctx)
    mods = _modulation(cvec, w_mod, b_mod).reshape(DEPTH, 8, N_MOD, d)
    tables = _rope_tables(HEAD_DIM // 4) + _rope_tables(B_SUB_DIM // 4)

    x2 = x.reshape(b * s, d)
    g2 = ctx.reshape(b * n_ctx, d)
    tm_g = b * n_ctx
    lat_tiles = s // TM

    mod_x = lambda i: i // lat_tiles
    mod_g = lambda i: b

    for l in range(DEPTH):
        with_ctx = l < DEPTH - 1
        lam_init = 0.8 - 0.6 * math.exp(-0.3 * l)
        gains = _column_gains(a_q_gain[l], a_k_gain[l], b_q_gain[l], b_k_gain[l],
                              c_q_gain[l], c_k_gain[l])

        x2 = _ffn(x2, mods, mod_x, l, ffn1_w_gu, ffn1_w_down, 0, TM)
        g2 = _ffn(g2, mods, mod_g, l, ffn1_w_gu, ffn1_w_down, 0, tm_g)

        px = _inproj(x2, mods, mod_x, l, w_in, gains, tables, TM).reshape(b, s, IN_WIDTH)
        pg = _inproj(g2, mods, mod_g, l, w_in, gains, None, tm_g).reshape(b, n_ctx, IN_WIDTH)

        a_lat = _gqa(px, pg, px, A_GROUP, A_KV_HEADS, AQ_H, AK_H, AV_H, TQ_A)
        b_lat = _diff(px, pg, px, b_lambda[l], lam_init, TQ_B)
        c_lat = _nbr(px, pg, _nbr_bias(c_rel_bias[l]))

        out_gains = (a_out_gain[l].reshape(1, A_WIDTH), b_out_gain[l].reshape(1, HEAD_DIM),
                     c_out_gain[l].reshape(1, C_WIDTH))
        x2 = _outproj(x2, mods, mod_x, l, a_lat.reshape(b * s, A_WIDTH),
                      b_lat.reshape(b * s, B_WIDTH), c_lat.reshape(b * s, C_WIDTH),
                      *out_gains, w_out, 1.0 - lam_init, TM)
        x2 = _ffn(x2, mods, mod_x, l, ffn2_w_gu, ffn2_w_down, 6, TM)

        if with_ctx:
            a_ctx = _gqa(pg, pg, None, A_GROUP, A_KV_HEADS, AQ_H, AK_H, AV_H, n_ctx)
            b_ctx = _diff(pg, pg, None, b_lambda[l], lam_init, n_ctx)
            c_ctx = _gqa(pg, pg, None, 1, C_HEADS, CQ_H, CK_H, CV_H, n_ctx)
            g2 = _outproj(g2, mods, mod_g, l, a_ctx.reshape(tm_g, A_WIDTH),
                          b_ctx.reshape(tm_g, B_WIDTH), c_ctx.reshape(tm_g, C_WIDTH),
                          *out_gains, w_out, 1.0 - lam_init, tm_g)
            g2 = _ffn(g2, mods, mod_g, l, ffn2_w_gu, ffn2_w_down, 6, tm_g)

    return x2.reshape(b, s, d)
```

```python
import functools
import math

import numpy as np
import jax
import jax.numpy as jnp
from jax import lax
from jax.experimental import pallas as pl
from jax.experimental.pallas import tpu as pltpu

F32 = jnp.float32
BF16 = jnp.bfloat16

D_MODEL = 2048
SEQ = 4096
DEPTH = 2
GRID_W = 64
GRID_H = SEQ // GRID_W
CTX_LEN = 256
HEAD_DIM = 128
N_MOD = 9
FFN_HIDDEN = 5632
RMS_EPS = 1e-6
ROPE_THETA = 10000.0

A_HEADS = 6
A_KV_HEADS = 2
A_GROUP = A_HEADS // A_KV_HEADS
B_HEADS = 4
C_HEADS = 6
B_SUB_DIM = HEAD_DIM // 2
A_WIDTH = A_HEADS * HEAD_DIM
A_KV_WIDTH = A_KV_HEADS * HEAD_DIM
B_WIDTH = B_HEADS * HEAD_DIM
C_WIDTH = C_HEADS * HEAD_DIM
MIX_WIDTH = A_WIDTH + B_WIDTH + C_WIDTH
IN_WIDTH = A_WIDTH + 2 * A_KV_WIDTH + 3 * B_WIDTH + 3 * C_WIDTH
NA_ROWS = 8
NA_COLS = 16

AQ_H = 0
AK_H = AQ_H + A_HEADS
AV_H = AK_H + A_KV_HEADS
BQ_H = AV_H + A_KV_HEADS
BK_H = BQ_H + B_HEADS
BV_H = BK_H + B_HEADS
CQ_H = BV_H + B_HEADS
CK_H = CQ_H + C_HEADS
CV_H = CK_H + C_HEADS

VMEM_LIMIT = 56 * 1024 * 1024
MASK_VALUE = -1e30
LOG2E = math.log2(math.e)

TM = 1024
TF = 256
TN = 256
TQ_A = 256
TQ_B = 256
KV_CHUNK = 2048
NB_ROWS = 4
NB_Q = NB_ROWS * GRID_W
NB_KROWS = 12
NB_K = NB_KROWS * GRID_W


def _silu(v):
    return v / (1.0 + jnp.exp(-v))


def _rms(v):
    return v * lax.rsqrt(jnp.mean(v * v, axis=-1, keepdims=True) + RMS_EPS)


def _params(*sem):
    return pltpu.CompilerParams(dimension_semantics=sem, vmem_limit_bytes=VMEM_LIMIT)


def _mod_kernel(c_ref, w_ref, b_ref, o_ref):
    s = _silu(c_ref[...]).astype(BF16)
    o_ref[...] = jnp.dot(s, w_ref[...].astype(BF16), preferred_element_type=F32) + b_ref[...]


def _modulation(cvec, w_mod, b_mod):
    depth, d, n = w_mod.shape
    tn = 1024
    return pl.pallas_call(
        _mod_kernel,
        grid=(depth, n // tn),
        in_specs=[
            pl.BlockSpec((8, d), lambda l, j: (0, 0)),
            pl.BlockSpec((None, d, tn), lambda l, j: (l, 0, j)),
            pl.BlockSpec((None, 1, tn), lambda l, j: (l, 0, j)),
        ],
        out_specs=pl.BlockSpec((None, 8, tn), lambda l, j: (l, 0, j)),
        out_shape=jax.ShapeDtypeStruct((depth, 8, n), F32),
        compiler_params=_params("parallel", "parallel"),
        name="modulation",
    )(cvec, w_mod, b_mod.reshape(depth, 1, n))


def _ffn_kernel(mod_row, nf, x_ref, mod_ref, wg_ref, wu_ref, wd_ref, o_ref, xn_ref):
    f = pl.program_id(1)

    @pl.when(f == 0)
    def _():
        shift = mod_ref[mod_row:mod_row + 1, :]
        scale = mod_ref[mod_row + 1:mod_row + 2, :]
        xn_ref[...] = (_rms(x_ref[...]) * (1.0 + scale) + shift).astype(BF16)
        o_ref[...] = jnp.zeros_like(o_ref)

    xn = xn_ref[...]
    g = jnp.dot(xn, wg_ref[...].astype(BF16), preferred_element_type=F32)
    u = jnp.dot(xn, wu_ref[...].astype(BF16), preferred_element_type=F32)
    a = (_silu(g) * u).astype(BF16)
    o_ref[...] += jnp.dot(a, wd_ref[...].astype(BF16), preferred_element_type=F32)

    @pl.when(f == nf - 1)
    def _():
        gate = mod_ref[mod_row + 2:mod_row + 3, :]
        o_ref[...] = x_ref[...] + (0.5 * gate) * o_ref[...]


def _ffn(x2, mods, mod_at, layer, w_gu, w_down, mod_row, tm):
    m, d = x2.shape
    hidden = w_down.shape[1]
    nf = hidden // TF
    return pl.pallas_call(
        functools.partial(_ffn_kernel, mod_row, nf),
        grid=(m // tm, nf),
        in_specs=[
            pl.BlockSpec((tm, d), lambda i, f: (i, 0), pipeline_mode=pl.Buffered(1)),
            pl.BlockSpec((None, None, N_MOD, d), lambda i, f: (layer, mod_at(i), 0, 0)),
            pl.BlockSpec((None, d, TF), lambda i, f: (layer, 0, f)),
            pl.BlockSpec((None, d, TF), lambda i, f: (layer, 0, nf + f)),
            pl.BlockSpec((None, TF, d), lambda i, f: (layer, f, 0)),
        ],
        out_specs=pl.BlockSpec((tm, d), lambda i, f: (i, 0)),
        out_shape=jax.ShapeDtypeStruct((m, d), F32),
        scratch_shapes=[pltpu.VMEM((tm, d), BF16)],
        compiler_params=_params("parallel", "arbitrary"),
        name="ffn",
    )(x2, mods, w_gu, w_gu, w_down)


def _norm_lanes(y, width):
    y2 = y * y
    if width == HEAD_DIM:
        ms = jnp.mean(y2, axis=-1, keepdims=True)
    else:
        lane = lax.broadcasted_iota(jnp.int32, y.shape, 1)
        low = lane < width
        s_lo = jnp.sum(jnp.where(low, y2, 0.0), axis=-1, keepdims=True)
        s_hi = jnp.sum(jnp.where(low, 0.0, y2), axis=-1, keepdims=True)
        ms = jnp.where(low, s_lo, s_hi) * (1.0 / width)
    return y * lax.rsqrt(ms + RMS_EPS)


def _inproj_kernel(rope, x_ref, mod_ref, w_ref, gain_ref, *rest):
    if rope:
        ca_ref, sa1_ref, sa2_ref, cb_ref, sb1_ref, sb2_ref, o_ref, xn_ref = rest
    else:
        o_ref, xn_ref = rest
    j = pl.program_id(1)

    @pl.when(j == 0)
    def _():
        shift = mod_ref[3:4, :]
        scale = mod_ref[4:5, :]
        xn_ref[...] = (_rms(x_ref[...]) * (1.0 + scale) + shift).astype(BF16)

    y = jnp.dot(xn_ref[...], w_ref[...].astype(BF16), preferred_element_type=F32)
    heads_per_tile = TN // HEAD_DIM

    def emit(width, tables):
        for hh in range(heads_per_tile):
            sl = slice(hh * HEAD_DIM, (hh + 1) * HEAD_DIM)
            v = y[:, sl]
            if width:
                v = _norm_lanes(v, width) * gain_ref[:, sl]
            if tables is not None:
                c_ref, s1_ref, s2_ref, half = tables
                v = (v * c_ref[...]
                     + pltpu.roll(v, HEAD_DIM - half, 1) * s1_ref[...]
                     + pltpu.roll(v, half, 1) * s2_ref[...])
            o_ref[:, sl] = v.astype(o_ref.dtype)

    tile = lambda h: h // heads_per_tile
    in_a = j < tile(AV_H)
    in_b = (j >= tile(BQ_H)) & (j < tile(BV_H))
    in_c = (j >= tile(CQ_H)) & (j < tile(CV_H))
    plain = jnp.logical_not(in_a | in_b | in_c)

    @pl.when(in_a)
    def _():
        emit(HEAD_DIM, (ca_ref, sa1_ref, sa2_ref, HEAD_DIM // 4) if rope else None)

    @pl.when(in_b)
    def _():
        emit(B_SUB_DIM, (cb_ref, sb1_ref, sb2_ref, B_SUB_DIM // 4) if rope else None)

    @pl.when(in_c)
    def _():
        emit(HEAD_DIM, None)

    @pl.when(plain)
    def _():
        emit(0, None)


def _inproj(x2, mods, mod_at, layer, w_in, gains, tables, tm):
    m, d = x2.shape
    n = w_in.shape[2]
    rope = tables is not None
    in_specs = [
        pl.BlockSpec((tm, d), lambda i, j: (i, 0), pipeline_mode=pl.Buffered(1)),
        pl.BlockSpec((None, None, N_MOD, d), lambda i, j: (layer, mod_at(i), 0, 0)),
        pl.BlockSpec((None, d, TN), lambda i, j: (layer, 0, j)),
        pl.BlockSpec((1, TN), lambda i, j: (0, j)),
    ]
    args = [x2, mods, w_in, gains]
    if rope:
        pos_tiles = SEQ // tm
        for t in tables:
            in_specs.append(pl.BlockSpec((tm, HEAD_DIM), lambda i, j: (i % pos_tiles, 0)))
            args.append(t)
    return pl.pallas_call(
        functools.partial(_inproj_kernel, rope),
        grid=(m // tm, n // TN),
        in_specs=in_specs,
        out_specs=pl.BlockSpec((tm, TN), lambda i, j: (i, j)),
        out_shape=jax.ShapeDtypeStruct((m, n), BF16),
        scratch_shapes=[pltpu.VMEM((tm, d), BF16)],
        compiler_params=_params("parallel", "arbitrary"),
        name="inproj",
    )(*args)


def _rope_tables(half):
    t = jnp.arange(SEQ, dtype=jnp.int32)
    row = (t // GRID_W).astype(F32)[:, None]
    col = (t % GRID_W).astype(F32)[:, None]
    lane = np.arange(HEAD_DIM)
    freqs = ROPE_THETA ** (-jnp.asarray(lane % half, dtype=F32) / half)
    is_col = jnp.asarray((lane % (4 * half)) >= 2 * half)[None, :]
    ang = jnp.where(is_col, col, row) * freqs[None, :]
    first = jnp.asarray((lane % (2 * half)) < half)[None, :]
    cos = jnp.cos(ang)
    sin = jnp.sin(ang)
    return cos, jnp.where(first, -sin, 0.0), jnp.where(first, 0.0, sin)


_NT = (((1,), (1,)), ((), ()))


def _softmax_step(s, v, state):
    m_cur = jnp.max(s, axis=-1, keepdims=True)
    if state is None:
        m_new = m_cur
        p = jnp.exp2(s - m_new)
        l_new = jnp.sum(p, axis=-1, keepdims=True)
        acc = jnp.dot(p.astype(BF16), v, preferred_element_type=F32)
    else:
        m_old, l_old, acc_old = state
        m_new = jnp.maximum(m_old, m_cur)
        alpha = jnp.exp2(m_old - m_new)
        p = jnp.exp2(s - m_new)
        l_new = alpha * l_old + jnp.sum(p, axis=-1, keepdims=True)
        acc = alpha * acc_old + jnp.dot(p.astype(BF16), v, preferred_element_type=F32)
    return m_new, l_new, acc


def _attend(qs, kc_ref, vc_ref, kl_ref, vl_ref):
    def scores(k):
        return lax.dot_general(qs, k, _NT, preferred_element_type=F32)

    state = _softmax_step(scores(kc_ref[...]), vc_ref[...], None)
    if kl_ref is not None:
        n_chunks = kl_ref.shape[0] // KV_CHUNK

        def body(ci, st):
            start = pl.multiple_of(ci * KV_CHUNK, KV_CHUNK)
            return _softmax_step(scores(kl_ref[pl.ds(start, KV_CHUNK), :]),
                                 vl_ref[pl.ds(start, KV_CHUNK), :], st)

        state = lax.fori_loop(0, n_chunks, body, state)
    _, l, acc = state
    return acc / l


def _gqa_kernel(group, latent, q_ref, kc_ref, vc_ref, *rest):
    if latent:
        kl_ref, vl_ref, o_ref = rest
    else:
        kl_ref = vl_ref = None
        (o_ref,) = rest
    tq = q_ref.shape[0]
    q = q_ref[...]
    qs = jnp.concatenate([q[:, g * HEAD_DIM:(g + 1) * HEAD_DIM] for g in range(group)], axis=0)
    o = _attend(qs, kc_ref, vc_ref, kl_ref, vl_ref)
    for g in range(group):
        o_ref[:, g * HEAD_DIM:(g + 1) * HEAD_DIM] = o[g * tq:(g + 1) * tq].astype(o_ref.dtype)


def _gqa(pq, pg, px, group, n_kv, q_h, k_h, v_h, tq):
    b, sq, _ = pq.shape
    latent = px is not None
    gw = group * HEAD_DIM
    in_specs = [
        pl.BlockSpec((None, tq, gw), lambda bi, g, qi: (bi, qi, q_h // group + g)),
        pl.BlockSpec((None, CTX_LEN, HEAD_DIM), lambda bi, g, qi: (bi, 0, k_h + g)),
        pl.BlockSpec((None, CTX_LEN, HEAD_DIM), lambda bi, g, qi: (bi, 0, v_h + g)),
    ]
    args = [pq, pg, pg]
    if latent:
        in_specs += [
            pl.BlockSpec((None, SEQ, HEAD_DIM), lambda bi, g, qi: (bi, 0, k_h + g)),
            pl.BlockSpec((None, SEQ, HEAD_DIM), lambda bi, g, qi: (bi, 0, v_h + g)),
        ]
        args += [px, px]
    return pl.pallas_call(
        functools.partial(_gqa_kernel, group, latent),
        grid=(b, n_kv, sq // tq),
        in_specs=in_specs,
        out_specs=pl.BlockSpec((None, tq, gw), lambda bi, g, qi: (bi, qi, g)),
        out_shape=jax.ShapeDtypeStruct((b, sq, n_kv * gw), F32),
        compiler_params=_params("parallel", "parallel", "arbitrary"),
        name="gqa_attention",
    )(*args)


def _diff_kernel(lam_init, latent, q_ref, lam_ref, kc_ref, vc_ref, *rest):
    if latent:
        kl_ref, vl_ref, o_ref = rest
    else:
        kl_ref = vl_ref = None
        (o_ref,) = rest
    tq = q_ref.shape[0]
    q = q_ref[...]
    low = lax.broadcasted_iota(jnp.int32, q.shape, 1) < B_SUB_DIM
    zero = jnp.zeros_like(q)
    qs = jnp.concatenate([jnp.where(low, q, zero), jnp.where(low, zero, q)], axis=0)
    o = _attend(qs, kc_ref, vc_ref, kl_ref, vl_ref)
    lv = lam_ref[...]
    lam = (jnp.exp(jnp.sum(lv[0:1] * lv[1:2], axis=-1, keepdims=True))
           - jnp.exp(jnp.sum(lv[2:3] * lv[3:4], axis=-1, keepdims=True)) + lam_init)
    o_ref[...] = (o[:tq] - lam * o[tq:]).astype(o_ref.dtype)


def _diff(pq, pg, px, lam_vecs, lam_init, tq):
    b, sq, _ = pq.shape
    latent = px is not None
    in_specs = [
        pl.BlockSpec((None, tq, HEAD_DIM), lambda bi, h, qi: (bi, qi, BQ_H + h)),
        pl.BlockSpec((4, B_SUB_DIM), lambda bi, h, qi: (0, 0)),
        pl.BlockSpec((None, CTX_LEN, HEAD_DIM), lambda bi, h, qi: (bi, 0, BK_H + h)),
        pl.BlockSpec((None, CTX_LEN, HEAD_DIM), lambda bi, h, qi: (bi, 0, BV_H + h)),
    ]
    args = [pq, lam_vecs, pg, pg]
    if latent:
        in_specs += [
            pl.BlockSpec((None, SEQ, HEAD_DIM), lambda bi, h, qi: (bi, 0, BK_H + h)),
            pl.BlockSpec((None, SEQ, HEAD_DIM), lambda bi, h, qi: (bi, 0, BV_H + h)),
        ]
        args += [px, px]
    return pl.pallas_call(
        functools.partial(_diff_kernel, lam_init, latent),
        grid=(b, B_HEADS, sq // tq),
        in_specs=in_specs,
        out_specs=pl.BlockSpec((None, tq, HEAD_DIM), lambda bi, h, qi: (bi, qi, h)),
        out_shape=jax.ShapeDtypeStruct((b, sq, B_WIDTH), F32),
        compiler_params=_params("parallel", "parallel", "arbitrary"),
        name="diff_attention",
    )(*args)


def _nbr_key_row(j):
    return jnp.clip(NB_ROWS * j - NA_ROWS // 2, 0, GRID_H - NB_KROWS)


def _nbr_kernel(q_ref, k_ref, v_ref, kc_ref, vc_ref, bias_ref, o_ref):
    j = pl.program_id(2)
    start = pl.multiple_of(_nbr_key_row(j) * GRID_W, NB_ROWS * GRID_W)
    q = q_ref[...]
    kw = k_ref[pl.ds(start, NB_K), :]
    vw = v_ref[pl.ds(start, NB_K), :]
    s_w = lax.dot_general(q, kw, _NT, preferred_element_type=F32) + bias_ref[...]
    s_c = lax.dot_general(q, kc_ref[...], _NT, preferred_element_type=F32)
    m = jnp.maximum(jnp.max(s_w, axis=-1, keepdims=True), jnp.max(s_c, axis=-1, keepdims=True))
    p_w = jnp.exp2(s_w - m)
    p_c = jnp.exp2(s_c - m)
    l = jnp.sum(p_w, axis=-1, keepdims=True) + jnp.sum(p_c, axis=-1, keepdims=True)
    o = (jnp.dot(p_w.astype(BF16), vw, preferred_element_type=F32)
         + jnp.dot(p_c.astype(BF16), vc_ref[...], preferred_element_type=F32))
    o_ref[...] = (o / l).astype(o_ref.dtype)


def _nbr_bias(rel_bias):
    n_blocks = GRID_H // NB_ROWS
    col = np.arange(GRID_W)
    cs = np.clip(col - NA_COLS // 2, 0, GRID_W - NA_COLS)
    col_ok = (col[None, :] >= cs[:, None]) & (col[None, :] < cs[:, None] + NA_COLS)
    dc = col[None, :] - col[:, None] + NA_COLS - 1
    onehot = (dc[None] == np.arange(2 * NA_COLS - 1)[:, None, None]) & col_ok[None]
    per_row = jnp.einsum('hrt,tck->hrck', rel_bias.astype(F32) * LOG2E, jnp.asarray(onehot, F32),
                         precision=lax.Precision.HIGHEST)
    per_row = jnp.where(col_ok, per_row, MASK_VALUE)
    masked = jnp.full((C_HEADS, GRID_W, GRID_W), MASK_VALUE, F32)
    classes = []
    for j in (0, 1, n_blocks - 1):
        key_row0 = int(np.clip(NB_ROWS * j - NA_ROWS // 2, 0, GRID_H - NB_KROWS))
        strips = []
        for i in range(NB_ROWS):
            r = NB_ROWS * j + i
            rs = int(np.clip(r - NA_ROWS // 2, 0, GRID_H - NA_ROWS))
            blocks = []
            for a in range(NB_KROWS):
                kr = key_row0 + a
                in_window = rs <= kr < rs + NA_ROWS
                blocks.append(per_row[:, kr - r + NA_ROWS - 1] if in_window else masked)
            strips.append(jnp.concatenate(blocks, axis=-1))
        classes.append(jnp.concatenate(strips, axis=-2))
    return jnp.stack(classes)


def _nbr(px, pg, bias_cls):
    b = px.shape[0]
    n_blocks = GRID_H // NB_ROWS

    def cls(j):
        return jnp.where(j == 0, 0, jnp.where(j == n_blocks - 1, 2, 1))

    return pl.pallas_call(
        _nbr_kernel,
        grid=(b, C_HEADS, n_blocks),
        in_specs=[
            pl.BlockSpec((None, NB_Q, HEAD_DIM), lambda bi, h, j: (bi, j, CQ_H + h)),
            pl.BlockSpec((None, SEQ, HEAD_DIM), lambda bi, h, j: (bi, 0, CK_H + h)),
            pl.BlockSpec((None, SEQ, HEAD_DIM), lambda bi, h, j: (bi, 0, CV_H + h)),
            pl.BlockSpec((None, CTX_LEN, HEAD_DIM), lambda bi, h, j: (bi, 0, CK_H + h)),
            pl.BlockSpec((None, CTX_LEN, HEAD_DIM), lambda bi, h, j: (bi, 0, CV_H + h)),
            pl.BlockSpec((None, None, NB_Q, NB_K), lambda bi, h, j: (cls(j), h, 0, 0)),
        ],
        out_specs=pl.BlockSpec((None, NB_Q, HEAD_DIM), lambda bi, h, j: (bi, j, h)),
        out_shape=jax.ShapeDtypeStruct((b, SEQ, C_WIDTH), F32),
        compiler_params=_params("parallel", "parallel", "arbitrary"),
        name="nbr_attention",
    )(px, px, px, pg, pg, bias_cls)


def _outproj_kernel(b_scale, x_ref, mod_ref, a_ref, b_ref, c_ref, ag_ref, bg_ref, cg_ref,
                    w_ref, o_ref, mix_ref):
    j = pl.program_id(1)

    @pl.when(j == 0)
    def _():
        mix_ref[:, :A_WIDTH] = (_rms(a_ref[...]) * ag_ref[...]).astype(BF16)
        for h in range(B_HEADS):
            sl = slice(h * HEAD_DIM, (h + 1) * HEAD_DIM)
            bn = _rms(b_ref[:, sl]) * bg_ref[...] * b_scale
            mix_ref[:, A_WIDTH + h * HEAD_DIM:A_WIDTH + (h + 1) * HEAD_DIM] = bn.astype(BF16)
        mix_ref[:, A_WIDTH + B_WIDTH:] = (_rms(c_ref[...]) * cg_ref[...]).astype(BF16)

    y = jnp.dot(mix_ref[...], w_ref[...].astype(BF16), preferred_element_type=F32)
    o_ref[...] = x_ref[...] + mod_ref[5:6, :] * y


def _outproj(x2, mods, mod_at, layer, a2, b2, c2, a_gain, b_gain, c_gain, w_out, b_scale, tm):
    m, d = x2.shape
    return pl.pallas_call(
        functools.partial(_outproj_kernel, b_scale),
        grid=(m // tm, d // TN),
        in_specs=[
            pl.BlockSpec((tm, TN), lambda i, j: (i, j)),
            pl.BlockSpec((None, None, N_MOD, TN), lambda i, j: (layer, mod_at(i), 0, j)),
            pl.BlockSpec((tm, A_WIDTH), lambda i, j: (i, 0)),
            pl.BlockSpec((tm, B_WIDTH), lambda i, j: (i, 0)),
            pl.BlockSpec((tm, C_WIDTH), lambda i, j: (i, 0)),
            pl.BlockSpec((1, A_WIDTH), lambda i, j: (0, 0)),
            pl.BlockSpec((1, HEAD_DIM), lambda i, j: (0, 0)),
            pl.BlockSpec((1, C_WIDTH), lambda i, j: (0, 0)),
            pl.BlockSpec((None, MIX_WIDTH, TN), lambda i, j: (layer, 0, j)),
        ],
        out_specs=pl.BlockSpec((tm, TN), lambda i, j: (i, j)),
        out_shape=jax.ShapeDtypeStruct((m, d), F32),
        scratch_shapes=[pltpu.VMEM((tm, MIX_WIDTH), BF16)],
        compiler_params=_params("parallel", "arbitrary"),
        name="outproj",
    )(x2, mods, a2, b2, c2, a_gain, b_gain, c_gain, w_out)


def _column_gains(a_q, a_k, b_q, b_k, c_q, c_k):
    a_scale = LOG2E * HEAD_DIM ** -0.5
    b_scale = LOG2E * B_SUB_DIM ** -0.5
    parts = (
        jnp.tile(a_q * a_scale, A_HEADS), jnp.tile(a_k, A_KV_HEADS), jnp.ones((A_KV_WIDTH,), F32),
        jnp.tile(b_q * b_scale, 2 * B_HEADS), jnp.tile(b_k, 2 * B_HEADS), jnp.ones((B_WIDTH,), F32),
        jnp.tile(c_q * a_scale, C_HEADS), jnp.tile(c_k, C_HEADS), jnp.ones((C_WIDTH,), F32),
    )
    return jnp.concatenate(parts).reshape(1, IN_WIDTH)


def kernel(x, c, ctx, c_ctx, w_mod, b_mod, ffn1_w_gu, ffn1_w_down, ffn2_w_gu, ffn2_w_down,
           w_in, w_out, a_q_gain, a_k_gain, a_out_gain, b_q_gain, b_k_gain, b_lambda, b_out_gain,
           c_q_gain, c_k_gain, c_rel_bias, c_out_gain):
    b, s, d = x.shape
    n_ctx = ctx.shape[1]
    assert (b, s, d, n_ctx) == (2, SEQ, D_MODEL, CTX_LEN)

    cvec = jnp.zeros((8, d), F32).at[:b].set(c).at[b].set(c_ctx)
    mods = _modulation(cvec, w_mod, b_mod).reshape(DEPTH, 8, N_MOD, d)
    tables = _rope_tables(HEAD_DIM // 4) + _rope_tables(B_SUB_DIM // 4)

    x2 = x.reshape(b * s, d)
    g2 = ctx.reshape(b * n_ctx, d)
    tm_g = b * n_ctx
    lat_tiles = s // TM

    mod_x = lambda i: i // lat_tiles
    mod_g = lambda i: b

    for l in range(DEPTH):
        with_ctx = l < DEPTH - 1
        lam_init = 0.8 - 0.6 * math.exp(-0.3 * l)
        gains = _column_gains(a_q_gain[l], a_k_gain[l], b_q_gain[l], b_k_gain[l],
                              c_q_gain[l], c_k_gain[l])

        x2 = _ffn(x2, mods, mod_x, l, ffn1_w_gu, ffn1_w_down, 0, TM)
        g2 = _ffn(g2, mods, mod_g, l, ffn1_w_gu, ffn1_w_down, 0, tm_g)

        px = _inproj(x2, mods, mod_x, l, w_in, gains, tables, TM).reshape(b, s, IN_WIDTH)
        pg = _inproj(g2, mods, mod_g, l, w_in, gains, None, tm_g).reshape(b, n_ctx, IN_WIDTH)

        a_lat = _gqa(px, pg, px, A_GROUP, A_KV_HEADS, AQ_H, AK_H, AV_H, TQ_A)
        b_lat = _diff(px, pg, px, b_lambda[l], lam_init, TQ_B)
        c_lat = _nbr(px, pg, _nbr_bias(c_rel_bias[l]))

        out_gains = (a_out_gain[l].reshape(1, A_WIDTH), b_out_gain[l].reshape(1, HEAD_DIM),
                     c_out_gain[l].reshape(1, C_WIDTH))
        x2 = _outproj(x2, mods, mod_x, l, a_lat.reshape(b * s, A_WIDTH),
                      b_lat.reshape(b * s, B_WIDTH), c_lat.reshape(b * s, C_WIDTH),
                      *out_gains, w_out, 1.0 - lam_init, TM)
        x2 = _ffn(x2, mods, mod_x, l, ffn2_w_gu, ffn2_w_down, 6, TM)

        if with_ctx:
            a_ctx = _gqa(pg, pg, None, A_GROUP, A_KV_HEADS, AQ_H, AK_H, AV_H, n_ctx)
            b_ctx = _diff(pg, pg, None, b_lambda[l], lam_init, n_ctx)
            c_ctx = _gqa(pg, pg, None, 1, C_HEADS, CQ_H, CK_H, CV_H, n_ctx)
            g2 = _outproj(g2, mods, mod_g, l, a_ctx.reshape(tm_g, A_WIDTH),
                          b_ctx.reshape(tm_g, B_WIDTH), c_ctx.reshape(tm_g, C_WIDTH),
                          *out_gains, w_out, 1.0 - lam_init, tm_g)
            g2 = _ffn(g2, mods, mod_g, l, ffn2_w_gu, ffn2_w_down, 6, tm_g)

    return x2.reshape(b, s, d)
```

```python
import functools
import math

import numpy as np
import jax
import jax.numpy as jnp
from jax import lax
from jax.experimental import pallas as pl
from jax.experimental.pallas import tpu as pltpu

F32 = jnp.float32
BF16 = jnp.bfloat16

D_MODEL = 2048
SEQ = 4096
DEPTH = 2
GRID_W = 64
GRID_H = SEQ // GRID_W
CTX_LEN = 256
HEAD_DIM = 128
N_MOD = 9
FFN_HIDDEN = 5632
RMS_EPS = 1e-6
ROPE_THETA = 10000.0

A_HEADS = 6
A_KV_HEADS = 2
A_GROUP = A_HEADS // A_KV_HEADS
B_HEADS = 4
C_HEADS = 6
B_SUB_DIM = HEAD_DIM // 2
A_WIDTH = A_HEADS * HEAD_DIM
A_KV_WIDTH = A_KV_HEADS * HEAD_DIM
B_WIDTH = B_HEADS * HEAD_DIM
C_WIDTH = C_HEADS * HEAD_DIM
MIX_WIDTH = A_WIDTH + B_WIDTH + C_WIDTH
IN_WIDTH = A_WIDTH + 2 * A_KV_WIDTH + 3 * B_WIDTH + 3 * C_WIDTH
NA_ROWS = 8
NA_COLS = 16

CQ_H = 0
CK_H = CQ_H + C_HEADS
CV_H = CK_H + C_HEADS
AQ_H = CV_H + C_HEADS
AK_H = AQ_H + A_HEADS
AV_H = AK_H + A_KV_HEADS
BQ_H = AV_H + A_KV_HEADS
BK_H = BQ_H + B_HEADS
BV_H = BK_H + B_HEADS

VMEM_LIMIT = 56 * 1024 * 1024
MASK_VALUE = -1e30
LOG2E = math.log2(math.e)

TM = 1024
TM_OUT = 2048
TF = 256
TN = 256
ROW_GROUPS = 8
TQ_A = 256
TQ_B = 256
KV_CHUNK = 2048
NB_ROWS = 4
NB_Q = NB_ROWS * GRID_W
NB_KROWS = 12
NB_K = NB_KROWS * GRID_W

N_IN_TILES = IN_WIDTH // TN
SRC_SHIFT = (IN_WIDTH - 3 * C_WIDTH) // TN


def _silu(v):
    return v / (1.0 + jnp.exp(-v))


def _rms(v):
    return v * lax.rsqrt(jnp.mean(v * v, axis=-1, keepdims=True) + RMS_EPS)


def _head(h):
    return slice(h * HEAD_DIM, (h + 1) * HEAD_DIM)


def _params(*sem):
    return pltpu.CompilerParams(dimension_semantics=sem, vmem_limit_bytes=VMEM_LIMIT)


def _mod_kernel(c_ref, w_ref, b_ref, o_ref):
    s = _silu(c_ref[...]).astype(BF16)
    o_ref[...] = jnp.dot(s, w_ref[...].astype(BF16), preferred_element_type=F32) + b_ref[...]


def _modulation(cvec, w_mod, b_mod):
    depth, d, n = w_mod.shape
    tn = 1024
    return pl.pallas_call(
        _mod_kernel,
        grid=(depth, n // tn),
        in_specs=[
            pl.BlockSpec((8, d), lambda l, j: (0, 0)),
            pl.BlockSpec((None, d, tn), lambda l, j: (l, 0, j)),
            pl.BlockSpec((None, 1, tn), lambda l, j: (l, 0, j)),
        ],
        out_specs=pl.BlockSpec((None, 8, tn), lambda l, j: (l, 0, j)),
        out_shape=jax.ShapeDtypeStruct((depth, 8, n), F32),
        compiler_params=_params("parallel", "parallel"),
        name="modulation",
    )(cvec, w_mod, b_mod.reshape(depth, 1, n))


def _ffn_kernel(mod_row, nf, x_ref, mod_ref, wg_ref, wu_ref, wd_ref, o_ref, xn_ref):
    f = pl.program_id(1)

    @pl.when(f == 0)
    def _():
        shift = mod_ref[mod_row:mod_row + 1, :]
        scale = mod_ref[mod_row + 1:mod_row + 2, :]
        xn_ref[...] = (_rms(x_ref[...]) * (1.0 + scale) + shift).astype(BF16)
        o_ref[...] = jnp.zeros_like(o_ref)

    xn = xn_ref[...]
    g = jnp.dot(xn, wg_ref[...].astype(BF16), preferred_element_type=F32)
    u = jnp.dot(xn, wu_ref[...].astype(BF16), preferred_element_type=F32)
    a = (_silu(g) * u).astype(BF16)
    o_ref[...] += jnp.dot(a, wd_ref[...].astype(BF16), preferred_element_type=F32)

    @pl.when(f == nf - 1)
    def _():
        gate = mod_ref[mod_row + 2:mod_row + 3, :]
        o_ref[...] = x_ref[...] + (0.5 * gate) * o_ref[...]


def _ffn(x2, mods, mod_at, layer, w_gu, w_down, mod_row, tm):
    m, d = x2.shape
    hidden = w_down.shape[1]
    nf = hidden // TF
    return pl.pallas_call(
        functools.partial(_ffn_kernel, mod_row, nf),
        grid=(m // tm, nf),
        in_specs=[
            pl.BlockSpec((tm, d), lambda i, f: (i, 0), pipeline_mode=pl.Buffered(1)),
            pl.BlockSpec((None, None, N_MOD, d), lambda i, f: (layer, mod_at(i), 0, 0)),
            pl.BlockSpec((None, d, TF), lambda i, f: (layer, 0, f)),
            pl.BlockSpec((None, d, TF), lambda i, f: (layer, 0, nf + f)),
            pl.BlockSpec((None, TF, d), lambda i, f: (layer, f, 0)),
        ],
        out_specs=pl.BlockSpec((tm, d), lambda i, f: (i, 0)),
        out_shape=jax.ShapeDtypeStruct((m, d), F32),
        scratch_shapes=[pltpu.VMEM((tm, d), BF16)],
        compiler_params=_params("parallel", "arbitrary"),
        name="ffn",
    )(x2, mods, w_gu, w_gu, w_down)


def _norm_lanes(y, width):
    y2 = y * y
    if width == HEAD_DIM:
        ms = jnp.mean(y2, axis=-1, keepdims=True)
    else:
        lane = lax.broadcasted_iota(jnp.int32, y.shape, 1)
        low = lane < width
        s_lo = jnp.sum(jnp.where(low, y2, 0.0), axis=-1, keepdims=True)
        s_hi = jnp.sum(jnp.where(low, 0.0, y2), axis=-1, keepdims=True)
        ms = jnp.where(low, s_lo, s_hi) * (1.0 / width)
    return y * lax.rsqrt(ms + RMS_EPS)


def _inproj_kernel(rope, x_ref, mod_ref, w_ref, gain_ref, *rest):
    if rope:
        ca_ref, sa1_ref, sa2_ref, cb_ref, sb1_ref, sb2_ref, o_ref, xn_ref = rest
    else:
        o_ref, xn_ref = rest
    j = pl.program_id(1)

    @pl.when(j == 0)
    def _():
        shift = mod_ref[3:4, :]
        scale = mod_ref[4:5, :]
        xn_ref[...] = (_rms(x_ref[...]) * (1.0 + scale) + shift).astype(BF16)

    heads_per_tile = TN // HEAD_DIM
    group_rows = x_ref.shape[0] // ROW_GROUPS

    def emit(width, tables):
        w = w_ref[...].astype(BF16)
        for r in range(ROW_GROUPS):
            rows = slice(r * group_rows, (r + 1) * group_rows)
            y = jnp.dot(xn_ref[rows, :], w, preferred_element_type=F32)
            for hh in range(heads_per_tile):
                v = y[:, _head(hh)]
                if width:
                    v = _norm_lanes(v, width) * gain_ref[:, _head(hh)]
                if tables is not None:
                    c_ref, s1_ref, s2_ref, half = tables
                    v = (v * c_ref[rows, :]
                         + pltpu.roll(v, HEAD_DIM - half, 1) * s1_ref[rows, :]
                         + pltpu.roll(v, half, 1) * s2_ref[rows, :])
                o_ref[rows, _head(hh)] = v.astype(o_ref.dtype)

    tile = lambda h: h // heads_per_tile
    in_c = j < tile(CV_H)
    in_a = (j >= tile(AQ_H)) & (j < tile(AV_H))
    in_b = (j >= tile(BQ_H)) & (j < tile(BV_H))
    plain = jnp.logical_not(in_a | in_b | in_c)

    @pl.when(in_a)
    def _():
        emit(HEAD_DIM, (ca_ref, sa1_ref, sa2_ref, HEAD_DIM // 4) if rope else None)

    @pl.when(in_b)
    def _():
        emit(B_SUB_DIM, (cb_ref, sb1_ref, sb2_ref, B_SUB_DIM // 4) if rope else None)

    @pl.when(in_c)
    def _():
        emit(HEAD_DIM, None)

    @pl.when(plain)
    def _():
        emit(0, None)


def _inproj(x2, mods, mod_at, layer, w_in, gains, tables, tm):
    m, d = x2.shape
    rope = tables is not None
    src = lambda j: lax.rem(j + SRC_SHIFT, N_IN_TILES)
    in_specs = [
        pl.BlockSpec((tm, d), lambda i, j: (i, 0), pipeline_mode=pl.Buffered(1)),
        pl.BlockSpec((None, None, N_MOD, d), lambda i, j: (layer, mod_at(i), 0, 0)),
        pl.BlockSpec((None, d, TN), lambda i, j: (layer, 0, src(j))),
        pl.BlockSpec((1, TN), lambda i, j: (0, j)),
    ]
    args = [x2, mods, w_in, gains]
    if rope:
        pos_tiles = SEQ // tm
        for t in tables:
            in_specs.append(pl.BlockSpec((tm, HEAD_DIM), lambda i, j: (i % pos_tiles, 0)))
            args.append(t)
    return pl.pallas_call(
        functools.partial(_inproj_kernel, rope),
        grid=(m // tm, N_IN_TILES),
        in_specs=in_specs,
        out_specs=pl.BlockSpec((tm, TN), lambda i, j: (i, j)),
        out_shape=jax.ShapeDtypeStruct((m, IN_WIDTH), BF16),
        scratch_shapes=[pltpu.VMEM((tm, d), BF16)],
        compiler_params=_params("parallel", "arbitrary"),
        name="inproj",
    )(*args)


def _rope_tables(half):
    t = jnp.arange(SEQ, dtype=jnp.int32)
    row = (t // GRID_W).astype(F32)[:, None]
    col = (t % GRID_W).astype(F32)[:, None]
    lane = np.arange(HEAD_DIM)
    freqs = ROPE_THETA ** (-jnp.asarray(lane % half, dtype=F32) / half)
    is_col = jnp.asarray((lane % (4 * half)) >= 2 * half)[None, :]
    ang = jnp.where(is_col, col, row) * freqs[None, :]
    first = jnp.asarray((lane % (2 * half)) < half)[None, :]
    cos = jnp.cos(ang)
    sin = jnp.sin(ang)
    return cos, jnp.where(first, -sin, 0.0), jnp.where(first, 0.0, sin)


_NT = (((1,), (1,)), ((), ()))


def _softmax_step(s, v, state):
    m_cur = jnp.max(s, axis=-1, keepdims=True)
    if state is None:
        m_new = m_cur
        p = jnp.exp2(s - m_new)
        l_new = jnp.sum(p, axis=-1, keepdims=True)
        acc = jnp.dot(p.astype(BF16), v, preferred_element_type=F32)
    else:
        m_old, l_old, acc_old = state
        m_new = jnp.maximum(m_old, m_cur)
        alpha = jnp.exp2(m_old - m_new)
        p = jnp.exp2(s - m_new)
        l_new = alpha * l_old + jnp.sum(p, axis=-1, keepdims=True)
        acc = alpha * acc_old + jnp.dot(p.astype(BF16), v, preferred_element_type=F32)
    return m_new, l_new, acc


def _attend(qs, kc, vc, kl_ref, vl_ref):
    def scores(k):
        return lax.dot_general(qs, k, _NT, preferred_element_type=F32)

    state = _softmax_step(scores(kc), vc, None)
    if kl_ref is not None:
        for start in range(0, kl_ref.shape[0], KV_CHUNK):
            state = _softmax_step(scores(kl_ref[start:start + KV_CHUNK, :]),
                                  vl_ref[start:start + KV_CHUNK, :], state)
    _, l, acc = state
    return acc / l


def _store_normed(heads, gain_ref, o_ref):
    ms = sum(jnp.sum(o * o, axis=-1, keepdims=True) for o in heads) * (1.0 / (len(heads) * HEAD_DIM))
    r = lax.rsqrt(ms + RMS_EPS)
    for h, o in enumerate(heads):
        o_ref[:, _head(h)] = (o * r * gain_ref[:, _head(h)]).astype(o_ref.dtype)


def _gqa_heads(q, n_kv, group, pg_ref, k_h, v_h, kl_refs, vl_refs):
    tq = q.shape[0]
    heads = []
    for g in range(n_kv):
        qs = jnp.concatenate([q[:, _head(g * group + i)] for i in range(group)], axis=0)
        o = _attend(qs, pg_ref[:, _head(k_h + g)], pg_ref[:, _head(v_h + g)],
                    kl_refs[g] if kl_refs else None, vl_refs[g] if vl_refs else None)
        heads += [o[i * tq:(i + 1) * tq] for i in range(group)]
    return heads


def _gqa_lat_kernel(q_ref, pg_ref, *rest):
    kl_refs = rest[:A_KV_HEADS]
    vl_refs = rest[A_KV_HEADS:2 * A_KV_HEADS]
    gain_ref, o_ref = rest[2 * A_KV_HEADS:]
    heads = _gqa_heads(q_ref[...], A_KV_HEADS, A_GROUP, pg_ref, AK_H, AV_H, kl_refs, vl_refs)
    _store_normed(heads, gain_ref, o_ref)


def _gqa_ctx_kernel(n_kv, group, q_h, k_h, v_h, pg_ref, gain_ref, o_ref):
    q = pg_ref[:, q_h * HEAD_DIM:(q_h + n_kv * group) * HEAD_DIM]
    _store_normed(_gqa_heads(q, n_kv, group, pg_ref, k_h, v_h, None, None), gain_ref, o_ref)


def _gqa_lat(px, pg, gain):
    b = px.shape[0]
    head_spec = lambda h: pl.BlockSpec((None, SEQ, HEAD_DIM), lambda bi, qi: (bi, 0, h))
    kv_heads = [AK_H + g for g in range(A_KV_HEADS)] + [AV_H + g for g in range(A_KV_HEADS)]
    return pl.pallas_call(
        _gqa_lat_kernel,
        grid=(b, SEQ // TQ_A),
        in_specs=[
            pl.BlockSpec((None, TQ_A, A_WIDTH), lambda bi, qi: (bi, qi, AQ_H * HEAD_DIM // A_WIDTH)),
            pl.BlockSpec((None, CTX_LEN, IN_WIDTH), lambda bi, qi: (bi, 0, 0)),
            *[head_spec(h) for h in kv_heads],
            pl.BlockSpec((1, A_WIDTH), lambda bi, qi: (0, 0)),
        ],
        out_specs=pl.BlockSpec((None, TQ_A, A_WIDTH), lambda bi, qi: (bi, qi, 0)),
        out_shape=jax.ShapeDtypeStruct((b, SEQ, A_WIDTH), BF16),
        compiler_params=_params("parallel", "arbitrary"),
        name="gqa_attention",
    )(px, pg, *([px] * len(kv_heads)), gain)


def _gqa_ctx(pg, gain, n_kv, group, q_h, k_h, v_h):
    b = pg.shape[0]
    width = n_kv * group * HEAD_DIM
    return pl.pallas_call(
        functools.partial(_gqa_ctx_kernel, n_kv, group, q_h, k_h, v_h),
        grid=(b,),
        in_specs=[
            pl.BlockSpec((None, CTX_LEN, IN_WIDTH), lambda bi: (bi, 0, 0)),
            pl.BlockSpec((1, width), lambda bi: (0, 0)),
        ],
        out_specs=pl.BlockSpec((None, CTX_LEN, width), lambda bi: (bi, 0, 0)),
        out_shape=jax.ShapeDtypeStruct((b, CTX_LEN, width), BF16),
        compiler_params=_params("parallel"),
        name="ctx_attention",
    )(pg, gain)


def _diff_kernel(lam_init, latent, q_ref, lam_ref, gain_ref, kc_ref, vc_ref, *rest):
    if latent:
        kl_ref, vl_ref, o_ref = rest
    else:
        kl_ref = vl_ref = None
        (o_ref,) = rest
    tq = q_ref.shape[0]
    q = q_ref[...]
    low = lax.broadcasted_iota(jnp.int32, q.shape, 1) < B_SUB_DIM
    zero = jnp.zeros_like(q)
    qs = jnp.concatenate([jnp.where(low, q, zero), jnp.where(low, zero, q)], axis=0)
    o = _attend(qs, kc_ref[...], vc_ref[...], kl_ref, vl_ref)
    lv = lam_ref[...]
    lam = (jnp.exp(jnp.sum(lv[0:1] * lv[1:2], axis=-1, keepdims=True))
           - jnp.exp(jnp.sum(lv[2:3] * lv[3:4], axis=-1, keepdims=True)) + lam_init)
    o = _rms(o[:tq] - lam * o[tq:]) * gain_ref[...] * (1.0 - lam_init)
    o_ref[...] = o.astype(o_ref.dtype)


def _diff(pq, pg, px, lam_vecs, gain, lam_init, tq):
    b, sq, _ = pq.shape
    latent = px is not None
    in_specs = [
        pl.BlockSpec((None, tq, HEAD_DIM), lambda bi, h, qi: (bi, qi, BQ_H + h)),
        pl.BlockSpec((4, B_SUB_DIM), lambda bi, h, qi: (0, 0)),
        pl.BlockSpec((1, HEAD_DIM), lambda bi, h, qi: (0, 0)),
        pl.BlockSpec((None, CTX_LEN, HEAD_DIM), lambda bi, h, qi: (bi, 0, BK_H + h)),
        pl.BlockSpec((None, CTX_LEN, HEAD_DIM), lambda bi, h, qi: (bi, 0, BV_H + h)),
    ]
    args = [pq, lam_vecs, gain, pg, pg]
    if latent:
        in_specs += [
            pl.BlockSpec((None, SEQ, HEAD_DIM), lambda bi, h, qi: (bi, 0, BK_H + h)),
            pl.BlockSpec((None, SEQ, HEAD_DIM), lambda bi, h, qi: (bi, 0, BV_H + h)),
        ]
        args += [px, px]
    return pl.pallas_call(
        functools.partial(_diff_kernel, lam_init, latent),
        grid=(b, B_HEADS, sq // tq),
        in_specs=in_specs,
        out_specs=pl.BlockSpec((None, tq, HEAD_DIM), lambda bi, h, qi: (bi, qi, h)),
        out_shape=jax.ShapeDtypeStruct((b, sq, B_WIDTH), BF16),
        compiler_params=_params("parallel", "parallel", "arbitrary"),
        name="diff_attention",
    )(*args)


def _nbr_kernel(q_ref, k_ref, v_ref, pg_ref, bias_ref, gain_ref, o_ref):
    j = pl.program_id(1)
    key_row0 = jnp.clip(NB_ROWS * j - NA_ROWS // 2, 0, GRID_H - NB_KROWS)
    start = pl.multiple_of(key_row0 * GRID_W, NB_ROWS * GRID_W)
    heads = []
    for h in range(C_HEADS):
        q = q_ref[:, _head(h)]
        kw = k_ref[pl.ds(start, NB_K), _head(h)]
        vw = v_ref[pl.ds(start, NB_K), _head(h)]
        kc = pg_ref[:, _head(CK_H + h)]
        vc = pg_ref[:, _head(CV_H + h)]
        s_w = lax.dot_general(q, kw, _NT, preferred_element_type=F32) + bias_ref[h]
        s_c = lax.dot_general(q, kc, _NT, preferred_element_type=F32)
        m = jnp.maximum(jnp.max(s_w, axis=-1, keepdims=True), jnp.max(s_c, axis=-1, keepdims=True))
        p_w = jnp.exp2(s_w - m)
        p_c = jnp.exp2(s_c - m)
        l = jnp.sum(p_w, axis=-1, keepdims=True) + jnp.sum(p_c, axis=-1, keepdims=True)
        o = (jnp.dot(p_w.astype(BF16), vw, preferred_element_type=F32)
             + jnp.dot(p_c.astype(BF16), vc, preferred_element_type=F32))
        heads.append(o / l)
    _store_normed(heads, gain_ref, o_ref)


def _nbr_bias(rel_bias):
    n_blocks = GRID_H // NB_ROWS
    col = np.arange(GRID_W)
    cs = np.clip(col - NA_COLS // 2, 0, GRID_W - NA_COLS)
    col_ok = (col[None, :] >= cs[:, None]) & (col[None, :] < cs[:, None] + NA_COLS)
    dc = col[None, :] - col[:, None] + NA_COLS - 1
    onehot = (dc[None] == np.arange(2 * NA_COLS - 1)[:, None, None]) & col_ok[None]
    per_row = jnp.einsum('hrt,tck->hrck', rel_bias.astype(F32) * LOG2E, jnp.asarray(onehot, F32),
                         precision=lax.Precision.HIGHEST)
    per_row = jnp.where(col_ok, per_row, MASK_VALUE)
    masked = jnp.full((C_HEADS, GRID_W, GRID_W), MASK_VALUE, F32)
    classes = []
    for j in (0, 1, n_blocks - 1):
        key_row0 = int(np.clip(NB_ROWS * j - NA_ROWS // 2, 0, GRID_H - NB_KROWS))
        strips = []
        for i in range(NB_ROWS):
            r = NB_ROWS * j + i
            rs = int(np.clip(r - NA_ROWS // 2, 0, GRID_H - NA_ROWS))
            blocks = []
            for a in range(NB_KROWS):
                kr = key_row0 + a
                in_window = rs <= kr < rs + NA_ROWS
                blocks.append(per_row[:, kr - r + NA_ROWS - 1] if in_window else masked)
            strips.append(jnp.concatenate(blocks, axis=-1))
        classes.append(jnp.concatenate(strips, axis=-2))
    return jnp.stack(classes)


def _nbr(px, pg, bias_cls, gain):
    b = px.shape[0]
    n_blocks = GRID_H // NB_ROWS
    blk = lambda h: h * HEAD_DIM // C_WIDTH

    def cls(j):
        return jnp.where(j == 0, 0, jnp.where(j == n_blocks - 1, 2, 1))

    return pl.pallas_call(
        _nbr_kernel,
        grid=(b, n_blocks),
        in_specs=[
            pl.BlockSpec((None, NB_Q, C_WIDTH), lambda bi, j: (bi, j, blk(CQ_H))),
            pl.BlockSpec((None, SEQ, C_WIDTH), lambda bi, j: (bi, 0, blk(CK_H)),
                         pipeline_mode=pl.Buffered(1)),
            pl.BlockSpec((None, SEQ, C_WIDTH), lambda bi, j: (bi, 0, blk(CV_H)),
                         pipeline_mode=pl.Buffered(1)),
            pl.BlockSpec((None, CTX_LEN, IN_WIDTH), lambda bi, j: (bi, 0, 0)),
            pl.BlockSpec((None, C_HEADS, NB_Q, NB_K), lambda bi, j: (cls(j), 0, 0, 0)),
            pl.BlockSpec((1, C_WIDTH), lambda bi, j: (0, 0)),
        ],
        out_specs=pl.BlockSpec((None, NB_Q, C_WIDTH), lambda bi, j: (bi, j, 0)),
        out_shape=jax.ShapeDtypeStruct((b, SEQ, C_WIDTH), BF16),
        compiler_params=_params("parallel", "arbitrary"),
        name="nbr_attention",
    )(px, px, px, pg, bias_cls, gain)


def _outproj_kernel(x_ref, mod_ref, a_ref, b_ref, c_ref, w_ref, o_ref):
    y = (jnp.dot(a_ref[...], w_ref[:A_WIDTH, :].astype(BF16), preferred_element_type=F32)
         + jnp.dot(b_ref[...], w_ref[A_WIDTH:A_WIDTH + B_WIDTH, :].astype(BF16),
                   preferred_element_type=F32)
         + jnp.dot(c_ref[...], w_ref[A_WIDTH + B_WIDTH:, :].astype(BF16),
                   preferred_element_type=F32))
    o_ref[...] = x_ref[...] + mod_ref[5:6, :] * y


def _outproj(x2, mods, mod_at, layer, a2, b2, c2, w_out, tm):
    m, d = x2.shape
    return pl.pallas_call(
        _outproj_kernel,
        grid=(m // tm, d // TN),
        in_specs=[
            pl.BlockSpec((tm, TN), lambda i, j: (i, j)),
            pl.BlockSpec((None, None, N_MOD, TN), lambda i, j: (layer, mod_at(i), 0, j)),
            pl.BlockSpec((tm, A_WIDTH), lambda i, j: (i, 0)),
            pl.BlockSpec((tm, B_WIDTH), lambda i, j: (i, 0)),
            pl.BlockSpec((tm, C_WIDTH), lambda i, j: (i, 0)),
            pl.BlockSpec((None, MIX_WIDTH, TN), lambda i, j: (layer, 0, j)),
        ],
        out_specs=pl.BlockSpec((tm, TN), lambda i, j: (i, j)),
        out_shape=jax.ShapeDtypeStruct((m, d), F32),
        compiler_params=_params("parallel", "arbitrary"),
        name="outproj",
    )(x2, mods, a2, b2, c2, w_out)


def _column_gains(a_q, a_k, b_q, b_k, c_q, c_k):
    a_scale = LOG2E * HEAD_DIM ** -0.5
    b_scale = LOG2E * B_SUB_DIM ** -0.5
    parts = (
        jnp.tile(c_q * a_scale, C_HEADS), jnp.tile(c_k, C_HEADS), jnp.ones((C_WIDTH,), F32),
        jnp.tile(a_q * a_scale, A_HEADS), jnp.tile(a_k, A_KV_HEADS), jnp.ones((A_KV_WIDTH,), F32),
        jnp.tile(b_q * b_scale, 2 * B_HEADS), jnp.tile(b_k, 2 * B_HEADS), jnp.ones((B_WIDTH,), F32),
    )
    return jnp.concatenate(parts).reshape(1, IN_WIDTH)


def kernel(x, c, ctx, c_ctx, w_mod, b_mod, ffn1_w_gu, ffn1_w_down, ffn2_w_gu, ffn2_w_down,
           w_in, w_out, a_q_gain, a_k_gain, a_out_gain, b_q_gain, b_k_gain, b_lambda, b_out_gain,
           c_q_gain, c_k_gain, c_rel_bias, c_out_gain):
    b, s, d = x.shape
    n_ctx = ctx.shape[1]
    assert (b, s, d, n_ctx) == (2, SEQ, D_MODEL, CTX_LEN)

    cvec = jnp.zeros((8, d), F32).at[:b].set(c).at[b].set(c_ctx)
    mods = _modulation(cvec, w_mod, b_mod).reshape(DEPTH, 8, N_MOD, d)
    tables = _rope_tables(HEAD_DIM // 4) + _rope_tables(B_SUB_DIM // 4)

    x2 = x.reshape(b * s, d)
    g2 = ctx.reshape(b * n_ctx, d)
    tm_g = b * n_ctx

    mod_x = lambda i: i // (s // TM)
    mod_x_out = lambda i: i // (s // TM_OUT)
    mod_g = lambda i: b

    for l in range(DEPTH):
        with_ctx = l < DEPTH - 1
        lam_init = 0.8 - 0.6 * math.exp(-0.3 * l)
        gains = _column_gains(a_q_gain[l], a_k_gain[l], b_q_gain[l], b_k_gain[l],
                              c_q_gain[l], c_k_gain[l])
        a_gain = a_out_gain[l].reshape(1, A_WIDTH)
        b_gain = b_out_gain[l].reshape(1, HEAD_DIM)
        c_gain = c_out_gain[l].reshape(1, C_WIDTH)

        x2 = _ffn(x2, mods, mod_x, l, ffn1_w_gu, ffn1_w_down, 0, TM)
        g2 = _ffn(g2, mods, mod_g, l, ffn1_w_gu, ffn1_w_down, 0, tm_g)

        px = _inproj(x2, mods, mod_x, l, w_in, gains, tables, TM).reshape(b, s, IN_WIDTH)
        pg = _inproj(g2, mods, mod_g, l, w_in, gains, None, tm_g).reshape(b, n_ctx, IN_WIDTH)

        a_lat = _gqa_lat(px, pg, a_gain)
        b_lat = _diff(px, pg, px, b_lambda[l], b_gain, lam_init, TQ_B)
        c_lat = _nbr(px, pg, _nbr_bias(c_rel_bias[l]), c_gain)

        x2 = _outproj(x2, mods, mod_x_out, l, a_lat.reshape(b * s, A_WIDTH),
                      b_lat.reshape(b * s, B_WIDTH), c_lat.reshape(b * s, C_WIDTH), w_out, TM_OUT)
        x2 = _ffn(x2, mods, mod_x, l, ffn2_w_gu, ffn2_w_down, 6, TM)

        if with_ctx:
            a_ctx = _gqa_ctx(pg, a_gain, A_KV_HEADS, A_GROUP, AQ_H, AK_H, AV_H)
            b_ctx = _diff(pg, pg, None, b_lambda[l], b_gain, lam_init, n_ctx)
            c_ctx = _gqa_ctx(pg, c_gain, C_HEADS, 1, CQ_H, CK_H, CV_H)
            g2 = _outproj(g2, mods, mod_g, l, a_ctx.reshape(tm_g, A_WIDTH),
                          b_ctx.reshape(tm_g, B_WIDTH), c_ctx.reshape(tm_g, C_WIDTH), w_out, tm_g)
            g2 = _ffn(g2, mods, mod_g, l, ffn2_w_gu, ffn2_w_down, 6, tm_g)

    return x2.reshape(b, s, d)
```

```python
import functools
import math

import numpy as np
import jax
import jax.numpy as jnp
from jax import lax
from jax.experimental import pallas as pl
from jax.experimental.pallas import tpu as pltpu

F32 = jnp.float32
BF16 = jnp.bfloat16

D_MODEL = 2048
SEQ = 4096
DEPTH = 2
GRID_W = 64
GRID_H = SEQ // GRID_W
CTX_LEN = 256
HEAD_DIM = 128
N_MOD = 9
FFN_HIDDEN = 5632
RMS_EPS = 1e-6
ROPE_THETA = 10000.0

A_HEADS = 6
A_KV_HEADS = 2
A_GROUP = A_HEADS // A_KV_HEADS
B_HEADS = 4
C_HEADS = 6
B_SUB_DIM = HEAD_DIM // 2
A_WIDTH = A_HEADS * HEAD_DIM
A_KV_WIDTH = A_KV_HEADS * HEAD_DIM
B_WIDTH = B_HEADS * HEAD_DIM
C_WIDTH = C_HEADS * HEAD_DIM
MIX_WIDTH = A_WIDTH + B_WIDTH + C_WIDTH
IN_WIDTH = A_WIDTH + 2 * A_KV_WIDTH + 3 * B_WIDTH + 3 * C_WIDTH
NA_ROWS = 8
NA_COLS = 16

CQ_H = 0
CK_H = CQ_H + C_HEADS
CV_H = CK_H + C_HEADS
AQ_H = CV_H + C_HEADS
AK_H = AQ_H + A_HEADS
AV_H = AK_H + A_KV_HEADS
BQ_H = AV_H + A_KV_HEADS
BK_H = BQ_H + B_HEADS
BV_H = BK_H + B_HEADS

VMEM_LIMIT = 56 * 1024 * 1024
MASK_VALUE = -1e30
LOG2E = math.log2(math.e)

TM = 1024
TM_OUT = 2048
TF = 256
TN = 256
ROW_GROUPS = 8
EDGE_GROUPS = 4
TQ_A = 256
TQ_B = 256
KV_CHUNK = 2048
NB_ROWS = 4
NB_Q = NB_ROWS * GRID_W
NB_KROWS = 12
NB_K = NB_KROWS * GRID_W

N_IN_TILES = IN_WIDTH // TN
SRC_SHIFT = (IN_WIDTH - 3 * C_WIDTH) // TN


def _silu(v):
    return v / (1.0 + jnp.exp(-v))


def _rms(v):
    return v * lax.rsqrt(jnp.mean(v * v, axis=-1, keepdims=True) + RMS_EPS)


def _head(h):
    return slice(h * HEAD_DIM, (h + 1) * HEAD_DIM)


def _params(*sem):
    return pltpu.CompilerParams(dimension_semantics=sem, vmem_limit_bytes=VMEM_LIMIT)


def _mod_kernel(c_ref, w_ref, b_ref, o_ref):
    s = _silu(c_ref[...]).astype(BF16)
    o_ref[...] = jnp.dot(s, w_ref[...].astype(BF16), preferred_element_type=F32) + b_ref[...]


def _modulation(cvec, w_mod, b_mod):
    depth, d, n = w_mod.shape
    tn = 1024
    return pl.pallas_call(
        _mod_kernel,
        grid=(depth, n // tn),
        in_specs=[
            pl.BlockSpec((8, d), lambda l, j: (0, 0)),
            pl.BlockSpec((None, d, tn), lambda l, j: (l, 0, j)),
            pl.BlockSpec((None, 1, tn), lambda l, j: (l, 0, j)),
        ],
        out_specs=pl.BlockSpec((None, 8, tn), lambda l, j: (l, 0, j)),
        out_shape=jax.ShapeDtypeStruct((depth, 8, n), F32),
        compiler_params=_params("parallel", "parallel"),
        name="modulation",
    )(cvec, w_mod, b_mod.reshape(depth, 1, n))


def _ffn_kernel(mod_row, nf, x_ref, mod_ref, wg_ref, wu_ref, wd_ref, o_ref, xn_ref):
    f = pl.program_id(1)
    tm = x_ref.shape[0]

    def step(first, last, groups):
        wg = wg_ref[...].astype(BF16)
        wu = wu_ref[...].astype(BF16)
        wd = wd_ref[...].astype(BF16)
        for r in range(groups):
            rows = slice(r * (tm // groups), (r + 1) * (tm // groups))
            if first:
                shift = mod_ref[mod_row:mod_row + 1, :]
                scale = mod_ref[mod_row + 1:mod_row + 2, :]
                xn = (_rms(x_ref[rows, :]) * (1.0 + scale) + shift).astype(BF16)
                xn_ref[rows, :] = xn
            else:
                xn = xn_ref[rows, :]
            g = jnp.dot(xn, wg, preferred_element_type=F32)
            u = jnp.dot(xn, wu, preferred_element_type=F32)
            a = (_silu(g) * u).astype(BF16)
            y = jnp.dot(a, wd, preferred_element_type=F32)
            if not first:
                y = o_ref[rows, :] + y
            if last:
                gate = mod_ref[mod_row + 2:mod_row + 3, :]
                y = x_ref[rows, :] + (0.5 * gate) * y
            o_ref[rows, :] = y

    pl.when(f == 0)(lambda: step(True, False, EDGE_GROUPS))
    pl.when((f > 0) & (f < nf - 1))(lambda: step(False, False, 1))
    pl.when(f == nf - 1)(lambda: step(False, True, EDGE_GROUPS))


def _ffn(x2, mods, mod_at, layer, w_gu, w_down, mod_row, tm):
    m, d = x2.shape
    hidden = w_down.shape[1]
    nf = hidden // TF
    return pl.pallas_call(
        functools.partial(_ffn_kernel, mod_row, nf),
        grid=(m // tm, nf),
        in_specs=[
            pl.BlockSpec((tm, d), lambda i, f: (i, 0), pipeline_mode=pl.Buffered(1)),
            pl.BlockSpec((None, None, N_MOD, d), lambda i, f: (layer, mod_at(i), 0, 0)),
            pl.BlockSpec((None, d, TF), lambda i, f: (layer, 0, f)),
            pl.BlockSpec((None, d, TF), lambda i, f: (layer, 0, nf + f)),
            pl.BlockSpec((None, TF, d), lambda i, f: (layer, f, 0)),
        ],
        out_specs=pl.BlockSpec((tm, d), lambda i, f: (i, 0)),
        out_shape=jax.ShapeDtypeStruct((m, d), F32),
        scratch_shapes=[pltpu.VMEM((tm, d), BF16)],
        compiler_params=_params("parallel", "arbitrary"),
        name="ffn",
    )(x2, mods, w_gu, w_gu, w_down)


def _norm_lanes(y, width):
    y2 = y * y
    if width == HEAD_DIM:
        ms = jnp.mean(y2, axis=-1, keepdims=True)
    else:
        lane = lax.broadcasted_iota(jnp.int32, y.shape, 1)
        low = lane < width
        s_lo = jnp.sum(jnp.where(low, y2, 0.0), axis=-1, keepdims=True)
        s_hi = jnp.sum(jnp.where(low, 0.0, y2), axis=-1, keepdims=True)
        ms = jnp.where(low, s_lo, s_hi) * (1.0 / width)
    return y * lax.rsqrt(ms + RMS_EPS)


def _inproj_kernel(rope, x_ref, mod_ref, w_ref, gain_ref, *rest):
    if rope:
        ca_ref, sa1_ref, sa2_ref, cb_ref, sb1_ref, sb2_ref, o_ref, xn_ref = rest
    else:
        o_ref, xn_ref = rest
    j = pl.program_id(1)
    heads_per_tile = TN // HEAD_DIM
    group_rows = x_ref.shape[0] // ROW_GROUPS

    def emit(width, tables, first=False):
        w = w_ref[...].astype(BF16)
        for r in range(ROW_GROUPS):
            rows = slice(r * group_rows, (r + 1) * group_rows)
            if first:
                shift = mod_ref[3:4, :]
                scale = mod_ref[4:5, :]
                xn = (_rms(x_ref[rows, :]) * (1.0 + scale) + shift).astype(BF16)
                xn_ref[rows, :] = xn
            else:
                xn = xn_ref[rows, :]
            y = jnp.dot(xn, w, preferred_element_type=F32)
            for hh in range(heads_per_tile):
                v = y[:, _head(hh)]
                if width:
                    v = _norm_lanes(v, width) * gain_ref[:, _head(hh)]
                if tables is not None:
                    c_ref, s1_ref, s2_ref, half = tables
                    v = (v * c_ref[rows, :]
                         + pltpu.roll(v, HEAD_DIM - half, 1) * s1_ref[rows, :]
                         + pltpu.roll(v, half, 1) * s2_ref[rows, :])
                o_ref[rows, _head(hh)] = v.astype(o_ref.dtype)

    tile = lambda h: h // heads_per_tile
    in_c = j < tile(CV_H)
    in_a = (j >= tile(AQ_H)) & (j < tile(AV_H))
    in_b = (j >= tile(BQ_H)) & (j < tile(BV_H))
    plain = jnp.logical_not(in_a | in_b | in_c)

    @pl.when(in_a)
    def _():
        emit(HEAD_DIM, (ca_ref, sa1_ref, sa2_ref, HEAD_DIM // 4) if rope else None)

    @pl.when(in_b)
    def _():
        emit(B_SUB_DIM, (cb_ref, sb1_ref, sb2_ref, B_SUB_DIM // 4) if rope else None)

    @pl.when(j == 0)
    def _():
        emit(HEAD_DIM, None, first=True)

    @pl.when(in_c & (j > 0))
    def _():
        emit(HEAD_DIM, None)

    @pl.when(plain)
    def _():
        emit(0, None)


def _inproj(x2, mods, mod_at, layer, w_in, gains, tables, tm):
    m, d = x2.shape
    rope = tables is not None
    src = lambda j: lax.rem(j + SRC_SHIFT, N_IN_TILES)
    in_specs = [
        pl.BlockSpec((tm, d), lambda i, j: (i, 0), pipeline_mode=pl.Buffered(1)),
        pl.BlockSpec((None, None, N_MOD, d), lambda i, j: (layer, mod_at(i), 0, 0)),
        pl.BlockSpec((None, d, TN), lambda i, j: (layer, 0, src(j))),
        pl.BlockSpec((1, TN), lambda i, j: (0, j)),
    ]
    args = [x2, mods, w_in, gains]
    if rope:
        pos_tiles = SEQ // tm
        for t in tables:
            in_specs.append(pl.BlockSpec((tm, HEAD_DIM), lambda i, j: (i % pos_tiles, 0)))
            args.append(t)
    return pl.pallas_call(
        functools.partial(_inproj_kernel, rope),
        grid=(m // tm, N_IN_TILES),
        in_specs=in_specs,
        out_specs=pl.BlockSpec((tm, TN), lambda i, j: (i, j)),
        out_shape=jax.ShapeDtypeStruct((m, IN_WIDTH), BF16),
        scratch_shapes=[pltpu.VMEM((tm, d), BF16)],
        compiler_params=_params("parallel", "arbitrary"),
        name="inproj",
    )(*args)


def _rope_tables(half):
    t = jnp.arange(SEQ, dtype=jnp.int32)
    row = (t // GRID_W).astype(F32)[:, None]
    col = (t % GRID_W).astype(F32)[:, None]
    lane = np.arange(HEAD_DIM)
    freqs = ROPE_THETA ** (-jnp.asarray(lane % half, dtype=F32) / half)
    is_col = jnp.asarray((lane % (4 * half)) >= 2 * half)[None, :]
    ang = jnp.where(is_col, col, row) * freqs[None, :]
    first = jnp.asarray((lane % (2 * half)) < half)[None, :]
    cos = jnp.cos(ang)
    sin = jnp.sin(ang)
    return cos, jnp.where(first, -sin, 0.0), jnp.where(first, 0.0, sin)


_NT = (((1,), (1,)), ((), ()))


def _softmax_step(s, v, state):
    m_cur = jnp.max(s, axis=-1, keepdims=True)
    if state is None:
        m_new = m_cur
        p = jnp.exp2(s - m_new)
        l_new = jnp.sum(p, axis=-1, keepdims=True)
        acc = jnp.dot(p.astype(BF16), v, preferred_element_type=F32)
    else:
        m_old, l_old, acc_old = state
        m_new = jnp.maximum(m_old, m_cur)
        alpha = jnp.exp2(m_old - m_new)
        p = jnp.exp2(s - m_new)
        l_new = alpha * l_old + jnp.sum(p, axis=-1, keepdims=True)
        acc = alpha * acc_old + jnp.dot(p.astype(BF16), v, preferred_element_type=F32)
    return m_new, l_new, acc


def _attend(qs, kc, vc, kl_ref, vl_ref):
    def scores(k):
        return lax.dot_general(qs, k, _NT, preferred_element_type=F32)

    state = _softmax_step(scores(kc), vc, None)
    if kl_ref is not None:
        for start in range(0, kl_ref.shape[0], KV_CHUNK):
            state = _softmax_step(scores(kl_ref[start:start + KV_CHUNK, :]),
                                  vl_ref[start:start + KV_CHUNK, :], state)
    _, l, acc = state
    return acc / l


def _store_normed(heads, gain_ref, o_ref):
    ms = sum(jnp.sum(o * o, axis=-1, keepdims=True) for o in heads) * (1.0 / (len(heads) * HEAD_DIM))
    r = lax.rsqrt(ms + RMS_EPS)
    for h, o in enumerate(heads):
        o_ref[:, _head(h)] = (o * r * gain_ref[:, _head(h)]).astype(o_ref.dtype)


def _gqa_heads(q, n_kv, group, pg_ref, k_h, v_h, kl_refs, vl_refs):
    tq = q.shape[0]
    heads = []
    for g in range(n_kv):
        qs = jnp.concatenate([q[:, _head(g * group + i)] for i in range(group)], axis=0)
        o = _attend(qs, pg_ref[:, _head(k_h + g)], pg_ref[:, _head(v_h + g)],
                    kl_refs[g] if kl_refs else None, vl_refs[g] if vl_refs else None)
        heads += [o[i * tq:(i + 1) * tq] for i in range(group)]
    return heads


def _gqa_lat_kernel(q_ref, pg_ref, *rest):
    kl_refs = rest[:A_KV_HEADS]
    vl_refs = rest[A_KV_HEADS:2 * A_KV_HEADS]
    gain_ref, o_ref = rest[2 * A_KV_HEADS:]
    heads = _gqa_heads(q_ref[...], A_KV_HEADS, A_GROUP, pg_ref, AK_H, AV_H, kl_refs, vl_refs)
    _store_normed(heads, gain_ref, o_ref)


def _gqa_ctx_kernel(n_kv, group, q_h, k_h, v_h, pg_ref, gain_ref, o_ref):
    q = pg_ref[:, q_h * HEAD_DIM:(q_h + n_kv * group) * HEAD_DIM]
    _store_normed(_gqa_heads(q, n_kv, group, pg_ref, k_h, v_h, None, None), gain_ref, o_ref)


def _gqa_lat(px, pg, gain):
    b = px.shape[0]
    head_spec = lambda h: pl.BlockSpec((None, SEQ, HEAD_DIM), lambda bi, qi: (bi, 0, h))
    kv_heads = [AK_H + g for g in range(A_KV_HEADS)] + [AV_H + g for g in range(A_KV_HEADS)]
    return pl.pallas_call(
        _gqa_lat_kernel,
        grid=(b, SEQ // TQ_A),
        in_specs=[
            pl.BlockSpec((None, TQ_A, A_WIDTH), lambda bi, qi: (bi, qi, AQ_H * HEAD_DIM // A_WIDTH)),
            pl.BlockSpec((None, CTX_LEN, IN_WIDTH), lambda bi, qi: (bi, 0, 0)),
            *[head_spec(h) for h in kv_heads],
            pl.BlockSpec((1, A_WIDTH), lambda bi, qi: (0, 0)),
        ],
        out_specs=pl.BlockSpec((None, TQ_A, A_WIDTH), lambda bi, qi: (bi, qi, 0)),
        out_shape=jax.ShapeDtypeStruct((b, SEQ, A_WIDTH), BF16),
        compiler_params=_params("parallel", "arbitrary"),
        name="gqa_attention",
    )(px, pg, *([px] * len(kv_heads)), gain)


def _gqa_ctx(pg, gain, n_kv, group, q_h, k_h, v_h):
    b = pg.shape[0]
    width = n_kv * group * HEAD_DIM
    return pl.pallas_call(
        functools.partial(_gqa_ctx_kernel, n_kv, group, q_h, k_h, v_h),
        grid=(b,),
        in_specs=[
            pl.BlockSpec((None, CTX_LEN, IN_WIDTH), lambda bi: (bi, 0, 0)),
            pl.BlockSpec((1, width), lambda bi: (0, 0)),
        ],
        out_specs=pl.BlockSpec((None, CTX_LEN, width), lambda bi: (bi, 0, 0)),
        out_shape=jax.ShapeDtypeStruct((b, CTX_LEN, width), BF16),
        compiler_params=_params("parallel"),
        name="ctx_attention",
    )(pg, gain)


def _diff_heads(q, lam_init, lam_ref, gain_ref, pg_ref, kl_refs, vl_refs, o_ref):
    tq = q.shape[0]
    lv = lam_ref[...]
    lam = (jnp.exp(jnp.sum(lv[0:1] * lv[1:2], axis=-1, keepdims=True))
           - jnp.exp(jnp.sum(lv[2:3] * lv[3:4], axis=-1, keepdims=True)) + lam_init)
    low = lax.broadcasted_iota(jnp.int32, (tq, HEAD_DIM), 1) < B_SUB_DIM
    zero = jnp.zeros((tq, HEAD_DIM), q.dtype)
    for h in range(B_HEADS):
        qh = q[:, _head(h)]
        qs = jnp.concatenate([jnp.where(low, qh, zero), jnp.where(low, zero, qh)], axis=0)
        o = _attend(qs, pg_ref[:, _head(BK_H + h)], pg_ref[:, _head(BV_H + h)],
                    kl_refs[h] if kl_refs else None, vl_refs[h] if vl_refs else None)
        o = _rms(o[:tq] - lam * o[tq:]) * gain_ref[...] * (1.0 - lam_init)
        o_ref[:, _head(h)] = o.astype(o_ref.dtype)


def _diff_lat_kernel(lam_init, q_ref, lam_ref, gain_ref, pg_ref, *rest):
    kl_refs = rest[:B_HEADS]
    vl_refs = rest[B_HEADS:2 * B_HEADS]
    o_ref = rest[2 * B_HEADS]
    _diff_heads(q_ref[...], lam_init, lam_ref, gain_ref, pg_ref, kl_refs, vl_refs, o_ref)


def _diff_ctx_kernel(lam_init, lam_ref, gain_ref, pg_ref, o_ref):
    q = pg_ref[:, BQ_H * HEAD_DIM:BK_H * HEAD_DIM]
    _diff_heads(q, lam_init, lam_ref, gain_ref, pg_ref, None, None, o_ref)


def _diff_lat(px, pg, lam_vecs, gain, lam_init):
    b = px.shape[0]
    head_spec = lambda h: pl.BlockSpec((None, SEQ, HEAD_DIM), lambda bi, qi: (bi, 0, h))
    kv_heads = [BK_H + h for h in range(B_HEADS)] + [BV_H + h for h in range(B_HEADS)]
    return pl.pallas_call(
        functools.partial(_diff_lat_kernel, lam_init),
        grid=(b, SEQ // TQ_B),
        in_specs=[
            pl.BlockSpec((None, TQ_B, B_WIDTH), lambda bi, qi: (bi, qi, BQ_H * HEAD_DIM // B_WIDTH)),
            pl.BlockSpec((4, B_SUB_DIM), lambda bi, qi: (0, 0)),
            pl.BlockSpec((1, HEAD_DIM), lambda bi, qi: (0, 0)),
            pl.BlockSpec((None, CTX_LEN, IN_WIDTH), lambda bi, qi: (bi, 0, 0)),
            *[head_spec(h) for h in kv_heads],
        ],
        out_specs=pl.BlockSpec((None, TQ_B, B_WIDTH), lambda bi, qi: (bi, qi, 0)),
        out_shape=jax.ShapeDtypeStruct((b, SEQ, B_WIDTH), BF16),
        compiler_params=_params("parallel", "arbitrary"),
        name="diff_attention",
    )(px, lam_vecs, gain, pg, *([px] * len(kv_heads)))


def _diff_ctx(pg, lam_vecs, gain, lam_init):
    b = pg.shape[0]
    return pl.pallas_call(
        functools.partial(_diff_ctx_kernel, lam_init),
        grid=(b,),
        in_specs=[
            pl.BlockSpec((4, B_SUB_DIM), lambda bi: (0, 0)),
            pl.BlockSpec((1, HEAD_DIM), lambda bi: (0, 0)),
            pl.BlockSpec((None, CTX_LEN, IN_WIDTH), lambda bi: (bi, 0, 0)),
        ],
        out_specs=pl.BlockSpec((None, CTX_LEN, B_WIDTH), lambda bi: (bi, 0, 0)),
        out_shape=jax.ShapeDtypeStruct((b, CTX_LEN, B_WIDTH), BF16),
        compiler_params=_params("parallel"),
        name="diff_ctx_attention",
    )(lam_vecs, gain, pg)


def _nbr_kernel(q_ref, k_ref, v_ref, pg_ref, bias_ref, gain_ref, o_ref):
    j = pl.program_id(1)
    key_row0 = jnp.clip(NB_ROWS * j - NA_ROWS // 2, 0, GRID_H - NB_KROWS)
    start = pl.multiple_of(key_row0 * GRID_W, NB_ROWS * GRID_W)
    heads = []
    for h in range(C_HEADS):
        q = q_ref[:, _head(h)]
        kw = k_ref[pl.ds(start, NB_K), _head(h)]
        vw = v_ref[pl.ds(start, NB_K), _head(h)]
        kc = pg_ref[:, _head(CK_H + h)]
        vc = pg_ref[:, _head(CV_H + h)]
        s_w = lax.dot_general(q, kw, _NT, preferred_element_type=F32) + bias_ref[h]
        s_c = lax.dot_general(q, kc, _NT, preferred_element_type=F32)
        m = jnp.maximum(jnp.max(s_w, axis=-1, keepdims=True), jnp.max(s_c, axis=-1, keepdims=True))
        p_w = jnp.exp2(s_w - m)
        p_c = jnp.exp2(s_c - m)
        l = jnp.sum(p_w, axis=-1, keepdims=True) + jnp.sum(p_c, axis=-1, keepdims=True)
        o = (jnp.dot(p_w.astype(BF16), vw, preferred_element_type=F32)
             + jnp.dot(p_c.astype(BF16), vc, preferred_element_type=F32))
        heads.append(o / l)
    _store_normed(heads, gain_ref, o_ref)


def _nbr_bias(rel_bias):
    n_blocks = GRID_H // NB_ROWS
    col = np.arange(GRID_W)
    cs = np.clip(col - NA_COLS // 2, 0, GRID_W - NA_COLS)
    col_ok = (col[None, :] >= cs[:, None]) & (col[None, :] < cs[:, None] + NA_COLS)
    dc = col[None, :] - col[:, None] + NA_COLS - 1
    onehot = (dc[None] == np.arange(2 * NA_COLS - 1)[:, None, None]) & col_ok[None]
    per_row = jnp.einsum('hrt,tck->hrck', rel_bias.astype(F32) * LOG2E, jnp.asarray(onehot, F32),
                         precision=lax.Precision.HIGHEST)
    per_row = jnp.where(col_ok, per_row, MASK_VALUE)
    masked = jnp.full((C_HEADS, GRID_W, GRID_W), MASK_VALUE, F32)
    classes = []
    for j in (0, 1, n_blocks - 1):
        key_row0 = int(np.clip(NB_ROWS * j - NA_ROWS // 2, 0, GRID_H - NB_KROWS))
        strips = []
        for i in range(NB_ROWS):
            r = NB_ROWS * j + i
            rs = int(np.clip(r - NA_ROWS // 2, 0, GRID_H - NA_ROWS))
            blocks = []
            for a in range(NB_KROWS):
                kr = key_row0 + a
                in_window = rs <= kr < rs + NA_ROWS
                blocks.append(per_row[:, kr - r + NA_ROWS - 1] if in_window else masked)
            strips.append(jnp.concatenate(blocks, axis=-1))
        classes.append(jnp.concatenate(strips, axis=-2))
    return jnp.stack(classes)


def _nbr(px, pg, bias_cls, gain):
    b = px.shape[0]
    n_blocks = GRID_H // NB_ROWS
    blk = lambda h: h * HEAD_DIM // C_WIDTH

    def cls(j):
        return jnp.where(j == 0, 0, jnp.where(j == n_blocks - 1, 2, 1))

    return pl.pallas_call(
        _nbr_kernel,
        grid=(b, n_blocks),
        in_specs=[
            pl.BlockSpec((None, NB_Q, C_WIDTH), lambda bi, j: (bi, j, blk(CQ_H))),
            pl.BlockSpec((None, SEQ, C_WIDTH), lambda bi, j: (bi, 0, blk(CK_H)),
                         pipeline_mode=pl.Buffered(1)),
            pl.BlockSpec((None, SEQ, C_WIDTH), lambda bi, j: (bi, 0, blk(CV_H)),
                         pipeline_mode=pl.Buffered(1)),
            pl.BlockSpec((None, CTX_LEN, IN_WIDTH), lambda bi, j: (bi, 0, 0)),
            pl.BlockSpec((None, C_HEADS, NB_Q, NB_K), lambda bi, j: (cls(j), 0, 0, 0)),
            pl.BlockSpec((1, C_WIDTH), lambda bi, j: (0, 0)),
        ],
        out_specs=pl.BlockSpec((None, NB_Q, C_WIDTH), lambda bi, j: (bi, j, 0)),
        out_shape=jax.ShapeDtypeStruct((b, SEQ, C_WIDTH), BF16),
        compiler_params=_params("parallel", "arbitrary"),
        name="nbr_attention",
    )(px, px, px, pg, bias_cls, gain)


def _outproj_kernel(x_ref, mod_ref, a_ref, b_ref, c_ref, w_ref, o_ref):
    y = (jnp.dot(a_ref[...], w_ref[:A_WIDTH, :].astype(BF16), preferred_element_type=F32)
         + jnp.dot(b_ref[...], w_ref[A_WIDTH:A_WIDTH + B_WIDTH, :].astype(BF16),
                   preferred_element_type=F32)
         + jnp.dot(c_ref[...], w_ref[A_WIDTH + B_WIDTH:, :].astype(BF16),
                   preferred_element_type=F32))
    o_ref[...] = x_ref[...] + mod_ref[5:6, :] * y


def _outproj(x2, mods, mod_at, layer, a2, b2, c2, w_out, tm):
    m, d = x2.shape
    return pl.pallas_call(
        _outproj_kernel,
        grid=(m // tm, d // TN),
        in_specs=[
            pl.BlockSpec((tm, TN), lambda i, j: (i, j)),
            pl.BlockSpec((None, None, N_MOD, TN), lambda i, j: (layer, mod_at(i), 0, j)),
            pl.BlockSpec((tm, A_WIDTH), lambda i, j: (i, 0)),
            pl.BlockSpec((tm, B_WIDTH), lambda i, j: (i, 0)),
            pl.BlockSpec((tm, C_WIDTH), lambda i, j: (i, 0)),
            pl.BlockSpec((None, MIX_WIDTH, TN), lambda i, j: (layer, 0, j)),
        ],
        out_specs=pl.BlockSpec((tm, TN), lambda i, j: (i, j)),
        out_shape=jax.ShapeDtypeStruct((m, d), F32),
        compiler_params=_params("parallel", "arbitrary"),
        name="outproj",
    )(x2, mods, a2, b2, c2, w_out)


def _column_gains(a_q, a_k, b_q, b_k, c_q, c_k):
    a_scale = LOG2E * HEAD_DIM ** -0.5
    b_scale = LOG2E * B_SUB_DIM ** -0.5
    parts = (
        jnp.tile(c_q * a_scale, C_HEADS), jnp.tile(c_k, C_HEADS), jnp.ones((C_WIDTH,), F32),
        jnp.tile(a_q * a_scale, A_HEADS), jnp.tile(a_k, A_KV_HEADS), jnp.ones((A_KV_WIDTH,), F32),
        jnp.tile(b_q * b_scale, 2 * B_HEADS), jnp.tile(b_k, 2 * B_HEADS), jnp.ones((B_WIDTH,), F32),
    )
    return jnp.concatenate(parts).reshape(1, IN_WIDTH)


def kernel(x, c, ctx, c_ctx, w_mod, b_mod, ffn1_w_gu, ffn1_w_down, ffn2_w_gu, ffn2_w_down,
           w_in, w_out, a_q_gain, a_k_gain, a_out_gain, b_q_gain, b_k_gain, b_lambda, b_out_gain,
           c_q_gain, c_k_gain, c_rel_bias, c_out_gain):
    b, s, d = x.shape
    n_ctx = ctx.shape[1]
    assert (b, s, d, n_ctx) == (2, SEQ, D_MODEL, CTX_LEN)

    cvec = jnp.zeros((8, d), F32).at[:b].set(c).at[b].set(c_ctx)
    mods = _modulation(cvec, w_mod, b_mod).reshape(DEPTH, 8, N_MOD, d)
    tables = _rope_tables(HEAD_DIM // 4) + _rope_tables(B_SUB_DIM // 4)

    x2 = x.reshape(b * s, d)
    g2 = ctx.reshape(b * n_ctx, d)
    tm_g = b * n_ctx

    mod_x = lambda i: i // (s // TM)
    mod_x_out = lambda i: i // (s // TM_OUT)
    mod_g = lambda i: b

    for l in range(DEPTH):
        with_ctx = l < DEPTH - 1
        lam_init = 0.8 - 0.6 * math.exp(-0.3 * l)
        gains = _column_gains(a_q_gain[l], a_k_gain[l], b_q_gain[l], b_k_gain[l],
                              c_q_gain[l], c_k_gain[l])
        a_gain = a_out_gain[l].reshape(1, A_WIDTH)
        b_gain = b_out_gain[l].reshape(1, HEAD_DIM)
        c_gain = c_out_gain[l].reshape(1, C_WIDTH)

        x2 = _ffn(x2, mods, mod_x, l, ffn1_w_gu, ffn1_w_down, 0, TM)
        g2 = _ffn(g2, mods, mod_g, l, ffn1_w_gu, ffn1_w_down, 0, tm_g)

        px = _inproj(x2, mods, mod_x, l, w_in, gains, tables, TM).reshape(b, s, IN_WIDTH)
        pg = _inproj(g2, mods, mod_g, l, w_in, gains, None, tm_g).reshape(b, n_ctx, IN_WIDTH)

        a_lat = _gqa_lat(px, pg, a_gain)
        b_lat = _diff_lat(px, pg, b_lambda[l], b_gain, lam_init)
        c_lat = _nbr(px, pg, _nbr_bias(c_rel_bias[l]), c_gain)

        x2 = _outproj(x2, mods, mod_x_out, l, a_lat.reshape(b * s, A_WIDTH),
                      b_lat.reshape(b * s, B_WIDTH), c_lat.reshape(b * s, C_WIDTH), w_out, TM_OUT)
        x2 = _ffn(x2, mods, mod_x, l, ffn2_w_gu, ffn2_w_down, 6, TM)

        if with_ctx:
            a_ctx = _gqa_ctx(pg, a_gain, A_KV_HEADS, A_GROUP, AQ_H, AK_H, AV_H)
            b_ctx = _diff_ctx(pg, b_lambda[l], b_gain, lam_init)
            c_ctx = _gqa_ctx(pg, c_gain, C_HEADS, 1, CQ_H, CK_H, CV_H)
            g2 = _outproj(g2, mods, mod_g, l, a_ctx.reshape(tm_g, A_WIDTH),
                          b_ctx.reshape(tm_g, B_WIDTH), c_ctx.reshape(tm_g, C_WIDTH), w_out, tm_g)
            g2 = _ffn(g2, mods, mod_g, l, ffn2_w_gu, ffn2_w_down, 6, tm_g)

    return x2.reshape(b, s, d)
```

```python
import functools
import math

import numpy as np
import jax
import jax.numpy as jnp
from jax import lax
from jax.experimental import pallas as pl
from jax.experimental.pallas import tpu as pltpu

F32 = jnp.float32
BF16 = jnp.bfloat16

D_MODEL = 2048
SEQ = 4096
DEPTH = 2
GRID_W = 64
GRID_H = SEQ // GRID_W
CTX_LEN = 256
HEAD_DIM = 128
N_MOD = 9
FFN_HIDDEN = 5632
RMS_EPS = 1e-6
ROPE_THETA = 10000.0

A_HEADS = 6
A_KV_HEADS = 2
A_GROUP = A_HEADS // A_KV_HEADS
B_HEADS = 4
C_HEADS = 6
B_SUB_DIM = HEAD_DIM // 2
A_WIDTH = A_HEADS * HEAD_DIM
A_KV_WIDTH = A_KV_HEADS * HEAD_DIM
B_WIDTH = B_HEADS * HEAD_DIM
C_WIDTH = C_HEADS * HEAD_DIM
MIX_WIDTH = A_WIDTH + B_WIDTH + C_WIDTH
IN_WIDTH = A_WIDTH + 2 * A_KV_WIDTH + 3 * B_WIDTH + 3 * C_WIDTH
NA_ROWS = 8
NA_COLS = 16

CQ_H = 0
CK_H = CQ_H + C_HEADS
CV_H = CK_H + C_HEADS
AQ_H = CV_H + C_HEADS
AK_H = AQ_H + A_HEADS
AV_H = AK_H + A_KV_HEADS
BQ_H = AV_H + A_KV_HEADS
BK_H = BQ_H + B_HEADS
BV_H = BK_H + B_HEADS

VMEM_LIMIT = 56 * 1024 * 1024
FFN_VMEM_LIMIT = 61 * 1024 * 1024
MASK_VALUE = -1e30
LOG2E = math.log2(math.e)

TM = 1024
TM_OUT = 2048
TF = 512
TN = 256
IN_TILES_PER_STEP = 2
ROW_GROUPS = 4
EDGE_GROUPS = 4
TQ_A = 256
TQ_B = 256
KV_CHUNK = 2048
NB_ROWS = 4
NB_Q = NB_ROWS * GRID_W
NB_KROWS = 12
NB_K = NB_KROWS * GRID_W

N_IN_TILES = IN_WIDTH // TN
SRC_SHIFT = (IN_WIDTH - 3 * C_WIDTH) // TN


def _silu(v):
    return v / (1.0 + jnp.exp(-v))


def _rms(v):
    return v * lax.rsqrt(jnp.mean(v * v, axis=-1, keepdims=True) + RMS_EPS)


def _head(h):
    return slice(h * HEAD_DIM, (h + 1) * HEAD_DIM)


def _params(*sem, vmem_limit=VMEM_LIMIT):
    return pltpu.CompilerParams(dimension_semantics=sem, vmem_limit_bytes=vmem_limit)


def _mod_kernel(c_ref, w_ref, b_ref, o_ref):
    s = _silu(c_ref[...]).astype(BF16)
    o_ref[...] = jnp.dot(s, w_ref[...].astype(BF16), preferred_element_type=F32) + b_ref[...]


def _modulation(cvec, w_mod, b_mod):
    depth, d, n = w_mod.shape
    tn = 1024
    return pl.pallas_call(
        _mod_kernel,
        grid=(depth, n // tn),
        in_specs=[
            pl.BlockSpec((8, d), lambda l, j: (0, 0)),
            pl.BlockSpec((None, d, tn), lambda l, j: (l, 0, j)),
            pl.BlockSpec((None, 1, tn), lambda l, j: (l, 0, j)),
        ],
        out_specs=pl.BlockSpec((None, 8, tn), lambda l, j: (l, 0, j)),
        out_shape=jax.ShapeDtypeStruct((depth, 8, n), F32),
        compiler_params=_params("parallel", "parallel"),
        name="modulation",
    )(cvec, w_mod, b_mod.reshape(depth, 1, n))


def _ffn_kernel(mod_row, nf, x_ref, mod_ref, wg_ref, wu_ref, wd_ref, o_ref, xn_ref):
    f = pl.program_id(1)
    tm = x_ref.shape[0]

    def step(first, last, groups):
        wg = wg_ref[...].astype(BF16)
        wu = wu_ref[...].astype(BF16)
        wd = wd_ref[...].astype(BF16)
        for r in range(groups):
            rows = slice(r * (tm // groups), (r + 1) * (tm // groups))
            if first:
                shift = mod_ref[mod_row:mod_row + 1, :]
                scale = mod_ref[mod_row + 1:mod_row + 2, :]
                xn = (_rms(x_ref[rows, :]) * (1.0 + scale) + shift).astype(BF16)
                xn_ref[rows, :] = xn
            else:
                xn = xn_ref[rows, :]
            g = jnp.dot(xn, wg, preferred_element_type=F32)
            u = jnp.dot(xn, wu, preferred_element_type=F32)
            a = (_silu(g) * u).astype(BF16)
            y = jnp.dot(a, wd, preferred_element_type=F32)
            if not first:
                y = o_ref[rows, :] + y
            if last:
                gate = mod_ref[mod_row + 2:mod_row + 3, :]
                y = x_ref[rows, :] + (0.5 * gate) * y
            o_ref[rows, :] = y

    pl.when(f == 0)(lambda: step(True, False, EDGE_GROUPS))
    pl.when((f > 0) & (f < nf - 1))(lambda: step(False, False, 1))
    pl.when(f == nf - 1)(lambda: step(False, True, EDGE_GROUPS))


def _ffn(x2, mods, mod_at, layer, w_gu, w_down, mod_row, tm):
    m, d = x2.shape
    hidden = w_down.shape[1]
    nf = hidden // TF
    return pl.pallas_call(
        functools.partial(_ffn_kernel, mod_row, nf),
        grid=(m // tm, nf),
        in_specs=[
            pl.BlockSpec((tm, d), lambda i, f: (i, 0), pipeline_mode=pl.Buffered(1)),
            pl.BlockSpec((None, None, N_MOD, d), lambda i, f: (layer, mod_at(i), 0, 0)),
            pl.BlockSpec((None, d, TF), lambda i, f: (layer, 0, f)),
            pl.BlockSpec((None, d, TF), lambda i, f: (layer, 0, nf + f)),
            pl.BlockSpec((None, TF, d), lambda i, f: (layer, f, 0)),
        ],
        out_specs=pl.BlockSpec((tm, d), lambda i, f: (i, 0)),
        out_shape=jax.ShapeDtypeStruct((m, d), F32),
        scratch_shapes=[pltpu.VMEM((tm, d), BF16)],
        compiler_params=_params("parallel", "arbitrary", vmem_limit=FFN_VMEM_LIMIT),
        name="ffn",
    )(x2, mods, w_gu, w_gu, w_down)


def _norm_lanes(y, width):
    y2 = y * y
    if width == HEAD_DIM:
        ms = jnp.mean(y2, axis=-1, keepdims=True)
    else:
        lane = lax.broadcasted_iota(jnp.int32, y.shape, 1)
        low = lane < width
        s_lo = jnp.sum(jnp.where(low, y2, 0.0), axis=-1, keepdims=True)
        s_hi = jnp.sum(jnp.where(low, 0.0, y2), axis=-1, keepdims=True)
        ms = jnp.where(low, s_lo, s_hi) * (1.0 / width)
    return y * lax.rsqrt(ms + RMS_EPS)


def _tile_modes():
    per_head = (['c'] * (2 * C_HEADS) + ['p'] * C_HEADS + ['a'] * (A_HEADS + A_KV_HEADS)
                + ['p'] * A_KV_HEADS + ['b'] * (2 * B_HEADS) + ['p'] * B_HEADS)
    hpt = TN // HEAD_DIM
    tiles = [set(per_head[t * hpt:(t + 1) * hpt]) for t in range(N_IN_TILES)]
    assert all(len(t) == 1 for t in tiles)
    return [t.pop() for t in tiles]


def _inproj_kernel(rope, x_ref, mod_ref, *rest):
    w_refs = rest[:IN_TILES_PER_STEP]
    gain_ref = rest[IN_TILES_PER_STEP]
    rest = rest[IN_TILES_PER_STEP + 1:]
    if rope:
        ca_ref, sa_ref, cb_ref, sb_ref, o_ref, xn_ref = rest
    else:
        o_ref, xn_ref = rest
    j = pl.program_id(1)
    heads_per_tile = TN // HEAD_DIM
    group_rows = x_ref.shape[0] // ROW_GROUPS
    rope_tables = {'a': (ca_ref, sa_ref, HEAD_DIM // 4),
                   'b': (cb_ref, sb_ref, B_SUB_DIM // 4)} if rope else {}
    lane = lax.broadcasted_iota(jnp.int32, (group_rows, HEAD_DIM), 1)
    norm_width = {'a': HEAD_DIM, 'b': B_SUB_DIM, 'c': HEAD_DIM, 'p': 0}

    def emit(modes, first=False):
        ws = [w_ref[...].astype(BF16) for w_ref in w_refs]
        for r in range(ROW_GROUPS):
            rows = slice(r * group_rows, (r + 1) * group_rows)
            if first:
                shift = mod_ref[3:4, :]
                scale = mod_ref[4:5, :]
                xn = (_rms(x_ref[rows, :]) * (1.0 + scale) + shift).astype(BF16)
                xn_ref[rows, :] = xn
            else:
                xn = xn_ref[rows, :]
            for t, mode in enumerate(modes):
                y = jnp.dot(xn, ws[t], preferred_element_type=F32)
                for hh in range(heads_per_tile):
                    cols = _head(t * heads_per_tile + hh)
                    v = y[:, _head(hh)]
                    if norm_width[mode]:
                        v = _norm_lanes(v, norm_width[mode]) * gain_ref[:, cols]
                    if mode in rope_tables:
                        c_ref, s_ref, half = rope_tables[mode]
                        partner = jnp.where((lane & (2 * half - 1)) < half,
                                            pltpu.roll(v, HEAD_DIM - half, 1), pltpu.roll(v, half, 1))
                        v = v * c_ref[rows, :] + partner * s_ref[rows, :]
                    o_ref[rows, cols] = v.astype(o_ref.dtype)

    tile_modes = _tile_modes()
    steps = {}
    for step in range(N_IN_TILES // IN_TILES_PER_STEP):
        modes = tuple(tile_modes[step * IN_TILES_PER_STEP:(step + 1) * IN_TILES_PER_STEP])
        steps.setdefault(modes, []).append(step)

    pl.when(j == 0)(functools.partial(emit, tuple(tile_modes[:IN_TILES_PER_STEP]), first=True))
    for modes, where in steps.items():
        later = [t for t in where if t > 0]
        if later:
            cond = functools.reduce(lambda a, b: a | b, [j == t for t in later])
            pl.when(cond)(functools.partial(emit, modes))


def _inproj(x2, mods, mod_at, layer, w_in, gains, tables, tm):
    m, d = x2.shape
    rope = tables is not None
    tn = TN * IN_TILES_PER_STEP
    src = lambda t: lax.rem(t + SRC_SHIFT, N_IN_TILES)
    w_spec = lambda k: pl.BlockSpec((None, d, TN),
                                    lambda i, j: (layer, 0, src(j * IN_TILES_PER_STEP + k)))
    in_specs = [
        pl.BlockSpec((tm, d), lambda i, j: (i, 0), pipeline_mode=pl.Buffered(1)),
        pl.BlockSpec((None, None, N_MOD, d), lambda i, j: (layer, mod_at(i), 0, 0)),
        *[w_spec(k) for k in range(IN_TILES_PER_STEP)],
        pl.BlockSpec((1, tn), lambda i, j: (0, j)),
    ]
    args = [x2, mods] + [w_in] * IN_TILES_PER_STEP + [gains]
    if rope:
        pos_tiles = SEQ // tm
        for t in tables:
            in_specs.append(pl.BlockSpec((tm, HEAD_DIM), lambda i, j: (i % pos_tiles, 0)))
            args.append(t)
    return pl.pallas_call(
        functools.partial(_inproj_kernel, rope),
        grid=(m // tm, N_IN_TILES // IN_TILES_PER_STEP),
        in_specs=in_specs,
        out_specs=pl.BlockSpec((tm, tn), lambda i, j: (i, j)),
        out_shape=jax.ShapeDtypeStruct((m, IN_WIDTH), BF16),
        scratch_shapes=[pltpu.VMEM((tm, d), BF16)],
        compiler_params=_params("parallel", "arbitrary"),
        name="inproj",
    )(*args)


def _rope_tables(half):
    t = jnp.arange(SEQ, dtype=jnp.int32)
    row = (t // GRID_W).astype(F32)[:, None]
    col = (t % GRID_W).astype(F32)[:, None]
    lane = np.arange(HEAD_DIM)
    freqs = ROPE_THETA ** (-jnp.asarray(lane % half, dtype=F32) / half)
    is_col = jnp.asarray((lane % (4 * half)) >= 2 * half)[None, :]
    ang = jnp.where(is_col, col, row) * freqs[None, :]
    first = jnp.asarray((lane % (2 * half)) < half)[None, :]
    cos = jnp.cos(ang)
    sin = jnp.sin(ang)
    return cos, jnp.where(first, -sin, sin)


_NT = (((1,), (1,)), ((), ()))


def _softmax_step(s, v, state):
    m_cur = jnp.max(s, axis=-1, keepdims=True)
    if state is None:
        m_new = m_cur
        p = jnp.exp2(s - m_new)
        l_new = jnp.sum(p, axis=-1, keepdims=True)
        acc = jnp.dot(p.astype(BF16), v, preferred_element_type=F32)
    else:
        m_old, l_old, acc_old = state
        m_new = jnp.maximum(m_old, m_cur)
        alpha = jnp.exp2(m_old - m_new)
        p = jnp.exp2(s - m_new)
        l_new = alpha * l_old + jnp.sum(p, axis=-1, keepdims=True)
        acc = alpha * acc_old + jnp.dot(p.astype(BF16), v, preferred_element_type=F32)
    return m_new, l_new, acc


def _attend(qs, kc, vc, kl_ref, vl_ref):
    def scores(k):
        return lax.dot_general(qs, k, _NT, preferred_element_type=F32)

    state = _softmax_step(scores(kc), vc, None)
    if kl_ref is not None:
        for start in range(0, kl_ref.shape[0], KV_CHUNK):
            state = _softmax_step(scores(kl_ref[start:start + KV_CHUNK, :]),
                                  vl_ref[start:start + KV_CHUNK, :], state)
    _, l, acc = state
    return acc / l


def _store_normed(heads, gain_ref, o_ref):
    ms = sum(jnp.sum(o * o, axis=-1, keepdims=True) for o in heads) * (1.0 / (len(heads) * HEAD_DIM))
    r = lax.rsqrt(ms + RMS_EPS)
    for h, o in enumerate(heads):
        o_ref[:, _head(h)] = (o * r * gain_ref[:, _head(h)]).astype(o_ref.dtype)


def _gqa_heads(q, n_kv, group, pg_ref, k_h, v_h, kl_refs, vl_refs):
    tq = q.shape[0]
    heads = []
    for g in range(n_kv):
        qs = jnp.concatenate([q[:, _head(g * group + i)] for i in range(group)], axis=0)
        o = _attend(qs, pg_ref[:, _head(k_h + g)], pg_ref[:, _head(v_h + g)],
                    kl_refs[g] if kl_refs else None, vl_refs[g] if vl_refs else None)
        heads += [o[i * tq:(i + 1) * tq] for i in range(group)]
    return heads


def _gqa_lat_kernel(q_ref, pg_ref, *rest):
    kl_refs = rest[:A_KV_HEADS]
    vl_refs = rest[A_KV_HEADS:2 * A_KV_HEADS]
    gain_ref, o_ref = rest[2 * A_KV_HEADS:]
    heads = _gqa_heads(q_ref[...], A_KV_HEADS, A_GROUP, pg_ref, AK_H, AV_H, kl_refs, vl_refs)
    _store_normed(heads, gain_ref, o_ref)


def _gqa_ctx_kernel(n_kv, group, q_h, k_h, v_h, pg_ref, gain_ref, o_ref):
    q = pg_ref[:, q_h * HEAD_DIM:(q_h + n_kv * group) * HEAD_DIM]
    _store_normed(_gqa_heads(q, n_kv, group, pg_ref, k_h, v_h, None, None), gain_ref, o_ref)


def _gqa_lat(px, pg, gain):
    b = px.shape[0]
    head_spec = lambda h: pl.BlockSpec((None, SEQ, HEAD_DIM), lambda bi, qi: (bi, 0, h))
    kv_heads = [AK_H + g for g in range(A_KV_HEADS)] + [AV_H + g for g in range(A_KV_HEADS)]
    return pl.pallas_call(
        _gqa_lat_kernel,
        grid=(b, SEQ // TQ_A),
        in_specs=[
            pl.BlockSpec((None, TQ_A, A_WIDTH), lambda bi, qi: (bi, qi, AQ_H * HEAD_DIM // A_WIDTH)),
            pl.BlockSpec((None, CTX_LEN, IN_WIDTH), lambda bi, qi: (bi, 0, 0)),
            *[head_spec(h) for h in kv_heads],
            pl.BlockSpec((1, A_WIDTH), lambda bi, qi: (0, 0)),
        ],
        out_specs=pl.BlockSpec((None, TQ_A, A_WIDTH), lambda bi, qi: (bi, qi, 0)),
        out_shape=jax.ShapeDtypeStruct((b, SEQ, A_WIDTH), BF16),
        compiler_params=_params("parallel", "arbitrary"),
        name="gqa_attention",
    )(px, pg, *([px] * len(kv_heads)), gain)


def _gqa_ctx(pg, gain, n_kv, group, q_h, k_h, v_h):
    b = pg.shape[0]
    width = n_kv * group * HEAD_DIM
    return pl.pallas_call(
        functools.partial(_gqa_ctx_kernel, n_kv, group, q_h, k_h, v_h),
        grid=(b,),
        in_specs=[
            pl.BlockSpec((None, CTX_LEN, IN_WIDTH), lambda bi: (bi, 0, 0)),
            pl.BlockSpec((1, width), lambda bi: (0, 0)),
        ],
        out_specs=pl.BlockSpec((None, CTX_LEN, width), lambda bi: (bi, 0, 0)),
        out_shape=jax.ShapeDtypeStruct((b, CTX_LEN, width), BF16),
        compiler_params=_params("parallel"),
        name="ctx_attention",
    )(pg, gain)


def _diff_heads(q, lam_init, lam_ref, gain_ref, pg_ref, kl_refs, vl_refs, o_ref):
    tq = q.shape[0]
    lv = lam_ref[...]
    lam = (jnp.exp(jnp.sum(lv[0:1] * lv[1:2], axis=-1, keepdims=True))
           - jnp.exp(jnp.sum(lv[2:3] * lv[3:4], axis=-1, keepdims=True)) + lam_init)
    low = lax.broadcasted_iota(jnp.int32, (tq, HEAD_DIM), 1) < B_SUB_DIM
    zero = jnp.zeros((tq, HEAD_DIM), q.dtype)
    for h in range(B_HEADS):
        qh = q[:, _head(h)]
        qs = jnp.concatenate([jnp.where(low, qh, zero), jnp.where(low, zero, qh)], axis=0)
        o = _attend(qs, pg_ref[:, _head(BK_H + h)], pg_ref[:, _head(BV_H + h)],
                    kl_refs[h] if kl_refs else None, vl_refs[h] if vl_refs else None)
        o = _rms(o[:tq] - lam * o[tq:]) * gain_ref[...] * (1.0 - lam_init)
        o_ref[:, _head(h)] = o.astype(o_ref.dtype)


def _diff_lat_kernel(lam_init, q_ref, lam_ref, gain_ref, pg_ref, *rest):
    kl_refs = rest[:B_HEADS]
    vl_refs = rest[B_HEADS:2 * B_HEADS]
    o_ref = rest[2 * B_HEADS]
    _diff_heads(q_ref[...], lam_init, lam_ref, gain_ref, pg_ref, kl_refs, vl_refs, o_ref)


def _diff_ctx_kernel(lam_init, lam_ref, gain_ref, pg_ref, o_ref):
    q = pg_ref[:, BQ_H * HEAD_DIM:BK_H * HEAD_DIM]
    _diff_heads(q, lam_init, lam_ref, gain_ref, pg_ref, None, None, o_ref)


def _diff_lat(px, pg, lam_vecs, gain, lam_init):
    b = px.shape[0]
    head_spec = lambda h: pl.BlockSpec((None, SEQ, HEAD_DIM), lambda bi, qi: (bi, 0, h))
    kv_heads = [BK_H + h for h in range(B_HEADS)] + [BV_H + h for h in range(B_HEADS)]
    return pl.pallas_call(
        functools.partial(_diff_lat_kernel, lam_init),
        grid=(b, SEQ // TQ_B),
        in_specs=[
            pl.BlockSpec((None, TQ_B, B_WIDTH), lambda bi, qi: (bi, qi, BQ_H * HEAD_DIM // B_WIDTH)),
            pl.BlockSpec((4, B_SUB_DIM), lambda bi, qi: (0, 0)),
            pl.BlockSpec((1, HEAD_DIM), lambda bi, qi: (0, 0)),
            pl.BlockSpec((None, CTX_LEN, IN_WIDTH), lambda bi, qi: (bi, 0, 0)),
            *[head_spec(h) for h in kv_heads],
        ],
        out_specs=pl.BlockSpec((None, TQ_B, B_WIDTH), lambda bi, qi: (bi, qi, 0)),
        out_shape=jax.ShapeDtypeStruct((b, SEQ, B_WIDTH), BF16),
        compiler_params=_params("parallel", "arbitrary"),
        name="diff_attention",
    )(px, lam_vecs, gain, pg, *([px] * len(kv_heads)))


def _diff_ctx(pg, lam_vecs, gain, lam_init):
    b = pg.shape[0]
    return pl.pallas_call(
        functools.partial(_diff_ctx_kernel, lam_init),
        grid=(b,),
        in_specs=[
            pl.BlockSpec((4, B_SUB_DIM), lambda bi: (0, 0)),
            pl.BlockSpec((1, HEAD_DIM), lambda bi: (0, 0)),
            pl.BlockSpec((None, CTX_LEN, IN_WIDTH), lambda bi: (bi, 0, 0)),
        ],
        out_specs=pl.BlockSpec((None, CTX_LEN, B_WIDTH), lambda bi: (bi, 0, 0)),
        out_shape=jax.ShapeDtypeStruct((b, CTX_LEN, B_WIDTH), BF16),
        compiler_params=_params("parallel"),
        name="diff_ctx_attention",
    )(lam_vecs, gain, pg)


def _nbr_kernel(q_ref, k_ref, v_ref, pg_ref, bias_ref, gain_ref, o_ref):
    j = pl.program_id(1)
    key_row0 = jnp.clip(NB_ROWS * j - NA_ROWS // 2, 0, GRID_H - NB_KROWS)
    start = pl.multiple_of(key_row0 * GRID_W, NB_ROWS * GRID_W)
    heads = []
    for h in range(C_HEADS):
        q = q_ref[:, _head(h)]
        kw = k_ref[pl.ds(start, NB_K), _head(h)]
        vw = v_ref[pl.ds(start, NB_K), _head(h)]
        kc = pg_ref[:, _head(CK_H + h)]
        vc = pg_ref[:, _head(CV_H + h)]
        s_w = lax.dot_general(q, kw, _NT, preferred_element_type=F32) + bias_ref[h]
        s_c = lax.dot_general(q, kc, _NT, preferred_element_type=F32)
        m = jnp.maximum(jnp.max(s_w, axis=-1, keepdims=True), jnp.max(s_c, axis=-1, keepdims=True))
        p_w = jnp.exp2(s_w - m)
        p_c = jnp.exp2(s_c - m)
        l = jnp.sum(p_w, axis=-1, keepdims=True) + jnp.sum(p_c, axis=-1, keepdims=True)
        o = (jnp.dot(p_w.astype(BF16), vw, preferred_element_type=F32)
             + jnp.dot(p_c.astype(BF16), vc, preferred_element_type=F32))
        heads.append(o / l)
    _store_normed(heads, gain_ref, o_ref)


def _nbr_bias(rel_bias):
    n_blocks = GRID_H // NB_ROWS
    col = np.arange(GRID_W)
    cs = np.clip(col - NA_COLS // 2, 0, GRID_W - NA_COLS)
    col_ok = (col[None, :] >= cs[:, None]) & (col[None, :] < cs[:, None] + NA_COLS)
    dc = col[None, :] - col[:, None] + NA_COLS - 1
    onehot = (dc[None] == np.arange(2 * NA_COLS - 1)[:, None, None]) & col_ok[None]
    per_row = jnp.einsum('hrt,tck->hrck', rel_bias.astype(F32) * LOG2E, jnp.asarray(onehot, F32),
                         precision=lax.Precision.HIGHEST)
    per_row = jnp.where(col_ok, per_row, MASK_VALUE)
    masked = jnp.full((C_HEADS, GRID_W, GRID_W), MASK_VALUE, F32)
    classes = []
    for j in (0, 1, n_blocks - 1):
        key_row0 = int(np.clip(NB_ROWS * j - NA_ROWS // 2, 0, GRID_H - NB_KROWS))
        strips = []
        for i in range(NB_ROWS):
            r = NB_ROWS * j + i
            rs = int(np.clip(r - NA_ROWS // 2, 0, GRID_H - NA_ROWS))
            blocks = []
            for a in range(NB_KROWS):
                kr = key_row0 + a
                in_window = rs <= kr < rs + NA_ROWS
                blocks.append(per_row[:, kr - r + NA_ROWS - 1] if in_window else masked)
            strips.append(jnp.concatenate(blocks, axis=-1))
        classes.append(jnp.concatenate(strips, axis=-2))
    return jnp.stack(classes)


def _nbr(px, pg, bias_cls, gain):
    b = px.shape[0]
    n_blocks = GRID_H // NB_ROWS
    blk = lambda h: h * HEAD_DIM // C_WIDTH

    def cls(j):
        return jnp.where(j == 0, 0, jnp.where(j == n_blocks - 1, 2, 1))

    return pl.pallas_call(
        _nbr_kernel,
        grid=(b, n_blocks),
        in_specs=[
            pl.BlockSpec((None, NB_Q, C_WIDTH), lambda bi, j: (bi, j, blk(CQ_H))),
            pl.BlockSpec((None, SEQ, C_WIDTH), lambda bi, j: (bi, 0, blk(CK_H)),
                         pipeline_mode=pl.Buffered(1)),
            pl.BlockSpec((None, SEQ, C_WIDTH), lambda bi, j: (bi, 0, blk(CV_H)),
                         pipeline_mode=pl.Buffered(1)),
            pl.BlockSpec((None, CTX_LEN, IN_WIDTH), lambda bi, j: (bi, 0, 0)),
            pl.BlockSpec((None, C_HEADS, NB_Q, NB_K), lambda bi, j: (cls(j), 0, 0, 0)),
            pl.BlockSpec((1, C_WIDTH), lambda bi, j: (0, 0)),
        ],
        out_specs=pl.BlockSpec((None, NB_Q, C_WIDTH), lambda bi, j: (bi, j, 0)),
        out_shape=jax.ShapeDtypeStruct((b, SEQ, C_WIDTH), BF16),
        compiler_params=_params("parallel", "arbitrary"),
        name="nbr_attention",
    )(px, px, px, pg, bias_cls, gain)


def _outproj_kernel(x_ref, mod_ref, a_ref, b_ref, c_ref, w_ref, o_ref):
    y = (jnp.dot(a_ref[...], w_ref[:A_WIDTH, :].astype(BF16), preferred_element_type=F32)
         + jnp.dot(b_ref[...], w_ref[A_WIDTH:A_WIDTH + B_WIDTH, :].astype(BF16),
                   preferred_element_type=F32)
         + jnp.dot(c_ref[...], w_ref[A_WIDTH + B_WIDTH:, :].astype(BF16),
                   preferred_element_type=F32))
    o_ref[...] = x_ref[...] + mod_ref[5:6, :] * y


def _outproj(x2, mods, mod_at, layer, a2, b2, c2, w_out, tm):
    m, d = x2.shape
    return pl.pallas_call(
        _outproj_kernel,
        grid=(m // tm, d // TN),
        in_specs=[
            pl.BlockSpec((tm, TN), lambda i, j: (i, j)),
            pl.BlockSpec((None, None, N_MOD, TN), lambda i, j: (layer, mod_at(i), 0, j)),
            pl.BlockSpec((tm, A_WIDTH), lambda i, j: (i, 0)),
            pl.BlockSpec((tm, B_WIDTH), lambda i, j: (i, 0)),
            pl.BlockSpec((tm, C_WIDTH), lambda i, j: (i, 0)),
            pl.BlockSpec((None, MIX_WIDTH, TN), lambda i, j: (layer, 0, j)),
        ],
        out_specs=pl.BlockSpec((tm, TN), lambda i, j: (i, j)),
        out_shape=jax.ShapeDtypeStruct((m, d), F32),
        compiler_params=_params("parallel", "arbitrary"),
        name="outproj",
    )(x2, mods, a2, b2, c2, w_out)


def _column_gains(a_q, a_k, b_q, b_k, c_q, c_k):
    a_scale = LOG2E * HEAD_DIM ** -0.5
    b_scale = LOG2E * B_SUB_DIM ** -0.5
    parts = (
        jnp.tile(c_q * a_scale, C_HEADS), jnp.tile(c_k, C_HEADS), jnp.ones((C_WIDTH,), F32),
        jnp.tile(a_q * a_scale, A_HEADS), jnp.tile(a_k, A_KV_HEADS), jnp.ones((A_KV_WIDTH,), F32),
        jnp.tile(b_q * b_scale, 2 * B_HEADS), jnp.tile(b_k, 2 * B_HEADS), jnp.ones((B_WIDTH,), F32),
    )
    return jnp.concatenate(parts).reshape(1, IN_WIDTH)


def kernel(x, c, ctx, c_ctx, w_mod, b_mod, ffn1_w_gu, ffn1_w_down, ffn2_w_gu, ffn2_w_down,
           w_in, w_out, a_q_gain, a_k_gain, a_out_gain, b_q_gain, b_k_gain, b_lambda, b_out_gain,
           c_q_gain, c_k_gain, c_rel_bias, c_out_gain):
    b, s, d = x.shape
    n_ctx = ctx.shape[1]
    assert (b, s, d, n_ctx) == (2, SEQ, D_MODEL, CTX_LEN)

    cvec = jnp.zeros((8, d), F32).at[:b].set(c).at[b].set(c_ctx)
    mods = _modulation(cvec, w_mod, b_mod).reshape(DEPTH, 8, N_MOD, d)
    tables = _rope_tables(HEAD_DIM // 4) + _rope_tables(B_SUB_DIM // 4)

    x2 = x.reshape(b * s, d)
    g2 = ctx.reshape(b * n_ctx, d)
    tm_g = b * n_ctx

    mod_x = lambda i: i // (s // TM)
    mod_x_out = lambda i: i // (s // TM_OUT)
    mod_g = lambda i: b

    for l in range(DEPTH):
        with_ctx = l < DEPTH - 1
        lam_init = 0.8 - 0.6 * math.exp(-0.3 * l)
        gains = _column_gains(a_q_gain[l], a_k_gain[l], b_q_gain[l], b_k_gain[l],
                              c_q_gain[l], c_k_gain[l])
        a_gain = a_out_gain[l].reshape(1, A_WIDTH)
        b_gain = b_out_gain[l].reshape(1, HEAD_DIM)
        c_gain = c_out_gain[l].reshape(1, C_WIDTH)

        x2 = _ffn(x2, mods, mod_x, l, ffn1_w_gu, ffn1_w_down, 0, TM)
        g2 = _ffn(g2, mods, mod_g, l, ffn1_w_gu, ffn1_w_down, 0, tm_g)

        px = _inproj(x2, mods, mod_x, l, w_in, gains, tables, TM).reshape(b, s, IN_WIDTH)
        pg = _inproj(g2, mods, mod_g, l, w_in, gains, None, tm_g).reshape(b, n_ctx, IN_WIDTH)

        a_lat = _gqa_lat(px, pg, a_gain)
        b_lat = _diff_lat(px, pg, b_lambda[l], b_gain, lam_init)
        c_lat = _nbr(px, pg, _nbr_bias(c_rel_bias[l]), c_gain)

        x2 = _outproj(x2, mods, mod_x_out, l, a_lat.reshape(b * s, A_WIDTH),
                      b_lat.reshape(b * s, B_WIDTH), c_lat.reshape(b * s, C_WIDTH), w_out, TM_OUT)
        x2 = _ffn(x2, mods, mod_x, l, ffn2_w_gu, ffn2_w_down, 6, TM)

        if with_ctx:
            a_ctx = _gqa_ctx(pg, a_gain, A_KV_HEADS, A_GROUP, AQ_H, AK_H, AV_H)
            b_ctx = _diff_ctx(pg, b_lambda[l], b_gain, lam_init)
            c_ctx = _gqa_ctx(pg, c_gain, C_HEADS, 1, CQ_H, CK_H, CV_H)
            g2 = _outproj(g2, mods, mod_g, l, a_ctx.reshape(tm_g, A_WIDTH),
                          b_ctx.reshape(tm_g, B_WIDTH), c_ctx.reshape(tm_g, C_WIDTH), w_out, tm_g)
            g2 = _ffn(g2, mods, mod_g, l, ffn2_w_gu, ffn2_w_down, 6, tm_g)

    return x2.reshape(b, s, d)
```

```python
import functools
import math

import numpy as np
import jax
import jax.numpy as jnp
from jax import lax
from jax.experimental import pallas as pl
from jax.experimental.pallas import tpu as pltpu

F32 = jnp.float32
BF16 = jnp.bfloat16

D_MODEL = 2048
SEQ = 4096
DEPTH = 2
GRID_W = 64
GRID_H = SEQ // GRID_W
CTX_LEN = 256
HEAD_DIM = 128
N_MOD = 9
FFN_HIDDEN = 5632
RMS_EPS = 1e-6
ROPE_THETA = 10000.0

A_HEADS = 6
A_KV_HEADS = 2
A_GROUP = A_HEADS // A_KV_HEADS
B_HEADS = 4
C_HEADS = 6
B_SUB_DIM = HEAD_DIM // 2
A_WIDTH = A_HEADS * HEAD_DIM
A_KV_WIDTH = A_KV_HEADS * HEAD_DIM
B_WIDTH = B_HEADS * HEAD_DIM
C_WIDTH = C_HEADS * HEAD_DIM
MIX_WIDTH = A_WIDTH + B_WIDTH + C_WIDTH
IN_WIDTH = A_WIDTH + 2 * A_KV_WIDTH + 3 * B_WIDTH + 3 * C_WIDTH
NA_ROWS = 8
NA_COLS = 16

CQ_H = 0
CK_H = CQ_H + C_HEADS
CV_H = CK_H + C_HEADS
AQ_H = CV_H + C_HEADS
AK_H = AQ_H + A_HEADS
AV_H = AK_H + A_KV_HEADS
BQ_H = AV_H + A_KV_HEADS
BK_H = BQ_H + B_HEADS
BV_H = BK_H + B_HEADS

VMEM_LIMIT = 56 * 1024 * 1024
FFN_VMEM_LIMIT = 61 * 1024 * 1024
MASK_VALUE = -1e30
LOG2E = math.log2(math.e)

TM = 1024
TM_OUT = 2048
TM_FFN = 2048
TF = 256
TF_CTX = 512
TN = 256
IN_TILES_PER_STEP = 2
ROW_GROUPS = 4
EDGE_ROWS = 256
TQ_A = 256
TQ_B = 256
KV_CHUNK = 2048
NB_ROWS = 4
NB_Q = NB_ROWS * GRID_W
NB_KROWS = 12
NB_K = NB_KROWS * GRID_W

N_IN_TILES = IN_WIDTH // TN
SRC_SHIFT = (IN_WIDTH - 3 * C_WIDTH) // TN


def _silu(v):
    return v / (1.0 + jnp.exp(-v))


def _rms(v):
    return v * lax.rsqrt(jnp.mean(v * v, axis=-1, keepdims=True) + RMS_EPS)


def _head(h):
    return slice(h * HEAD_DIM, (h + 1) * HEAD_DIM)


def _params(*sem, vmem_limit=VMEM_LIMIT):
    return pltpu.CompilerParams(dimension_semantics=sem, vmem_limit_bytes=vmem_limit)


def _mod_kernel(c_ref, w_ref, b_ref, o_ref):
    s = _silu(c_ref[...]).astype(BF16)
    o_ref[...] = jnp.dot(s, w_ref[...].astype(BF16), preferred_element_type=F32) + b_ref[...]


def _modulation(cvec, w_mod, b_mod):
    depth, d, n = w_mod.shape
    tn = 1024
    return pl.pallas_call(
        _mod_kernel,
        grid=(depth, n // tn),
        in_specs=[
            pl.BlockSpec((8, d), lambda l, j: (0, 0)),
            pl.BlockSpec((None, d, tn), lambda l, j: (l, 0, j)),
            pl.BlockSpec((None, 1, tn), lambda l, j: (l, 0, j)),
        ],
        out_specs=pl.BlockSpec((None, 8, tn), lambda l, j: (l, 0, j)),
        out_shape=jax.ShapeDtypeStruct((depth, 8, n), F32),
        compiler_params=_params("parallel", "parallel"),
        name="modulation",
    )(cvec, w_mod, b_mod.reshape(depth, 1, n))


def _ffn_kernel(mod_row, nf, x_ref, mod_ref, wg_ref, wu_ref, wd_ref, o_ref, xn_ref):
    f = pl.program_id(1)
    tm = x_ref.shape[0]

    def step(first, last, groups):
        wg = wg_ref[...].astype(BF16)
        wu = wu_ref[...].astype(BF16)
        wd = wd_ref[...].astype(BF16)
        for r in range(groups):
            rows = slice(r * (tm // groups), (r + 1) * (tm // groups))
            if first:
                shift = mod_ref[mod_row:mod_row + 1, :]
                scale = mod_ref[mod_row + 1:mod_row + 2, :]
                xn = (_rms(x_ref[rows, :]) * (1.0 + scale) + shift).astype(BF16)
                xn_ref[rows, :] = xn
            else:
                xn = xn_ref[rows, :]
            g = jnp.dot(xn, wg, preferred_element_type=F32)
            u = jnp.dot(xn, wu, preferred_element_type=F32)
            a = (_silu(g) * u).astype(BF16)
            y = jnp.dot(a, wd, preferred_element_type=F32)
            if not first:
                y = o_ref[rows, :] + y
            if last:
                gate = mod_ref[mod_row + 2:mod_row + 3, :]
                y = x_ref[rows, :] + (0.5 * gate) * y
            o_ref[rows, :] = y

    edge_groups = tm // EDGE_ROWS
    pl.when(f == 0)(lambda: step(True, False, edge_groups))
    pl.when((f > 0) & (f < nf - 1))(lambda: step(False, False, 1))
    pl.when(f == nf - 1)(lambda: step(False, True, edge_groups))


def _ffn(x2, mods, mod_at, layer, w_gu, w_down, mod_row, tm, tf):
    m, d = x2.shape
    hidden = w_down.shape[1]
    nf = hidden // tf
    return pl.pallas_call(
        functools.partial(_ffn_kernel, mod_row, nf),
        grid=(m // tm, nf),
        in_specs=[
            pl.BlockSpec((tm, d), lambda i, f: (i, 0), pipeline_mode=pl.Buffered(1)),
            pl.BlockSpec((None, None, N_MOD, d), lambda i, f: (layer, mod_at(i), 0, 0)),
            pl.BlockSpec((None, d, tf), lambda i, f: (layer, 0, f)),
            pl.BlockSpec((None, d, tf), lambda i, f: (layer, 0, nf + f)),
            pl.BlockSpec((None, tf, d), lambda i, f: (layer, f, 0)),
        ],
        out_specs=pl.BlockSpec((tm, d), lambda i, f: (i, 0), pipeline_mode=pl.Buffered(1)),
        out_shape=jax.ShapeDtypeStruct((m, d), F32),
        scratch_shapes=[pltpu.VMEM((tm, d), BF16)],
        compiler_params=_params("parallel", "arbitrary", vmem_limit=FFN_VMEM_LIMIT),
        name="ffn",
    )(x2, mods, w_gu, w_gu, w_down)


def _norm_lanes(y, width):
    y2 = y * y
    if width == HEAD_DIM:
        ms = jnp.mean(y2, axis=-1, keepdims=True)
    else:
        lane = lax.broadcasted_iota(jnp.int32, y.shape, 1)
        low = lane < width
        s_lo = jnp.sum(jnp.where(low, y2, 0.0), axis=-1, keepdims=True)
        s_hi = jnp.sum(jnp.where(low, 0.0, y2), axis=-1, keepdims=True)
        ms = jnp.where(low, s_lo, s_hi) * (1.0 / width)
    return y * lax.rsqrt(ms + RMS_EPS)


def _tile_modes():
    per_head = (['c'] * (2 * C_HEADS) + ['p'] * C_HEADS + ['a'] * (A_HEADS + A_KV_HEADS)
                + ['p'] * A_KV_HEADS + ['b'] * (2 * B_HEADS) + ['p'] * B_HEADS)
    hpt = TN // HEAD_DIM
    tiles = [set(per_head[t * hpt:(t + 1) * hpt]) for t in range(N_IN_TILES)]
    assert all(len(t) == 1 for t in tiles)
    return [t.pop() for t in tiles]


def _inproj_kernel(rope, x_ref, mod_ref, *rest):
    w_refs = rest[:IN_TILES_PER_STEP]
    gain_ref = rest[IN_TILES_PER_STEP]
    rest = rest[IN_TILES_PER_STEP + 1:]
    if rope:
        ca_ref, sa_ref, cb_ref, sb_ref, o_ref, xn_ref = rest
    else:
        o_ref, xn_ref = rest
    j = pl.program_id(1)
    heads_per_tile = TN // HEAD_DIM
    group_rows = x_ref.shape[0] // ROW_GROUPS
    rope_tables = {'a': (ca_ref, sa_ref, HEAD_DIM // 4),
                   'b': (cb_ref, sb_ref, B_SUB_DIM // 4)} if rope else {}
    lane = lax.broadcasted_iota(jnp.int32, (group_rows, HEAD_DIM), 1)
    norm_width = {'a': HEAD_DIM, 'b': B_SUB_DIM, 'c': HEAD_DIM, 'p': 0}

    def emit(modes, first=False):
        ws = [w_ref[...].astype(BF16) for w_ref in w_refs]
        for r in range(ROW_GROUPS):
            rows = slice(r * group_rows, (r + 1) * group_rows)
            if first:
                shift = mod_ref[3:4, :]
                scale = mod_ref[4:5, :]
                xn = (_rms(x_ref[rows, :]) * (1.0 + scale) + shift).astype(BF16)
                xn_ref[rows, :] = xn
            else:
                xn = xn_ref[rows, :]
            for t, mode in enumerate(modes):
                y = jnp.dot(xn, ws[t], preferred_element_type=F32)
                for hh in range(heads_per_tile):
                    cols = _head(t * heads_per_tile + hh)
                    v = y[:, _head(hh)]
                    if norm_width[mode]:
                        v = _norm_lanes(v, norm_width[mode]) * gain_ref[:, cols]
                    if mode in rope_tables:
                        c_ref, s_ref, half = rope_tables[mode]
                        partner = jnp.where((lane & (2 * half - 1)) < half,
                                            pltpu.roll(v, HEAD_DIM - half, 1), pltpu.roll(v, half, 1))
                        v = v * c_ref[rows, :] + partner * s_ref[rows, :]
                    o_ref[rows, cols] = v.astype(o_ref.dtype)

    tile_modes = _tile_modes()
    steps = {}
    for step in range(N_IN_TILES // IN_TILES_PER_STEP):
        modes = tuple(tile_modes[step * IN_TILES_PER_STEP:(step + 1) * IN_TILES_PER_STEP])
        steps.setdefault(modes, []).append(step)

    pl.when(j == 0)(functools.partial(emit, tuple(tile_modes[:IN_TILES_PER_STEP]), first=True))
    for modes, where in steps.items():
        later = [t for t in where if t > 0]
        if later:
            cond = functools.reduce(lambda a, b: a | b, [j == t for t in later])
            pl.when(cond)(functools.partial(emit, modes))


def _inproj(x2, mods, mod_at, layer, w_in, gains, tables, tm):
    m, d = x2.shape
    rope = tables is not None
    tn = TN * IN_TILES_PER_STEP
    src = lambda t: lax.rem(t + SRC_SHIFT, N_IN_TILES)
    w_spec = lambda k: pl.BlockSpec((None, d, TN),
                                    lambda i, j: (layer, 0, src(j * IN_TILES_PER_STEP + k)))
    in_specs = [
        pl.BlockSpec((tm, d), lambda i, j: (i, 0), pipeline_mode=pl.Buffered(1)),
        pl.BlockSpec((None, None, N_MOD, d), lambda i, j: (layer, mod_at(i), 0, 0)),
        *[w_spec(k) for k in range(IN_TILES_PER_STEP)],
        pl.BlockSpec((1, tn), lambda i, j: (0, j)),
    ]
    args = [x2, mods] + [w_in] * IN_TILES_PER_STEP + [gains]
    if rope:
        pos_tiles = SEQ // tm
        for t in tables:
            in_specs.append(pl.BlockSpec((tm, HEAD_DIM), lambda i, j: (i % pos_tiles, 0)))
            args.append(t)
    return pl.pallas_call(
        functools.partial(_inproj_kernel, rope),
        grid=(m // tm, N_IN_TILES // IN_TILES_PER_STEP),
        in_specs=in_specs,
        out_specs=pl.BlockSpec((tm, tn), lambda i, j: (i, j)),
        out_shape=jax.ShapeDtypeStruct((m, IN_WIDTH), BF16),
        scratch_shapes=[pltpu.VMEM((tm, d), BF16)],
        compiler_params=_params("parallel", "arbitrary"),
        name="inproj",
    )(*args)


def _rope_tables(half):
    t = jnp.arange(SEQ, dtype=jnp.int32)
    row = (t // GRID_W).astype(F32)[:, None]
    col = (t % GRID_W).astype(F32)[:, None]
    lane = np.arange(HEAD_DIM)
    freqs = ROPE_THETA ** (-jnp.asarray(lane % half, dtype=F32) / half)
    is_col = jnp.asarray((lane % (4 * half)) >= 2 * half)[None, :]
    ang = jnp.where(is_col, col, row) * freqs[None, :]
    first = jnp.asarray((lane % (2 * half)) < half)[None, :]
    cos = jnp.cos(ang)
    sin = jnp.sin(ang)
    return cos, jnp.where(first, -sin, sin)


_NT = (((1,), (1,)), ((), ()))


def _softmax_step(s, v, state):
    m_cur = jnp.max(s, axis=-1, keepdims=True)
    if state is None:
        m_new = m_cur
        p = jnp.exp2(s - m_new)
        l_new = jnp.sum(p, axis=-1, keepdims=True)
        acc = jnp.dot(p.astype(BF16), v, preferred_element_type=F32)
    else:
        m_old, l_old, acc_old = state
        m_new = jnp.maximum(m_old, m_cur)
        alpha = jnp.exp2(m_old - m_new)
        p = jnp.exp2(s - m_new)
        l_new = alpha * l_old + jnp.sum(p, axis=-1, keepdims=True)
        acc = alpha * acc_old + jnp.dot(p.astype(BF16), v, preferred_element_type=F32)
    return m_new, l_new, acc


def _attend(qs, kc, vc, kl_ref, vl_ref):
    def scores(k):
        return lax.dot_general(qs, k, _NT, preferred_element_type=F32)

    state = _softmax_step(scores(kc), vc, None)
    if kl_ref is not None:
        for start in range(0, kl_ref.shape[0], KV_CHUNK):
            state = _softmax_step(scores(kl_ref[start:start + KV_CHUNK, :]),
                                  vl_ref[start:start + KV_CHUNK, :], state)
    _, l, acc = state
    return acc / l


def _store_normed(heads, gain_ref, o_ref):
    ms = sum(jnp.sum(o * o, axis=-1, keepdims=True) for o in heads) * (1.0 / (len(heads) * HEAD_DIM))
    r = lax.rsqrt(ms + RMS_EPS)
    for h, o in enumerate(heads):
        o_ref[:, _head(h)] = (o * r * gain_ref[:, _head(h)]).astype(o_ref.dtype)


def _gqa_heads(q, n_kv, group, pg_ref, k_h, v_h, kl_refs, vl_refs):
    tq = q.shape[0]
    heads = []
    for g in range(n_kv):
        qs = jnp.concatenate([q[:, _head(g * group + i)] for i in range(group)], axis=0)
        o = _attend(qs, pg_ref[:, _head(k_h + g)], pg_ref[:, _head(v_h + g)],
                    kl_refs[g] if kl_refs else None, vl_refs[g] if vl_refs else None)
        heads += [o[i * tq:(i + 1) * tq] for i in range(group)]
    return heads


def _gqa_lat_kernel(q_ref, pg_ref, *rest):
    kl_refs = rest[:A_KV_HEADS]
    vl_refs = rest[A_KV_HEADS:2 * A_KV_HEADS]
    gain_ref, o_ref = rest[2 * A_KV_HEADS:]
    heads = _gqa_heads(q_ref[...], A_KV_HEADS, A_GROUP, pg_ref, AK_H, AV_H, kl_refs, vl_refs)
    _store_normed(heads, gain_ref, o_ref)


def _gqa_ctx_kernel(n_kv, group, q_h, k_h, v_h, pg_ref, gain_ref, o_ref):
    q = pg_ref[:, q_h * HEAD_DIM:(q_h + n_kv * group) * HEAD_DIM]
    _store_normed(_gqa_heads(q, n_kv, group, pg_ref, k_h, v_h, None, None), gain_ref, o_ref)


def _gqa_lat(px, pg, gain):
    b = px.shape[0]
    head_spec = lambda h: pl.BlockSpec((None, SEQ, HEAD_DIM), lambda bi, qi: (bi, 0, h))
    kv_heads = [AK_H + g for g in range(A_KV_HEADS)] + [AV_H + g for g in range(A_KV_HEADS)]
    return pl.pallas_call(
        _gqa_lat_kernel,
        grid=(b, SEQ // TQ_A),
        in_specs=[
            pl.BlockSpec((None, TQ_A, A_WIDTH), lambda bi, qi: (bi, qi, AQ_H * HEAD_DIM // A_WIDTH)),
            pl.BlockSpec((None, CTX_LEN, IN_WIDTH), lambda bi, qi: (bi, 0, 0)),
            *[head_spec(h) for h in kv_heads],
            pl.BlockSpec((1, A_WIDTH), lambda bi, qi: (0, 0)),
        ],
        out_specs=pl.BlockSpec((None, TQ_A, A_WIDTH), lambda bi, qi: (bi, qi, 0)),
        out_shape=jax.ShapeDtypeStruct((b, SEQ, A_WIDTH), BF16),
        compiler_params=_params("parallel", "arbitrary"),
        name="gqa_attention",
    )(px, pg, *([px] * len(kv_heads)), gain)


def _gqa_ctx(pg, gain, n_kv, group, q_h, k_h, v_h):
    b = pg.shape[0]
    width = n_kv * group * HEAD_DIM
    return pl.pallas_call(
        functools.partial(_gqa_ctx_kernel, n_kv, group, q_h, k_h, v_h),
        grid=(b,),
        in_specs=[
            pl.BlockSpec((None, CTX_LEN, IN_WIDTH), lambda bi: (bi, 0, 0)),
            pl.BlockSpec((1, width), lambda bi: (0, 0)),
        ],
        out_specs=pl.BlockSpec((None, CTX_LEN, width), lambda bi: (bi, 0, 0)),
        out_shape=jax.ShapeDtypeStruct((b, CTX_LEN, width), BF16),
        compiler_params=_params("parallel"),
        name="ctx_attention",
    )(pg, gain)


def _diff_heads(q, lam_init, lam_ref, gain_ref, pg_ref, kl_refs, vl_refs, o_ref):
    tq = q.shape[0]
    lv = lam_ref[...]
    lam = (jnp.exp(jnp.sum(lv[0:1] * lv[1:2], axis=-1, keepdims=True))
           - jnp.exp(jnp.sum(lv[2:3] * lv[3:4], axis=-1, keepdims=True)) + lam_init)
    low = lax.broadcasted_iota(jnp.int32, (tq, HEAD_DIM), 1) < B_SUB_DIM
    zero = jnp.zeros((tq, HEAD_DIM), q.dtype)
    for h in range(B_HEADS):
        qh = q[:, _head(h)]
        qs = jnp.concatenate([jnp.where(low, qh, zero), jnp.where(low, zero, qh)], axis=0)
        o = _attend(qs, pg_ref[:, _head(BK_H + h)], pg_ref[:, _head(BV_H + h)],
                    kl_refs[h] if kl_refs else None, vl_refs[h] if vl_refs else None)
        o = _rms(o[:tq] - lam * o[tq:]) * gain_ref[...] * (1.0 - lam_init)
        o_ref[:, _head(h)] = o.astype(o_ref.dtype)


def _diff_lat_kernel(lam_init, q_ref, lam_ref, gain_ref, pg_ref, *rest):
    kl_refs = rest[:B_HEADS]
    vl_refs = rest[B_HEADS:2 * B_HEADS]
    o_ref = rest[2 * B_HEADS]
    _diff_heads(q_ref[...], lam_init, lam_ref, gain_ref, pg_ref, kl_refs, vl_refs, o_ref)


def _diff_ctx_kernel(lam_init, lam_ref, gain_ref, pg_ref, o_ref):
    q = pg_ref[:, BQ_H * HEAD_DIM:BK_H * HEAD_DIM]
    _diff_heads(q, lam_init, lam_ref, gain_ref, pg_ref, None, None, o_ref)


def _diff_lat(px, pg, lam_vecs, gain, lam_init):
    b = px.shape[0]
    head_spec = lambda h: pl.BlockSpec((None, SEQ, HEAD_DIM), lambda bi, qi: (bi, 0, h))
    kv_heads = [BK_H + h for h in range(B_HEADS)] + [BV_H + h for h in range(B_HEADS)]
    return pl.pallas_call(
        functools.partial(_diff_lat_kernel, lam_init),
        grid=(b, SEQ // TQ_B),
        in_specs=[
            pl.BlockSpec((None, TQ_B, B_WIDTH), lambda bi, qi: (bi, qi, BQ_H * HEAD_DIM // B_WIDTH)),
            pl.BlockSpec((4, B_SUB_DIM), lambda bi, qi: (0, 0)),
            pl.BlockSpec((1, HEAD_DIM), lambda bi, qi: (0, 0)),
            pl.BlockSpec((None, CTX_LEN, IN_WIDTH), lambda bi, qi: (bi, 0, 0)),
            *[head_spec(h) for h in kv_heads],
        ],
        out_specs=pl.BlockSpec((None, TQ_B, B_WIDTH), lambda bi, qi: (bi, qi, 0)),
        out_shape=jax.ShapeDtypeStruct((b, SEQ, B_WIDTH), BF16),
        compiler_params=_params("parallel", "arbitrary"),
        name="diff_attention",
    )(px, lam_vecs, gain, pg, *([px] * len(kv_heads)))


def _diff_ctx(pg, lam_vecs, gain, lam_init):
    b = pg.shape[0]
    return pl.pallas_call(
        functools.partial(_diff_ctx_kernel, lam_init),
        grid=(b,),
        in_specs=[
            pl.BlockSpec((4, B_SUB_DIM), lambda bi: (0, 0)),
            pl.BlockSpec((1, HEAD_DIM), lambda bi: (0, 0)),
            pl.BlockSpec((None, CTX_LEN, IN_WIDTH), lambda bi: (bi, 0, 0)),
        ],
        out_specs=pl.BlockSpec((None, CTX_LEN, B_WIDTH), lambda bi: (bi, 0, 0)),
        out_shape=jax.ShapeDtypeStruct((b, CTX_LEN, B_WIDTH), BF16),
        compiler_params=_params("parallel"),
        name="diff_ctx_attention",
    )(lam_vecs, gain, pg)


def _nbr_kernel(q_ref, k_ref, v_ref, pg_ref, bias_ref, gain_ref, o_ref):
    j = pl.program_id(1)
    key_row0 = jnp.clip(NB_ROWS * j - NA_ROWS // 2, 0, GRID_H - NB_KROWS)
    start = pl.multiple_of(key_row0 * GRID_W, NB_ROWS * GRID_W)
    heads = []
    for h in range(C_HEADS):
        q = q_ref[:, _head(h)]
        kw = k_ref[pl.ds(start, NB_K), _head(h)]
        vw = v_ref[pl.ds(start, NB_K), _head(h)]
        kc = pg_ref[:, _head(CK_H + h)]
        vc = pg_ref[:, _head(CV_H + h)]
        s_w = lax.dot_general(q, kw, _NT, preferred_element_type=F32) + bias_ref[h]
        s_c = lax.dot_general(q, kc, _NT, preferred_element_type=F32)
        m = jnp.maximum(jnp.max(s_w, axis=-1, keepdims=True), jnp.max(s_c, axis=-1, keepdims=True))
        p_w = jnp.exp2(s_w - m)
        p_c = jnp.exp2(s_c - m)
        l = jnp.sum(p_w, axis=-1, keepdims=True) + jnp.sum(p_c, axis=-1, keepdims=True)
        o = (jnp.dot(p_w.astype(BF16), vw, preferred_element_type=F32)
             + jnp.dot(p_c.astype(BF16), vc, preferred_element_type=F32))
        heads.append(o / l)
    _store_normed(heads, gain_ref, o_ref)


def _nbr_bias(rel_bias):
    n_blocks = GRID_H // NB_ROWS
    col = np.arange(GRID_W)
    cs = np.clip(col - NA_COLS // 2, 0, GRID_W - NA_COLS)
    col_ok = (col[None, :] >= cs[:, None]) & (col[None, :] < cs[:, None] + NA_COLS)
    dc = col[None, :] - col[:, None] + NA_COLS - 1
    onehot = (dc[None] == np.arange(2 * NA_COLS - 1)[:, None, None]) & col_ok[None]
    per_row = jnp.einsum('hrt,tck->hrck', rel_bias.astype(F32) * LOG2E, jnp.asarray(onehot, F32),
                         precision=lax.Precision.HIGHEST)
    per_row = jnp.where(col_ok, per_row, MASK_VALUE)
    masked = jnp.full((C_HEADS, GRID_W, GRID_W), MASK_VALUE, F32)
    classes = []
    for j in (0, 1, n_blocks - 1):
        key_row0 = int(np.clip(NB_ROWS * j - NA_ROWS // 2, 0, GRID_H - NB_KROWS))
        strips = []
        for i in range(NB_ROWS):
            r = NB_ROWS * j + i
            rs = int(np.clip(r - NA_ROWS // 2, 0, GRID_H - NA_ROWS))
            blocks = []
            for a in range(NB_KROWS):
                kr = key_row0 + a
                in_window = rs <= kr < rs + NA_ROWS
                blocks.append(per_row[:, kr - r + NA_ROWS - 1] if in_window else masked)
            strips.append(jnp.concatenate(blocks, axis=-1))
        classes.append(jnp.concatenate(strips, axis=-2))
    return jnp.stack(classes)


def _nbr(px, pg, bias_cls, gain):
    b = px.shape[0]
    n_blocks = GRID_H // NB_ROWS
    blk = lambda h: h * HEAD_DIM // C_WIDTH

    def cls(j):
        return jnp.where(j == 0, 0, jnp.where(j == n_blocks - 1, 2, 1))

    return pl.pallas_call(
        _nbr_kernel,
        grid=(b, n_blocks),
        in_specs=[
            pl.BlockSpec((None, NB_Q, C_WIDTH), lambda bi, j: (bi, j, blk(CQ_H))),
            pl.BlockSpec((None, SEQ, C_WIDTH), lambda bi, j: (bi, 0, blk(CK_H)),
                         pipeline_mode=pl.Buffered(1)),
            pl.BlockSpec((None, SEQ, C_WIDTH), lambda bi, j: (bi, 0, blk(CV_H)),
                         pipeline_mode=pl.Buffered(1)),
            pl.BlockSpec((None, CTX_LEN, IN_WIDTH), lambda bi, j: (bi, 0, 0)),
            pl.BlockSpec((None, C_HEADS, NB_Q, NB_K), lambda bi, j: (cls(j), 0, 0, 0)),
            pl.BlockSpec((1, C_WIDTH), lambda bi, j: (0, 0)),
        ],
        out_specs=pl.BlockSpec((None, NB_Q, C_WIDTH), lambda bi, j: (bi, j, 0)),
        out_shape=jax.ShapeDtypeStruct((b, SEQ, C_WIDTH), BF16),
        compiler_params=_params("parallel", "arbitrary"),
        name="nbr_attention",
    )(px, px, px, pg, bias_cls, gain)


def _outproj_kernel(x_ref, mod_ref, a_ref, b_ref, c_ref, w_ref, o_ref):
    y = (jnp.dot(a_ref[...], w_ref[:A_WIDTH, :].astype(BF16), preferred_element_type=F32)
         + jnp.dot(b_ref[...], w_ref[A_WIDTH:A_WIDTH + B_WIDTH, :].astype(BF16),
                   preferred_element_type=F32)
         + jnp.dot(c_ref[...], w_ref[A_WIDTH + B_WIDTH:, :].astype(BF16),
                   preferred_element_type=F32))
    o_ref[...] = x_ref[...] + mod_ref[5:6, :] * y


def _outproj(x2, mods, mod_at, layer, a2, b2, c2, w_out, tm):
    m, d = x2.shape
    return pl.pallas_call(
        _outproj_kernel,
        grid=(m // tm, d // TN),
        in_specs=[
            pl.BlockSpec((tm, TN), lambda i, j: (i, j)),
            pl.BlockSpec((None, None, N_MOD, TN), lambda i, j: (layer, mod_at(i), 0, j)),
            pl.BlockSpec((tm, A_WIDTH), lambda i, j: (i, 0)),
            pl.BlockSpec((tm, B_WIDTH), lambda i, j: (i, 0)),
            pl.BlockSpec((tm, C_WIDTH), lambda i, j: (i, 0)),
            pl.BlockSpec((None, MIX_WIDTH, TN), lambda i, j: (layer, 0, j)),
        ],
        out_specs=pl.BlockSpec((tm, TN), lambda i, j: (i, j)),
        out_shape=jax.ShapeDtypeStruct((m, d), F32),
        compiler_params=_params("parallel", "arbitrary"),
        name="outproj",
    )(x2, mods, a2, b2, c2, w_out)


def _column_gains(a_q, a_k, b_q, b_k, c_q, c_k):
    a_scale = LOG2E * HEAD_DIM ** -0.5
    b_scale = LOG2E * B_SUB_DIM ** -0.5
    parts = (
        jnp.tile(c_q * a_scale, C_HEADS), jnp.tile(c_k, C_HEADS), jnp.ones((C_WIDTH,), F32),
        jnp.tile(a_q * a_scale, A_HEADS), jnp.tile(a_k, A_KV_HEADS), jnp.ones((A_KV_WIDTH,), F32),
        jnp.tile(b_q * b_scale, 2 * B_HEADS), jnp.tile(b_k, 2 * B_HEADS), jnp.ones((B_WIDTH,), F32),
    )
    return jnp.concatenate(parts).reshape(1, IN_WIDTH)


def kernel(x, c, ctx, c_ctx, w_mod, b_mod, ffn1_w_gu, ffn1_w_down, ffn2_w_gu, ffn2_w_down,
           w_in, w_out, a_q_gain, a_k_gain, a_out_gain, b_q_gain, b_k_gain, b_lambda, b_out_gain,
           c_q_gain, c_k_gain, c_rel_bias, c_out_gain):
    b, s, d = x.shape
    n_ctx = ctx.shape[1]
    assert (b, s, d, n_ctx) == (2, SEQ, D_MODEL, CTX_LEN)

    cvec = jnp.zeros((8, d), F32).at[:b].set(c).at[b].set(c_ctx)
    mods = _modulation(cvec, w_mod, b_mod).reshape(DEPTH, 8, N_MOD, d)
    tables = _rope_tables(HEAD_DIM // 4) + _rope_tables(B_SUB_DIM // 4)

    x2 = x.reshape(b * s, d)
    g2 = ctx.reshape(b * n_ctx, d)
    tm_g = b * n_ctx

    mod_x = lambda i: i // (s // TM)
    mod_x_out = lambda i: i // (s // TM_OUT)
    mod_x_ffn = lambda i: i // (s // TM_FFN)
    mod_g = lambda i: b

    for l in range(DEPTH):
        with_ctx = l < DEPTH - 1
        lam_init = 0.8 - 0.6 * math.exp(-0.3 * l)
        gains = _column_gains(a_q_gain[l], a_k_gain[l], b_q_gain[l], b_k_gain[l],
                              c_q_gain[l], c_k_gain[l])
        a_gain = a_out_gain[l].reshape(1, A_WIDTH)
        b_gain = b_out_gain[l].reshape(1, HEAD_DIM)
        c_gain = c_out_gain[l].reshape(1, C_WIDTH)

        x2 = _ffn(x2, mods, mod_x_ffn, l, ffn1_w_gu, ffn1_w_down, 0, TM_FFN, TF)
        g2 = _ffn(g2, mods, mod_g, l, ffn1_w_gu, ffn1_w_down, 0, tm_g, TF_CTX)

        px = _inproj(x2, mods, mod_x, l, w_in, gains, tables, TM).reshape(b, s, IN_WIDTH)
        pg = _inproj(g2, mods, mod_g, l, w_in, gains, None, tm_g).reshape(b, n_ctx, IN_WIDTH)

        a_lat = _gqa_lat(px, pg, a_gain)
        b_lat = _diff_lat(px, pg, b_lambda[l], b_gain, lam_init)
        c_lat = _nbr(px, pg, _nbr_bias(c_rel_bias[l]), c_gain)

        x2 = _outproj(x2, mods, mod_x_out, l, a_lat.reshape(b * s, A_WIDTH),
                      b_lat.reshape(b * s, B_WIDTH), c_lat.reshape(b * s, C_WIDTH), w_out, TM_OUT)
        x2 = _ffn(x2, mods, mod_x_ffn, l, ffn2_w_gu, ffn2_w_down, 6, TM_FFN, TF)

        if with_ctx:
            a_ctx = _gqa_ctx(pg, a_gain, A_KV_HEADS, A_GROUP, AQ_H, AK_H, AV_H)
            b_ctx = _diff_ctx(pg, b_lambda[l], b_gain, lam_init)
            c_ctx = _gqa_ctx(pg, c_gain, C_HEADS, 1, CQ_H, CK_H, CV_H)
            g2 = _outproj(g2, mods, mod_g, l, a_ctx.reshape(tm_g, A_WIDTH),
                          b_ctx.reshape(tm_g, B_WIDTH), c_ctx.reshape(tm_g, C_WIDTH), w_out, tm_g)
            g2 = _ffn(g2, mods, mod_g, l, ffn2_w_gu, ffn2_w_down, 6, tm_g, TF_CTX)

    return x2.reshape(b, s, d)
```

```python
import functools
import math

import numpy as np
import jax
import jax.numpy as jnp
from jax import lax
from jax.experimental import pallas as pl
from jax.experimental.pallas import tpu as pltpu

F32 = jnp.float32
BF16 = jnp.bfloat16

D_MODEL = 2048
SEQ = 4096
DEPTH = 2
GRID_W = 64
GRID_H = SEQ // GRID_W
CTX_LEN = 256
HEAD_DIM = 128
N_MOD = 9
FFN_HIDDEN = 5632
RMS_EPS = 1e-6
ROPE_THETA = 10000.0

A_HEADS = 6
A_KV_HEADS = 2
A_GROUP = A_HEADS // A_KV_HEADS
B_HEADS = 4
C_HEADS = 6
B_SUB_DIM = HEAD_DIM // 2
A_WIDTH = A_HEADS * HEAD_DIM
A_KV_WIDTH = A_KV_HEADS * HEAD_DIM
B_WIDTH = B_HEADS * HEAD_DIM
C_WIDTH = C_HEADS * HEAD_DIM
MIX_WIDTH = A_WIDTH + B_WIDTH + C_WIDTH
IN_WIDTH = A_WIDTH + 2 * A_KV_WIDTH + 3 * B_WIDTH + 3 * C_WIDTH
NA_ROWS = 8
NA_COLS = 16

CQ_H = 0
CK_H = CQ_H + C_HEADS
CV_H = CK_H + C_HEADS
AQ_H = CV_H + C_HEADS
AK_H = AQ_H + A_HEADS
AV_H = AK_H + A_KV_HEADS
BQ_H = AV_H + A_KV_HEADS
BK_H = BQ_H + B_HEADS
BV_H = BK_H + B_HEADS

VMEM_LIMIT = 56 * 1024 * 1024
FFN_VMEM_LIMIT = 61 * 1024 * 1024
MASK_VALUE = -1e30
LOG2E = math.log2(math.e)

TM = 1024
TM_OUT = 2048
TM_FFN = 1024
TF = 512
TF_CTX = 512
TN = 256
TN_OUT = 512
IN_TILES_PER_STEP = 2
ROW_GROUPS = 4
EDGE_ROWS = 256
TQ_A = 256
TQ_B = 256
KV_CHUNK = 2048
NB_ROWS = 4
NB_Q = NB_ROWS * GRID_W
NB_KROWS = 12
NB_K = NB_KROWS * GRID_W

N_IN_TILES = IN_WIDTH // TN
SRC_SHIFT = (IN_WIDTH - 3 * C_WIDTH) // TN


def _silu(v):
    return v / (1.0 + jnp.exp(-v))


def _rms(v):
    return v * lax.rsqrt(jnp.mean(v * v, axis=-1, keepdims=True) + RMS_EPS)


def _head(h):
    return slice(h * HEAD_DIM, (h + 1) * HEAD_DIM)


def _params(*sem, vmem_limit=VMEM_LIMIT):
    return pltpu.CompilerParams(dimension_semantics=sem, vmem_limit_bytes=vmem_limit)


def _mod_kernel(c_ref, w_ref, b_ref, o_ref):
    s = _silu(c_ref[...]).astype(BF16)
    o_ref[...] = jnp.dot(s, w_ref[...].astype(BF16), preferred_element_type=F32) + b_ref[...]


def _modulation(cvec, w_mod, b_mod):
    depth, d, n = w_mod.shape
    tn = 1024
    return pl.pallas_call(
        _mod_kernel,
        grid=(depth, n // tn),
        in_specs=[
            pl.BlockSpec((8, d), lambda l, j: (0, 0)),
            pl.BlockSpec((None, d, tn), lambda l, j: (l, 0, j)),
            pl.BlockSpec((None, 1, tn), lambda l, j: (l, 0, j)),
        ],
        out_specs=pl.BlockSpec((None, 8, tn), lambda l, j: (l, 0, j)),
        out_shape=jax.ShapeDtypeStruct((depth, 8, n), F32),
        compiler_params=_params("parallel", "parallel"),
        name="modulation",
    )(cvec, w_mod, b_mod.reshape(depth, 1, n))


def _ffn_kernel(mod_row, nf, x_ref, mod_ref, wg_ref, wu_ref, wd_ref, o_ref, xn_ref):
    f = pl.program_id(1)
    tm = x_ref.shape[0]

    def step(first, last, groups):
        wg = wg_ref[...].astype(BF16)
        wu = wu_ref[...].astype(BF16)
        wd = wd_ref[...].astype(BF16)
        for r in range(groups):
            rows = slice(r * (tm // groups), (r + 1) * (tm // groups))
            if first:
                shift = mod_ref[mod_row:mod_row + 1, :]
                scale = mod_ref[mod_row + 1:mod_row + 2, :]
                xn = (_rms(x_ref[rows, :]) * (1.0 + scale) + shift).astype(BF16)
                xn_ref[rows, :] = xn
            else:
                xn = xn_ref[rows, :]
            g = jnp.dot(xn, wg, preferred_element_type=F32)
            u = jnp.dot(xn, wu, preferred_element_type=F32)
            a = (_silu(g) * u).astype(BF16)
            y = jnp.dot(a, wd, preferred_element_type=F32)
            if not first:
                y = o_ref[rows, :] + y
            if last:
                gate = mod_ref[mod_row + 2:mod_row + 3, :]
                y = x_ref[rows, :] + (0.5 * gate) * y
            o_ref[rows, :] = y

    edge_groups = tm // EDGE_ROWS
    pl.when(f == 0)(lambda: step(True, False, edge_groups))
    pl.when((f > 0) & (f < nf - 1))(lambda: step(False, False, 1))
    pl.when(f == nf - 1)(lambda: step(False, True, edge_groups))


def _ffn(x2, mods, mod_at, layer, w_gu, w_down, mod_row, tm, tf):
    m, d = x2.shape
    hidden = w_down.shape[1]
    nf = hidden // tf
    return pl.pallas_call(
        functools.partial(_ffn_kernel, mod_row, nf),
        grid=(m // tm, nf),
        in_specs=[
            pl.BlockSpec((tm, d), lambda i, f: (i, 0), pipeline_mode=pl.Buffered(1)),
            pl.BlockSpec((None, None, N_MOD, d), lambda i, f: (layer, mod_at(i), 0, 0)),
            pl.BlockSpec((None, d, tf), lambda i, f: (layer, 0, f)),
            pl.BlockSpec((None, d, tf), lambda i, f: (layer, 0, nf + f)),
            pl.BlockSpec((None, tf, d), lambda i, f: (layer, f, 0)),
        ],
        out_specs=pl.BlockSpec((tm, d), lambda i, f: (i, 0)),
        out_shape=jax.ShapeDtypeStruct((m, d), F32),
        scratch_shapes=[pltpu.VMEM((tm, d), BF16)],
        compiler_params=_params("parallel", "arbitrary", vmem_limit=FFN_VMEM_LIMIT),
        name="ffn",
    )(x2, mods, w_gu, w_gu, w_down)


def _norm_lanes(y, width):
    y2 = y * y
    if width == HEAD_DIM:
        ms = jnp.mean(y2, axis=-1, keepdims=True)
    else:
        lane = lax.broadcasted_iota(jnp.int32, y.shape, 1)
        low = lane < width
        s_lo = jnp.sum(jnp.where(low, y2, 0.0), axis=-1, keepdims=True)
        s_hi = jnp.sum(jnp.where(low, 0.0, y2), axis=-1, keepdims=True)
        ms = jnp.where(low, s_lo, s_hi) * (1.0 / width)
    return y * lax.rsqrt(ms + RMS_EPS)


def _tile_modes():
    per_head = (['c'] * (2 * C_HEADS) + ['p'] * C_HEADS + ['a'] * (A_HEADS + A_KV_HEADS)
                + ['p'] * A_KV_HEADS + ['b'] * (2 * B_HEADS) + ['p'] * B_HEADS)
    hpt = TN // HEAD_DIM
    tiles = [set(per_head[t * hpt:(t + 1) * hpt]) for t in range(N_IN_TILES)]
    assert all(len(t) == 1 for t in tiles)
    return [t.pop() for t in tiles]


def _inproj_kernel(rope, x_ref, mod_ref, *rest):
    w_refs = rest[:IN_TILES_PER_STEP]
    gain_ref = rest[IN_TILES_PER_STEP]
    rest = rest[IN_TILES_PER_STEP + 1:]
    if rope:
        ca_ref, sa_ref, cb_ref, sb_ref, o_ref, xn_ref = rest
    else:
        o_ref, xn_ref = rest
    j = pl.program_id(1)
    heads_per_tile = TN // HEAD_DIM
    group_rows = x_ref.shape[0] // ROW_GROUPS
    rope_tables = {'a': (ca_ref, sa_ref, HEAD_DIM // 4),
                   'b': (cb_ref, sb_ref, B_SUB_DIM // 4)} if rope else {}
    lane = lax.broadcasted_iota(jnp.int32, (group_rows, HEAD_DIM), 1)
    norm_width = {'a': HEAD_DIM, 'b': B_SUB_DIM, 'c': HEAD_DIM, 'p': 0}

    def emit(modes, first=False):
        ws = [w_ref[...].astype(BF16) for w_ref in w_refs]
        for r in range(ROW_GROUPS):
            rows = slice(r * group_rows, (r + 1) * group_rows)
            if first:
                shift = mod_ref[3:4, :]
                scale = mod_ref[4:5, :]
                xn = (_rms(x_ref[rows, :]) * (1.0 + scale) + shift).astype(BF16)
                xn_ref[rows, :] = xn
            else:
                xn = xn_ref[rows, :]
            for t, mode in enumerate(modes):
                y = jnp.dot(xn, ws[t], preferred_element_type=F32)
                for hh in range(heads_per_tile):
                    cols = _head(t * heads_per_tile + hh)
                    v = y[:, _head(hh)]
                    if norm_width[mode]:
                        v = _norm_lanes(v, norm_width[mode]) * gain_ref[:, cols]
                    if mode in rope_tables:
                        c_ref, s_ref, half = rope_tables[mode]
                        partner = jnp.where((lane & (2 * half - 1)) < half,
                                            pltpu.roll(v, HEAD_DIM - half, 1), pltpu.roll(v, half, 1))
                        v = v * c_ref[rows, :] + partner * s_ref[rows, :]
                    o_ref[rows, cols] = v.astype(o_ref.dtype)

    tile_modes = _tile_modes()
    steps = {}
    for step in range(N_IN_TILES // IN_TILES_PER_STEP):
        modes = tuple(tile_modes[step * IN_TILES_PER_STEP:(step + 1) * IN_TILES_PER_STEP])
        steps.setdefault(modes, []).append(step)

    pl.when(j == 0)(functools.partial(emit, tuple(tile_modes[:IN_TILES_PER_STEP]), first=True))
    for modes, where in steps.items():
        later = [t for t in where if t > 0]
        if later:
            cond = functools.reduce(lambda a, b: a | b, [j == t for t in later])
            pl.when(cond)(functools.partial(emit, modes))


def _inproj(x2, mods, mod_at, layer, w_in, gains, tables, tm):
    m, d = x2.shape
    rope = tables is not None
    tn = TN * IN_TILES_PER_STEP
    src = lambda t: lax.rem(t + SRC_SHIFT, N_IN_TILES)
    w_spec = lambda k: pl.BlockSpec((None, d, TN),
                                    lambda i, j: (layer, 0, src(j * IN_TILES_PER_STEP + k)))
    in_specs = [
        pl.BlockSpec((tm, d), lambda i, j: (i, 0), pipeline_mode=pl.Buffered(1)),
        pl.BlockSpec((None, None, N_MOD, d), lambda i, j: (layer, mod_at(i), 0, 0)),
        *[w_spec(k) for k in range(IN_TILES_PER_STEP)],
        pl.BlockSpec((1, tn), lambda i, j: (0, j)),
    ]
    args = [x2, mods] + [w_in] * IN_TILES_PER_STEP + [gains]
    if rope:
        pos_tiles = SEQ // tm
        for t in tables:
            in_specs.append(pl.BlockSpec((tm, HEAD_DIM), lambda i, j: (i % pos_tiles, 0)))
            args.append(t)
    return pl.pallas_call(
        functools.partial(_inproj_kernel, rope),
        grid=(m // tm, N_IN_TILES // IN_TILES_PER_STEP),
        in_specs=in_specs,
        out_specs=pl.BlockSpec((tm, tn), lambda i, j: (i, j)),
        out_shape=jax.ShapeDtypeStruct((m, IN_WIDTH), BF16),
        scratch_shapes=[pltpu.VMEM((tm, d), BF16)],
        compiler_params=_params("parallel", "arbitrary"),
        name="inproj",
    )(*args)


def _rope_tables(half):
    t = jnp.arange(SEQ, dtype=jnp.int32)
    row = (t // GRID_W).astype(F32)[:, None]
    col = (t % GRID_W).astype(F32)[:, None]
    lane = np.arange(HEAD_DIM)
    freqs = ROPE_THETA ** (-jnp.asarray(lane % half, dtype=F32) / half)
    is_col = jnp.asarray((lane % (4 * half)) >= 2 * half)[None, :]
    ang = jnp.where(is_col, col, row) * freqs[None, :]
    first = jnp.asarray((lane % (2 * half)) < half)[None, :]
    cos = jnp.cos(ang)
    sin = jnp.sin(ang)
    return cos, jnp.where(first, -sin, sin)


_NT = (((1,), (1,)), ((), ()))


def _softmax_step(s, v, state):
    m_cur = jnp.max(s, axis=-1, keepdims=True)
    if state is None:
        m_new = m_cur
        p = jnp.exp2(s - m_new)
        l_new = jnp.sum(p, axis=-1, keepdims=True)
        acc = jnp.dot(p.astype(BF16), v, preferred_element_type=F32)
    else:
        m_old, l_old, acc_old = state
        m_new = jnp.maximum(m_old, m_cur)
        alpha = jnp.exp2(m_old - m_new)
        p = jnp.exp2(s - m_new)
        l_new = alpha * l_old + jnp.sum(p, axis=-1, keepdims=True)
        acc = alpha * acc_old + jnp.dot(p.astype(BF16), v, preferred_element_type=F32)
    return m_new, l_new, acc


def _attend(qs, kc, vc, kl_ref, vl_ref):
    def scores(k):
        return lax.dot_general(qs, k, _NT, preferred_element_type=F32)

    state = _softmax_step(scores(kc), vc, None)
    if kl_ref is not None:
        for start in range(0, kl_ref.shape[0], KV_CHUNK):
            state = _softmax_step(scores(kl_ref[start:start + KV_CHUNK, :]),
                                  vl_ref[start:start + KV_CHUNK, :], state)
    _, l, acc = state
    return acc / l


def _store_normed(heads, gain_ref, o_ref):
    ms = sum(jnp.sum(o * o, axis=-1, keepdims=True) for o in heads) * (1.0 / (len(heads) * HEAD_DIM))
    r = lax.rsqrt(ms + RMS_EPS)
    for h, o in enumerate(heads):
        o_ref[:, _head(h)] = (o * r * gain_ref[:, _head(h)]).astype(o_ref.dtype)


def _gqa_heads(q, n_kv, group, pg_ref, k_h, v_h, kl_refs, vl_refs):
    tq = q.shape[0]
    heads = []
    for g in range(n_kv):
        qs = jnp.concatenate([q[:, _head(g * group + i)] for i in range(group)], axis=0)
        o = _attend(qs, pg_ref[:, _head(k_h + g)], pg_ref[:, _head(v_h + g)],
                    kl_refs[g] if kl_refs else None, vl_refs[g] if vl_refs else None)
        heads += [o[i * tq:(i + 1) * tq] for i in range(group)]
    return heads


def _gqa_lat_kernel(q_ref, pg_ref, *rest):
    kl_refs = rest[:A_KV_HEADS]
    vl_refs = rest[A_KV_HEADS:2 * A_KV_HEADS]
    gain_ref, o_ref = rest[2 * A_KV_HEADS:]
    heads = _gqa_heads(q_ref[...], A_KV_HEADS, A_GROUP, pg_ref, AK_H, AV_H, kl_refs, vl_refs)
    _store_normed(heads, gain_ref, o_ref)


def _gqa_ctx_kernel(n_kv, group, q_h, k_h, v_h, pg_ref, gain_ref, o_ref):
    q = pg_ref[:, q_h * HEAD_DIM:(q_h + n_kv * group) * HEAD_DIM]
    _store_normed(_gqa_heads(q, n_kv, group, pg_ref, k_h, v_h, None, None), gain_ref, o_ref)


def _kv_specs(k_h, v_h, n_heads):
    heads = [k_h + g for g in range(n_heads)] + [v_h + g for g in range(n_heads)]
    return [pl.BlockSpec((None, SEQ, HEAD_DIM), lambda bi, qi, h=h: (bi, 0, h)) for h in heads]


def _gqa_lat(px, pg, gain):
    b = px.shape[0]
    return pl.pallas_call(
        _gqa_lat_kernel,
        grid=(b, SEQ // TQ_A),
        in_specs=[
            pl.BlockSpec((None, TQ_A, A_WIDTH), lambda bi, qi: (bi, qi, AQ_H * HEAD_DIM // A_WIDTH)),
            pl.BlockSpec((None, CTX_LEN, IN_WIDTH), lambda bi, qi: (bi, 0, 0)),
            *_kv_specs(AK_H, AV_H, A_KV_HEADS),
            pl.BlockSpec((1, A_WIDTH), lambda bi, qi: (0, 0)),
        ],
        out_specs=pl.BlockSpec((None, TQ_A, A_WIDTH), lambda bi, qi: (bi, qi, 0)),
        out_shape=jax.ShapeDtypeStruct((b, SEQ, A_WIDTH), BF16),
        compiler_params=_params("parallel", "arbitrary"),
        name="gqa_attention",
    )(px, pg, *([px] * (2 * A_KV_HEADS)), gain)


def _gqa_ctx(pg, gain, n_kv, group, q_h, k_h, v_h):
    b = pg.shape[0]
    width = n_kv * group * HEAD_DIM
    return pl.pallas_call(
        functools.partial(_gqa_ctx_kernel, n_kv, group, q_h, k_h, v_h),
        grid=(b,),
        in_specs=[
            pl.BlockSpec((None, CTX_LEN, IN_WIDTH), lambda bi: (bi, 0, 0)),
            pl.BlockSpec((1, width), lambda bi: (0, 0)),
        ],
        out_specs=pl.BlockSpec((None, CTX_LEN, width), lambda bi: (bi, 0, 0)),
        out_shape=jax.ShapeDtypeStruct((b, CTX_LEN, width), BF16),
        compiler_params=_params("parallel"),
        name="ctx_attention",
    )(pg, gain)


def _diff_heads(q, lam_init, lam_ref, gain_ref, pg_ref, kl_refs, vl_refs, o_ref):
    tq = q.shape[0]
    lv = lam_ref[...]
    lam = (jnp.exp(jnp.sum(lv[0:1] * lv[1:2], axis=-1, keepdims=True))
           - jnp.exp(jnp.sum(lv[2:3] * lv[3:4], axis=-1, keepdims=True)) + lam_init)
    low = lax.broadcasted_iota(jnp.int32, (tq, HEAD_DIM), 1) < B_SUB_DIM
    zero = jnp.zeros((tq, HEAD_DIM), q.dtype)
    for h in range(B_HEADS):
        qh = q[:, _head(h)]
        qs = jnp.concatenate([jnp.where(low, qh, zero), jnp.where(low, zero, qh)], axis=0)
        o = _attend(qs, pg_ref[:, _head(BK_H + h)], pg_ref[:, _head(BV_H + h)],
                    kl_refs[h] if kl_refs else None, vl_refs[h] if vl_refs else None)
        o = _rms(o[:tq] - lam * o[tq:]) * gain_ref[...] * (1.0 - lam_init)
        o_ref[:, _head(h)] = o.astype(o_ref.dtype)


def _diff_lat_kernel(lam_init, q_ref, lam_ref, gain_ref, pg_ref, *rest):
    kl_refs = rest[:B_HEADS]
    vl_refs = rest[B_HEADS:2 * B_HEADS]
    o_ref = rest[2 * B_HEADS]
    _diff_heads(q_ref[...], lam_init, lam_ref, gain_ref, pg_ref, kl_refs, vl_refs, o_ref)


def _diff_ctx_kernel(lam_init, lam_ref, gain_ref, pg_ref, o_ref):
    q = pg_ref[:, BQ_H * HEAD_DIM:BK_H * HEAD_DIM]
    _diff_heads(q, lam_init, lam_ref, gain_ref, pg_ref, None, None, o_ref)


def _diff_lat(px, pg, lam_vecs, gain, lam_init):
    b = px.shape[0]
    return pl.pallas_call(
        functools.partial(_diff_lat_kernel, lam_init),
        grid=(b, SEQ // TQ_B),
        in_specs=[
            pl.BlockSpec((None, TQ_B, B_WIDTH), lambda bi, qi: (bi, qi, BQ_H * HEAD_DIM // B_WIDTH)),
            pl.BlockSpec((4, B_SUB_DIM), lambda bi, qi: (0, 0)),
            pl.BlockSpec((1, HEAD_DIM), lambda bi, qi: (0, 0)),
            pl.BlockSpec((None, CTX_LEN, IN_WIDTH), lambda bi, qi: (bi, 0, 0)),
            *_kv_specs(BK_H, BV_H, B_HEADS),
        ],
        out_specs=pl.BlockSpec((None, TQ_B, B_WIDTH), lambda bi, qi: (bi, qi, 0)),
        out_shape=jax.ShapeDtypeStruct((b, SEQ, B_WIDTH), BF16),
        compiler_params=_params("parallel", "arbitrary"),
        name="diff_attention",
    )(px, lam_vecs, gain, pg, *([px] * (2 * B_HEADS)))


def _diff_ctx(pg, lam_vecs, gain, lam_init):
    b = pg.shape[0]
    return pl.pallas_call(
        functools.partial(_diff_ctx_kernel, lam_init),
        grid=(b,),
        in_specs=[
            pl.BlockSpec((4, B_SUB_DIM), lambda bi: (0, 0)),
            pl.BlockSpec((1, HEAD_DIM), lambda bi: (0, 0)),
            pl.BlockSpec((None, CTX_LEN, IN_WIDTH), lambda bi: (bi, 0, 0)),
        ],
        out_specs=pl.BlockSpec((None, CTX_LEN, B_WIDTH), lambda bi: (bi, 0, 0)),
        out_shape=jax.ShapeDtypeStruct((b, CTX_LEN, B_WIDTH), BF16),
        compiler_params=_params("parallel"),
        name="diff_ctx_attention",
    )(lam_vecs, gain, pg)


def _nbr_kernel(q_ref, k_ref, v_ref, pg_ref, bias_ref, gain_ref, o_ref):
    j = pl.program_id(1)
    key_row0 = jnp.clip(NB_ROWS * j - NA_ROWS // 2, 0, GRID_H - NB_KROWS)
    start = pl.multiple_of(key_row0 * GRID_W, NB_ROWS * GRID_W)
    heads = []
    for h in range(C_HEADS):
        q = q_ref[:, _head(h)]
        kw = k_ref[pl.ds(start, NB_K), _head(h)]
        vw = v_ref[pl.ds(start, NB_K), _head(h)]
        kc = pg_ref[:, _head(CK_H + h)]
        vc = pg_ref[:, _head(CV_H + h)]
        s_w = lax.dot_general(q, kw, _NT, preferred_element_type=F32) + bias_ref[h]
        s_c = lax.dot_general(q, kc, _NT, preferred_element_type=F32)
        m = jnp.maximum(jnp.max(s_w, axis=-1, keepdims=True), jnp.max(s_c, axis=-1, keepdims=True))
        p_w = jnp.exp2(s_w - m)
        p_c = jnp.exp2(s_c - m)
        l = jnp.sum(p_w, axis=-1, keepdims=True) + jnp.sum(p_c, axis=-1, keepdims=True)
        o = (jnp.dot(p_w.astype(BF16), vw, preferred_element_type=F32)
             + jnp.dot(p_c.astype(BF16), vc, preferred_element_type=F32))
        heads.append(o / l)
    _store_normed(heads, gain_ref, o_ref)


def _nbr_bias(rel_bias):
    n_blocks = GRID_H // NB_ROWS
    col = np.arange(GRID_W)
    cs = np.clip(col - NA_COLS // 2, 0, GRID_W - NA_COLS)
    col_ok = (col[None, :] >= cs[:, None]) & (col[None, :] < cs[:, None] + NA_COLS)
    dc = col[None, :] - col[:, None] + NA_COLS - 1
    onehot = (dc[None] == np.arange(2 * NA_COLS - 1)[:, None, None]) & col_ok[None]
    per_row = jnp.einsum('hrt,tck->hrck', rel_bias.astype(F32) * LOG2E, jnp.asarray(onehot, F32),
                         precision=lax.Precision.HIGHEST)
    per_row = jnp.where(col_ok, per_row, MASK_VALUE)
    n_off = 2 * NA_ROWS - 1
    offs = np.arange(-NB_KROWS, n_off + NB_KROWS)
    padded = jnp.pad(per_row, ((0, 0), (NB_KROWS, NB_KROWS), (0, 0), (0, 0)),
                     constant_values=MASK_VALUE)
    strips = []
    for j in (0, 1, n_blocks - 1):
        key_row0 = int(np.clip(NB_ROWS * j - NA_ROWS // 2, 0, GRID_H - NB_KROWS))
        for i in range(NB_ROWS):
            r = NB_ROWS * j + i
            rs = int(np.clip(r - NA_ROWS // 2, 0, GRID_H - NA_ROWS))
            first = key_row0 - r + NA_ROWS - 1
            run = padded[:, first + NB_KROWS:first + 2 * NB_KROWS]
            a = np.arange(NB_KROWS)
            in_window = (key_row0 + a >= rs) & (key_row0 + a < rs + NA_ROWS)
            assert np.all((offs[first + NB_KROWS:first + 2 * NB_KROWS] >= 0)[in_window])
            strips.append(jnp.where(in_window[None, :, None, None], run, MASK_VALUE))
    dense = jnp.stack(strips).reshape(3, NB_ROWS, C_HEADS, NB_KROWS, GRID_W, GRID_W)
    return jnp.transpose(dense, (0, 2, 1, 4, 3, 5)).reshape(3, C_HEADS, NB_Q, NB_K)


def _nbr(px, pg, bias_cls, gain):
    b = px.shape[0]
    n_blocks = GRID_H // NB_ROWS
    blk = lambda h: h * HEAD_DIM // C_WIDTH

    def cls(j):
        return jnp.where(j == 0, 0, jnp.where(j == n_blocks - 1, 2, 1))

    return pl.pallas_call(
        _nbr_kernel,
        grid=(b, n_blocks),
        in_specs=[
            pl.BlockSpec((None, NB_Q, C_WIDTH), lambda bi, j: (bi, j, blk(CQ_H))),
            pl.BlockSpec((None, SEQ, C_WIDTH), lambda bi, j: (bi, 0, blk(CK_H)),
                         pipeline_mode=pl.Buffered(1)),
            pl.BlockSpec((None, SEQ, C_WIDTH), lambda bi, j: (bi, 0, blk(CV_H)),
                         pipeline_mode=pl.Buffered(1)),
            pl.BlockSpec((None, CTX_LEN, IN_WIDTH), lambda bi, j: (bi, 0, 0)),
            pl.BlockSpec((None, C_HEADS, NB_Q, NB_K), lambda bi, j: (cls(j), 0, 0, 0)),
            pl.BlockSpec((1, C_WIDTH), lambda bi, j: (0, 0)),
        ],
        out_specs=pl.BlockSpec((None, NB_Q, C_WIDTH), lambda bi, j: (bi, j, 0)),
        out_shape=jax.ShapeDtypeStruct((b, SEQ, C_WIDTH), BF16),
        compiler_params=_params("parallel", "arbitrary"),
        name="nbr_attention",
    )(px, px, px, pg, bias_cls, gain)


def _outproj_kernel(x_ref, mod_ref, a_ref, b_ref, c_ref, w_ref, o_ref):
    y = (jnp.dot(a_ref[...], w_ref[:A_WIDTH, :].astype(BF16), preferred_element_type=F32)
         + jnp.dot(b_ref[...], w_ref[A_WIDTH:A_WIDTH + B_WIDTH, :].astype(BF16),
                   preferred_element_type=F32)
         + jnp.dot(c_ref[...], w_ref[A_WIDTH + B_WIDTH:, :].astype(BF16),
                   preferred_element_type=F32))
    o_ref[...] = x_ref[...] + mod_ref[5:6, :] * y


def _outproj(x2, mods, mod_at, layer, a2, b2, c2, w_out, tm):
    m, d = x2.shape
    return pl.pallas_call(
        _outproj_kernel,
        grid=(m // tm, d // TN_OUT),
        in_specs=[
            pl.BlockSpec((tm, TN_OUT), lambda i, j: (i, j)),
            pl.BlockSpec((None, None, N_MOD, TN_OUT), lambda i, j: (layer, mod_at(i), 0, j)),
            pl.BlockSpec((tm, A_WIDTH), lambda i, j: (i, 0)),
            pl.BlockSpec((tm, B_WIDTH), lambda i, j: (i, 0)),
            pl.BlockSpec((tm, C_WIDTH), lambda i, j: (i, 0)),
            pl.BlockSpec((None, MIX_WIDTH, TN_OUT), lambda i, j: (layer, 0, j)),
        ],
        out_specs=pl.BlockSpec((tm, TN_OUT), lambda i, j: (i, j)),
        out_shape=jax.ShapeDtypeStruct((m, d), F32),
        compiler_params=_params("parallel", "arbitrary"),
        name="outproj",
    )(x2, mods, a2, b2, c2, w_out)


def _column_gains(a_q, a_k, b_q, b_k, c_q, c_k):
    a_scale = LOG2E * HEAD_DIM ** -0.5
    b_scale = LOG2E * B_SUB_DIM ** -0.5
    parts = (
        jnp.tile(c_q * a_scale, C_HEADS), jnp.tile(c_k, C_HEADS), jnp.ones((C_WIDTH,), F32),
        jnp.tile(a_q * a_scale, A_HEADS), jnp.tile(a_k, A_KV_HEADS), jnp.ones((A_KV_WIDTH,), F32),
        jnp.tile(b_q * b_scale, 2 * B_HEADS), jnp.tile(b_k, 2 * B_HEADS), jnp.ones((B_WIDTH,), F32),
    )
    return jnp.concatenate(parts).reshape(1, IN_WIDTH)


def kernel(x, c, ctx, c_ctx, w_mod, b_mod, ffn1_w_gu, ffn1_w_down, ffn2_w_gu, ffn2_w_down,
           w_in, w_out, a_q_gain, a_k_gain, a_out_gain, b_q_gain, b_k_gain, b_lambda, b_out_gain,
           c_q_gain, c_k_gain, c_rel_bias, c_out_gain):
    b, s, d = x.shape
    n_ctx = ctx.shape[1]
    assert (b, s, d, n_ctx) == (2, SEQ, D_MODEL, CTX_LEN)

    cvec = jnp.zeros((8, d), F32).at[:b].set(c).at[b].set(c_ctx)
    mods = _modulation(cvec, w_mod, b_mod).reshape(DEPTH, 8, N_MOD, d)
    tables = _rope_tables(HEAD_DIM // 4) + _rope_tables(B_SUB_DIM // 4)

    x2 = x.reshape(b * s, d)
    g2 = ctx.reshape(b * n_ctx, d)
    tm_g = b * n_ctx

    mod_x = lambda i: i // (s // TM)
    mod_x_out = lambda i: i // (s // TM_OUT)
    mod_x_ffn = lambda i: i // (s // TM_FFN)
    mod_g = lambda i: b

    for l in range(DEPTH):
        with_ctx = l < DEPTH - 1
        lam_init = 0.8 - 0.6 * math.exp(-0.3 * l)
        gains = _column_gains(a_q_gain[l], a_k_gain[l], b_q_gain[l], b_k_gain[l],
                              c_q_gain[l], c_k_gain[l])
        a_gain = a_out_gain[l].reshape(1, A_WIDTH)
        b_gain = b_out_gain[l].reshape(1, HEAD_DIM)
        c_gain = c_out_gain[l].reshape(1, C_WIDTH)

        x2 = _ffn(x2, mods, mod_x_ffn, l, ffn1_w_gu, ffn1_w_down, 0, TM_FFN, TF)
        g2 = _ffn(g2, mods, mod_g, l, ffn1_w_gu, ffn1_w_down, 0, tm_g, TF_CTX)

        px = _inproj(x2, mods, mod_x, l, w_in, gains, tables, TM).reshape(b, s, IN_WIDTH)
        pg = _inproj(g2, mods, mod_g, l, w_in, gains, None, tm_g).reshape(b, n_ctx, IN_WIDTH)

        a_lat = _gqa_lat(px, pg, a_gain)
        b_lat = _diff_lat(px, pg, b_lambda[l], b_gain, lam_init)
        c_lat = _nbr(px, pg, _nbr_bias(c_rel_bias[l]), c_gain)

        x2 = _outproj(x2, mods, mod_x_out, l, a_lat.reshape(b * s, A_WIDTH),
                      b_lat.reshape(b * s, B_WIDTH), c_lat.reshape(b * s, C_WIDTH), w_out, TM_OUT)
        x2 = _ffn(x2, mods, mod_x_ffn, l, ffn2_w_gu, ffn2_w_down, 6, TM_FFN, TF)

        if with_ctx:
            a_ctx = _gqa_ctx(pg, a_gain, A_KV_HEADS, A_GROUP, AQ_H, AK_H, AV_H)
            b_ctx = _diff_ctx(pg, b_lambda[l], b_gain, lam_init)
            c_ctx = _gqa_ctx(pg, c_gain, C_HEADS, 1, CQ_H, CK_H, CV_H)
            g2 = _outproj(g2, mods, mod_g, l, a_ctx.reshape(tm_g, A_WIDTH),
                          b_ctx.reshape(tm_g, B_WIDTH), c_ctx.reshape(tm_g, C_WIDTH), w_out, tm_g)
            g2 = _ffn(g2, mods, mod_g, l, ffn2_w_gu, ffn2_w_down, 6, tm_g, TF_CTX)

    return x2.reshape(b, s, d)
```

```python
import functools
import math

import numpy as np
import jax
import jax.numpy as jnp
from jax import lax
from jax.experimental import pallas as pl
from jax.experimental.pallas import tpu as pltpu

F32 = jnp.float32
BF16 = jnp.bfloat16

D_MODEL = 2048
SEQ = 4096
DEPTH = 2
GRID_W = 64
GRID_H = SEQ // GRID_W
CTX_LEN = 256
HEAD_DIM = 128
N_MOD = 9
FFN_HIDDEN = 5632
RMS_EPS = 1e-6
ROPE_THETA = 10000.0

A_HEADS = 6
A_KV_HEADS = 2
A_GROUP = A_HEADS // A_KV_HEADS
B_HEADS = 4
C_HEADS = 6
B_SUB_DIM = HEAD_DIM // 2
A_WIDTH = A_HEADS * HEAD_DIM
A_KV_WIDTH = A_KV_HEADS * HEAD_DIM
B_WIDTH = B_HEADS * HEAD_DIM
C_WIDTH = C_HEADS * HEAD_DIM
MIX_WIDTH = A_WIDTH + B_WIDTH + C_WIDTH
IN_WIDTH = A_WIDTH + 2 * A_KV_WIDTH + 3 * B_WIDTH + 3 * C_WIDTH
NA_ROWS = 8
NA_COLS = 16

CQ_H = 0
CK_H = CQ_H + C_HEADS
CV_H = CK_H + C_HEADS
AQ_H = CV_H + C_HEADS
AK_H = AQ_H + A_HEADS
AV_H = AK_H + A_KV_HEADS
BQ_H = AV_H + A_KV_HEADS
BK_H = BQ_H + B_HEADS
BV_H = BK_H + B_HEADS

VMEM_LIMIT = 56 * 1024 * 1024
FFN_VMEM_LIMIT = 61 * 1024 * 1024
MASK_VALUE = -1e30
LOG2E = math.log2(math.e)

TM = 1024
TM_OUT = 2048
TM_FFN = 1024
TF = 512
TF_CTX = 512
TN = 256
TN_OUT = 512
IN_TILES_PER_STEP = 2
ROW_GROUPS = 4
EDGE_ROWS = 256
TQ_A = 256
TQ_B = 256
KV_CHUNK = 2048
NB_ROWS = 4
NB_Q = NB_ROWS * GRID_W
NB_KROWS = 12
NB_K = NB_KROWS * GRID_W

N_IN_TILES = IN_WIDTH // TN
SRC_SHIFT = (IN_WIDTH - 3 * C_WIDTH) // TN


def _silu(v):
    return v / (1.0 + jnp.exp(-v))


def _rms(v):
    return v * lax.rsqrt(jnp.mean(v * v, axis=-1, keepdims=True) + RMS_EPS)


def _head(h):
    return slice(h * HEAD_DIM, (h + 1) * HEAD_DIM)


def _params(*sem, vmem_limit=VMEM_LIMIT):
    return pltpu.CompilerParams(dimension_semantics=sem, vmem_limit_bytes=vmem_limit)


def _mod_kernel(c_ref, w_ref, b_ref, o_ref):
    s = _silu(c_ref[...]).astype(BF16)
    o_ref[...] = jnp.dot(s, w_ref[...].astype(BF16), preferred_element_type=F32) + b_ref[...]


def _modulation(cvec, w_mod, b_mod):
    depth, d, n = w_mod.shape
    tn = 2048
    return pl.pallas_call(
        _mod_kernel,
        grid=(depth, n // tn),
        in_specs=[
            pl.BlockSpec((8, d), lambda l, j: (0, 0)),
            pl.BlockSpec((None, d, tn), lambda l, j: (l, 0, j)),
            pl.BlockSpec((None, 1, tn), lambda l, j: (l, 0, j)),
        ],
        out_specs=pl.BlockSpec((None, 8, tn), lambda l, j: (l, 0, j)),
        out_shape=jax.ShapeDtypeStruct((depth, 8, n), F32),
        compiler_params=_params("parallel", "parallel"),
        name="modulation",
    )(cvec, w_mod, b_mod.reshape(depth, 1, n))


def _ffn_kernel(mod_row, nf, x_ref, mod_ref, wg_ref, wu_ref, wd_ref, o_ref, xn_ref):
    f = pl.program_id(1)
    tm = x_ref.shape[0]

    def step(first, last, groups):
        wg = wg_ref[...].astype(BF16)
        wu = wu_ref[...].astype(BF16)
        wd = wd_ref[...].astype(BF16)
        for r in range(groups):
            rows = slice(r * (tm // groups), (r + 1) * (tm // groups))
            if first:
                shift = mod_ref[mod_row:mod_row + 1, :]
                scale = mod_ref[mod_row + 1:mod_row + 2, :]
                xn = (_rms(x_ref[rows, :]) * (1.0 + scale) + shift).astype(BF16)
                xn_ref[rows, :] = xn
            else:
                xn = xn_ref[rows, :]
            g = jnp.dot(xn, wg, preferred_element_type=F32)
            u = jnp.dot(xn, wu, preferred_element_type=F32)
            a = (_silu(g) * u).astype(BF16)
            y = jnp.dot(a, wd, preferred_element_type=F32)
            if not first:
                y = o_ref[rows, :] + y
            if last:
                gate = mod_ref[mod_row + 2:mod_row + 3, :]
                y = x_ref[rows, :] + (0.5 * gate) * y
            o_ref[rows, :] = y

    edge_groups = tm // EDGE_ROWS
    pl.when(f == 0)(lambda: step(True, False, edge_groups))
    pl.when((f > 0) & (f < nf - 1))(lambda: step(False, False, 1))
    pl.when(f == nf - 1)(lambda: step(False, True, edge_groups))


def _ffn(x2, mods, mod_at, layer, w_gu, w_down, mod_row, tm, tf):
    m, d = x2.shape
    hidden = w_down.shape[1]
    nf = hidden // tf
    return pl.pallas_call(
        functools.partial(_ffn_kernel, mod_row, nf),
        grid=(m // tm, nf),
        in_specs=[
            pl.BlockSpec((tm, d), lambda i, f: (i, 0), pipeline_mode=pl.Buffered(1)),
            pl.BlockSpec((None, None, N_MOD, d), lambda i, f: (layer, mod_at(i), 0, 0)),
            pl.BlockSpec((None, d, tf), lambda i, f: (layer, 0, f)),
            pl.BlockSpec((None, d, tf), lambda i, f: (layer, 0, nf + f)),
            pl.BlockSpec((None, tf, d), lambda i, f: (layer, f, 0)),
        ],
        out_specs=pl.BlockSpec((tm, d), lambda i, f: (i, 0)),
        out_shape=jax.ShapeDtypeStruct((m, d), F32),
        scratch_shapes=[pltpu.VMEM((tm, d), BF16)],
        compiler_params=_params("parallel", "arbitrary", vmem_limit=FFN_VMEM_LIMIT),
        name="ffn",
    )(x2, mods, w_gu, w_gu, w_down)


def _norm_lanes(y, width):
    y2 = y * y
    if width == HEAD_DIM:
        ms = jnp.mean(y2, axis=-1, keepdims=True)
    else:
        lane = lax.broadcasted_iota(jnp.int32, y.shape, 1)
        low = lane < width
        s_lo = jnp.sum(jnp.where(low, y2, 0.0), axis=-1, keepdims=True)
        s_hi = jnp.sum(jnp.where(low, 0.0, y2), axis=-1, keepdims=True)
        ms = jnp.where(low, s_lo, s_hi) * (1.0 / width)
    return y * lax.rsqrt(ms + RMS_EPS)


def _tile_modes():
    per_head = (['c'] * (2 * C_HEADS) + ['p'] * C_HEADS + ['a'] * (A_HEADS + A_KV_HEADS)
                + ['p'] * A_KV_HEADS + ['b'] * (2 * B_HEADS) + ['p'] * B_HEADS)
    hpt = TN // HEAD_DIM
    tiles = [set(per_head[t * hpt:(t + 1) * hpt]) for t in range(N_IN_TILES)]
    assert all(len(t) == 1 for t in tiles)
    return [t.pop() for t in tiles]


def _inproj_kernel(rope, x_ref, mod_ref, *rest):
    w_refs = rest[:IN_TILES_PER_STEP]
    gain_ref = rest[IN_TILES_PER_STEP]
    rest = rest[IN_TILES_PER_STEP + 1:]
    if rope:
        ca_ref, sa_ref, cb_ref, sb_ref, o_ref, xn_ref = rest
    else:
        o_ref, xn_ref = rest
    j = pl.program_id(1)
    heads_per_tile = TN // HEAD_DIM
    group_rows = x_ref.shape[0] // ROW_GROUPS
    rope_tables = {'a': (ca_ref, sa_ref, HEAD_DIM // 4),
                   'b': (cb_ref, sb_ref, B_SUB_DIM // 4)} if rope else {}
    lane = lax.broadcasted_iota(jnp.int32, (group_rows, HEAD_DIM), 1)
    norm_width = {'a': HEAD_DIM, 'b': B_SUB_DIM, 'c': HEAD_DIM, 'p': 0}

    def emit(modes, first=False):
        ws = [w_ref[...].astype(BF16) for w_ref in w_refs]
        for r in range(ROW_GROUPS):
            rows = slice(r * group_rows, (r + 1) * group_rows)
            if first:
                shift = mod_ref[3:4, :]
                scale = mod_ref[4:5, :]
                xn = (_rms(x_ref[rows, :]) * (1.0 + scale) + shift).astype(BF16)
                xn_ref[rows, :] = xn
            else:
                xn = xn_ref[rows, :]
            for t, mode in enumerate(modes):
                y = jnp.dot(xn, ws[t], preferred_element_type=F32)
                for hh in range(heads_per_tile):
                    cols = _head(t * heads_per_tile + hh)
                    v = y[:, _head(hh)]
                    if norm_width[mode]:
                        v = _norm_lanes(v, norm_width[mode]) * gain_ref[:, cols]
                    if mode in rope_tables:
                        c_ref, s_ref, half = rope_tables[mode]
                        partner = jnp.where((lane & (2 * half - 1)) < half,
                                            pltpu.roll(v, HEAD_DIM - half, 1), pltpu.roll(v, half, 1))
                        v = v * c_ref[rows, :] + partner * s_ref[rows, :]
                    o_ref[rows, cols] = v.astype(o_ref.dtype)

    tile_modes = _tile_modes()
    steps = {}
    for step in range(N_IN_TILES // IN_TILES_PER_STEP):
        modes = tuple(tile_modes[step * IN_TILES_PER_STEP:(step + 1) * IN_TILES_PER_STEP])
        steps.setdefault(modes, []).append(step)

    pl.when(j == 0)(functools.partial(emit, tuple(tile_modes[:IN_TILES_PER_STEP]), first=True))
    for modes, where in steps.items():
        later = [t for t in where if t > 0]
        if later:
            cond = functools.reduce(lambda a, b: a | b, [j == t for t in later])
            pl.when(cond)(functools.partial(emit, modes))


def _inproj(x2, mods, mod_at, layer, w_in, gains, tables, tm):
    m, d = x2.shape
    rope = tables is not None
    tn = TN * IN_TILES_PER_STEP
    src = lambda t: lax.rem(t + SRC_SHIFT, N_IN_TILES)
    w_spec = lambda k: pl.BlockSpec((None, d, TN),
                                    lambda i, j: (layer, 0, src(j * IN_TILES_PER_STEP + k)))
    in_specs = [
        pl.BlockSpec((tm, d), lambda i, j: (i, 0), pipeline_mode=pl.Buffered(1)),
        pl.BlockSpec((None, None, N_MOD, d), lambda i, j: (layer, mod_at(i), 0, 0)),
        *[w_spec(k) for k in range(IN_TILES_PER_STEP)],
        pl.BlockSpec((1, tn), lambda i, j: (0, j)),
    ]
    args = [x2, mods] + [w_in] * IN_TILES_PER_STEP + [gains]
    if rope:
        pos_tiles = SEQ // tm
        for t in tables:
            in_specs.append(pl.BlockSpec((tm, HEAD_DIM), lambda i, j: (i % pos_tiles, 0)))
            args.append(t)
    return pl.pallas_call(
        functools.partial(_inproj_kernel, rope),
        grid=(m // tm, N_IN_TILES // IN_TILES_PER_STEP),
        in_specs=in_specs,
        out_specs=pl.BlockSpec((tm, tn), lambda i, j: (i, j)),
        out_shape=jax.ShapeDtypeStruct((m, IN_WIDTH), BF16),
        scratch_shapes=[pltpu.VMEM((tm, d), BF16)],
        compiler_params=_params("parallel", "arbitrary"),
        name="inproj",
    )(*args)


def _rope_tables(half):
    lane = np.arange(HEAD_DIM)
    freqs = ROPE_THETA ** (-jnp.asarray(lane % half, dtype=F32) / half)
    ang = jnp.arange(max(GRID_H, GRID_W), dtype=F32)[:, None] * freqs[None, :]
    first = jnp.asarray((lane % (2 * half)) < half)[None, :]
    is_col = jnp.asarray((lane % (4 * half)) >= 2 * half)[None, None, :]

    def expand(table):
        by_row = jnp.broadcast_to(table[:GRID_H, None, :], (GRID_H, GRID_W, HEAD_DIM))
        by_col = jnp.broadcast_to(table[None, :GRID_W, :], (GRID_H, GRID_W, HEAD_DIM))
        return jnp.where(is_col, by_col, by_row).reshape(SEQ, HEAD_DIM)

    sin = jnp.sin(ang)
    return expand(jnp.cos(ang)), expand(jnp.where(first, -sin, sin))


_NT = (((1,), (1,)), ((), ()))


def _softmax_step(s, v, state):
    m_cur = jnp.max(s, axis=-1, keepdims=True)
    if state is None:
        m_new = m_cur
        p = jnp.exp2(s - m_new)
        l_new = jnp.sum(p, axis=-1, keepdims=True)
        acc = jnp.dot(p.astype(BF16), v, preferred_element_type=F32)
    else:
        m_old, l_old, acc_old = state
        m_new = jnp.maximum(m_old, m_cur)
        alpha = jnp.exp2(m_old - m_new)
        p = jnp.exp2(s - m_new)
        l_new = alpha * l_old + jnp.sum(p, axis=-1, keepdims=True)
        acc = alpha * acc_old + jnp.dot(p.astype(BF16), v, preferred_element_type=F32)
    return m_new, l_new, acc


def _attend(qs, kc, vc, kl_ref, vl_ref):
    def scores(k):
        return lax.dot_general(qs, k, _NT, preferred_element_type=F32)

    state = _softmax_step(scores(kc), vc, None)
    if kl_ref is not None:
        for start in range(0, kl_ref.shape[0], KV_CHUNK):
            state = _softmax_step(scores(kl_ref[start:start + KV_CHUNK, :]),
                                  vl_ref[start:start + KV_CHUNK, :], state)
    _, l, acc = state
    return acc / l


def _store_normed(heads, gain_ref, o_ref):
    ms = sum(jnp.sum(o * o, axis=-1, keepdims=True) for o in heads) * (1.0 / (len(heads) * HEAD_DIM))
    r = lax.rsqrt(ms + RMS_EPS)
    for h, o in enumerate(heads):
        o_ref[:, _head(h)] = (o * r * gain_ref[:, _head(h)]).astype(o_ref.dtype)


def _gqa_heads(q, n_kv, group, pg_ref, k_h, v_h, kl_refs, vl_refs):
    tq = q.shape[0]
    heads = []
    for g in range(n_kv):
        qs = jnp.concatenate([q[:, _head(g * group + i)] for i in range(group)], axis=0)
        o = _attend(qs, pg_ref[:, _head(k_h + g)], pg_ref[:, _head(v_h + g)],
                    kl_refs[g] if kl_refs else None, vl_refs[g] if vl_refs else None)
        heads += [o[i * tq:(i + 1) * tq] for i in range(group)]
    return heads


def _gqa_lat_kernel(q_ref, pg_ref, *rest):
    kl_refs = rest[:A_KV_HEADS]
    vl_refs = rest[A_KV_HEADS:2 * A_KV_HEADS]
    gain_ref, o_ref = rest[2 * A_KV_HEADS:]
    heads = _gqa_heads(q_ref[...], A_KV_HEADS, A_GROUP, pg_ref, AK_H, AV_H, kl_refs, vl_refs)
    _store_normed(heads, gain_ref, o_ref)


def _gqa_ctx_kernel(n_kv, group, q_h, k_h, v_h, pg_ref, gain_ref, o_ref):
    q = pg_ref[:, q_h * HEAD_DIM:(q_h + n_kv * group) * HEAD_DIM]
    _store_normed(_gqa_heads(q, n_kv, group, pg_ref, k_h, v_h, None, None), gain_ref, o_ref)


def _kv_specs(k_h, v_h, n_heads):
    heads = [k_h + g for g in range(n_heads)] + [v_h + g for g in range(n_heads)]
    return [pl.BlockSpec((None, SEQ, HEAD_DIM), lambda bi, qi, h=h: (bi, 0, h)) for h in heads]


def _gqa_lat(px, pg, gain):
    b = px.shape[0]
    return pl.pallas_call(
        _gqa_lat_kernel,
        grid=(b, SEQ // TQ_A),
        in_specs=[
            pl.BlockSpec((None, TQ_A, A_WIDTH), lambda bi, qi: (bi, qi, AQ_H * HEAD_DIM // A_WIDTH)),
            pl.BlockSpec((None, CTX_LEN, IN_WIDTH), lambda bi, qi: (bi, 0, 0)),
            *_kv_specs(AK_H, AV_H, A_KV_HEADS),
            pl.BlockSpec((1, A_WIDTH), lambda bi, qi: (0, 0)),
        ],
        out_specs=pl.BlockSpec((None, TQ_A, A_WIDTH), lambda bi, qi: (bi, qi, 0)),
        out_shape=jax.ShapeDtypeStruct((b, SEQ, A_WIDTH), BF16),
        compiler_params=_params("parallel", "arbitrary"),
        name="gqa_attention",
    )(px, pg, *([px] * (2 * A_KV_HEADS)), gain)


def _gqa_ctx(pg, gain, n_kv, group, q_h, k_h, v_h):
    b = pg.shape[0]
    width = n_kv * group * HEAD_DIM
    return pl.pallas_call(
        functools.partial(_gqa_ctx_kernel, n_kv, group, q_h, k_h, v_h),
        grid=(b,),
        in_specs=[
            pl.BlockSpec((None, CTX_LEN, IN_WIDTH), lambda bi: (bi, 0, 0)),
            pl.BlockSpec((1, width), lambda bi: (0, 0)),
        ],
        out_specs=pl.BlockSpec((None, CTX_LEN, width), lambda bi: (bi, 0, 0)),
        out_shape=jax.ShapeDtypeStruct((b, CTX_LEN, width), BF16),
        compiler_params=_params("parallel"),
        name="ctx_attention",
    )(pg, gain)


def _diff_heads(q, lam_init, lam_ref, gain_ref, pg_ref, kl_refs, vl_refs, o_ref):
    tq = q.shape[0]
    lv = lam_ref[...]
    lam = (jnp.exp(jnp.sum(lv[0:1] * lv[1:2], axis=-1, keepdims=True))
           - jnp.exp(jnp.sum(lv[2:3] * lv[3:4], axis=-1, keepdims=True)) + lam_init)
    low = lax.broadcasted_iota(jnp.int32, (tq, HEAD_DIM), 1) < B_SUB_DIM
    zero = jnp.zeros((tq, HEAD_DIM), q.dtype)
    for h in range(B_HEADS):
        qh = q[:, _head(h)]
        qs = jnp.concatenate([jnp.where(low, qh, zero), jnp.where(low, zero, qh)], axis=0)
        o = _attend(qs, pg_ref[:, _head(BK_H + h)], pg_ref[:, _head(BV_H + h)],
                    kl_refs[h] if kl_refs else None, vl_refs[h] if vl_refs else None)
        o = _rms(o[:tq] - lam * o[tq:]) * gain_ref[...] * (1.0 - lam_init)
        o_ref[:, _head(h)] = o.astype(o_ref.dtype)


def _diff_lat_kernel(lam_init, q_ref, lam_ref, gain_ref, pg_ref, *rest):
    kl_refs = rest[:B_HEADS]
    vl_refs = rest[B_HEADS:2 * B_HEADS]
    o_ref = rest[2 * B_HEADS]
    _diff_heads(q_ref[...], lam_init, lam_ref, gain_ref, pg_ref, kl_refs, vl_refs, o_ref)


def _diff_ctx_kernel(lam_init, lam_ref, gain_ref, pg_ref, o_ref):
    q = pg_ref[:, BQ_H * HEAD_DIM:BK_H * HEAD_DIM]
    _diff_heads(q, lam_init, lam_ref, gain_ref, pg_ref, None, None, o_ref)


def _diff_lat(px, pg, lam_vecs, gain, lam_init):
    b = px.shape[0]
    return pl.pallas_call(
        functools.partial(_diff_lat_kernel, lam_init),
        grid=(b, SEQ // TQ_B),
        in_specs=[
            pl.BlockSpec((None, TQ_B, B_WIDTH), lambda bi, qi: (bi, qi, BQ_H * HEAD_DIM // B_WIDTH)),
            pl.BlockSpec((4, B_SUB_DIM), lambda bi, qi: (0, 0)),
            pl.BlockSpec((1, HEAD_DIM), lambda bi, qi: (0, 0)),
            pl.BlockSpec((None, CTX_LEN, IN_WIDTH), lambda bi, qi: (bi, 0, 0)),
            *_kv_specs(BK_H, BV_H, B_HEADS),
        ],
        out_specs=pl.BlockSpec((None, TQ_B, B_WIDTH), lambda bi, qi: (bi, qi, 0)),
        out_shape=jax.ShapeDtypeStruct((b, SEQ, B_WIDTH), BF16),
        compiler_params=_params("parallel", "arbitrary"),
        name="diff_attention",
    )(px, lam_vecs, gain, pg, *([px] * (2 * B_HEADS)))


def _diff_ctx(pg, lam_vecs, gain, lam_init):
    b = pg.shape[0]
    return pl.pallas_call(
        functools.partial(_diff_ctx_kernel, lam_init),
        grid=(b,),
        in_specs=[
            pl.BlockSpec((4, B_SUB_DIM), lambda bi: (0, 0)),
            pl.BlockSpec((1, HEAD_DIM), lambda bi: (0, 0)),
            pl.BlockSpec((None, CTX_LEN, IN_WIDTH), lambda bi: (bi, 0, 0)),
        ],
        out_specs=pl.BlockSpec((None, CTX_LEN, B_WIDTH), lambda bi: (bi, 0, 0)),
        out_shape=jax.ShapeDtypeStruct((b, CTX_LEN, B_WIDTH), BF16),
        compiler_params=_params("parallel"),
        name="diff_ctx_attention",
    )(lam_vecs, gain, pg)


def _nbr_kernel(q_ref, k_ref, v_ref, pg_ref, bias_ref, gain_ref, o_ref):
    j = pl.program_id(1)
    key_row0 = jnp.clip(NB_ROWS * j - NA_ROWS // 2, 0, GRID_H - NB_KROWS)
    start = pl.multiple_of(key_row0 * GRID_W, NB_ROWS * GRID_W)
    heads = []
    for h in range(C_HEADS):
        q = q_ref[:, _head(h)]
        kw = k_ref[pl.ds(start, NB_K), _head(h)]
        vw = v_ref[pl.ds(start, NB_K), _head(h)]
        kc = pg_ref[:, _head(CK_H + h)]
        vc = pg_ref[:, _head(CV_H + h)]
        s_w = lax.dot_general(q, kw, _NT, preferred_element_type=F32) + bias_ref[h]
        s_c = lax.dot_general(q, kc, _NT, preferred_element_type=F32)
        m = jnp.maximum(jnp.max(s_w, axis=-1, keepdims=True), jnp.max(s_c, axis=-1, keepdims=True))
        p_w = jnp.exp2(s_w - m)
        p_c = jnp.exp2(s_c - m)
        l = jnp.sum(p_w, axis=-1, keepdims=True) + jnp.sum(p_c, axis=-1, keepdims=True)
        o = (jnp.dot(p_w.astype(BF16), vw, preferred_element_type=F32)
             + jnp.dot(p_c.astype(BF16), vc, preferred_element_type=F32))
        heads.append(o / l)
    _store_normed(heads, gain_ref, o_ref)


def _nbr_bias(rel_bias):
    n_blocks = GRID_H // NB_ROWS
    col = np.arange(GRID_W)
    cs = np.clip(col - NA_COLS // 2, 0, GRID_W - NA_COLS)
    col_ok = (col[None, :] >= cs[:, None]) & (col[None, :] < cs[:, None] + NA_COLS)
    dc = col[None, :] - col[:, None] + NA_COLS - 1
    onehot = (dc[None] == np.arange(2 * NA_COLS - 1)[:, None, None]) & col_ok[None]
    per_row = jnp.einsum('hrt,tck->hcrk', rel_bias.astype(F32) * LOG2E, jnp.asarray(onehot, F32),
                         precision=lax.Precision.HIGHEST)
    per_row = jnp.where(col_ok[None, :, None, :], per_row, MASK_VALUE)
    padded = jnp.pad(per_row, ((0, 0), (0, 0), (NB_KROWS, NB_KROWS), (0, 0)),
                     constant_values=MASK_VALUE)
    a = np.arange(NB_KROWS)
    classes = []
    for j in (0, 1, n_blocks - 1):
        key_row0 = int(np.clip(NB_ROWS * j - NA_ROWS // 2, 0, GRID_H - NB_KROWS))
        strips = []
        for i in range(NB_ROWS):
            r = NB_ROWS * j + i
            rs = int(np.clip(r - NA_ROWS // 2, 0, GRID_H - NA_ROWS))
            first = key_row0 - r + NA_ROWS - 1
            in_window = (key_row0 + a >= rs) & (key_row0 + a < rs + NA_ROWS)
            assert np.all((first + a >= 0)[in_window]) and np.all((first + a < 2 * NA_ROWS - 1)[in_window])
            run = padded[:, :, first + NB_KROWS:first + 2 * NB_KROWS, :]
            run = jnp.where(in_window[None, None, :, None], run, MASK_VALUE)
            strips.append(run.reshape(C_HEADS, GRID_W, NB_K))
        classes.append(jnp.concatenate(strips, axis=1))
    return jnp.stack(classes)


def _nbr(px, pg, bias_cls, gain):
    b = px.shape[0]
    n_blocks = GRID_H // NB_ROWS
    blk = lambda h: h * HEAD_DIM // C_WIDTH

    def cls(j):
        return jnp.where(j == 0, 0, jnp.where(j == n_blocks - 1, 2, 1))

    return pl.pallas_call(
        _nbr_kernel,
        grid=(b, n_blocks),
        in_specs=[
            pl.BlockSpec((None, NB_Q, C_WIDTH), lambda bi, j: (bi, j, blk(CQ_H))),
            pl.BlockSpec((None, SEQ, C_WIDTH), lambda bi, j: (bi, 0, blk(CK_H)),
                         pipeline_mode=pl.Buffered(1)),
            pl.BlockSpec((None, SEQ, C_WIDTH), lambda bi, j: (bi, 0, blk(CV_H)),
                         pipeline_mode=pl.Buffered(1)),
            pl.BlockSpec((None, CTX_LEN, IN_WIDTH), lambda bi, j: (bi, 0, 0)),
            pl.BlockSpec((None, C_HEADS, NB_Q, NB_K), lambda bi, j: (cls(j), 0, 0, 0)),
            pl.BlockSpec((1, C_WIDTH), lambda bi, j: (0, 0)),
        ],
        out_specs=pl.BlockSpec((None, NB_Q, C_WIDTH), lambda bi, j: (bi, j, 0)),
        out_shape=jax.ShapeDtypeStruct((b, SEQ, C_WIDTH), BF16),
        compiler_params=_params("parallel", "arbitrary"),
        name="nbr_attention",
    )(px, px, px, pg, bias_cls, gain)


def _outproj_kernel(x_ref, mod_ref, a_ref, b_ref, c_ref, w_ref, o_ref):
    y = (jnp.dot(a_ref[...], w_ref[:A_WIDTH, :].astype(BF16), preferred_element_type=F32)
         + jnp.dot(b_ref[...], w_ref[A_WIDTH:A_WIDTH + B_WIDTH, :].astype(BF16),
                   preferred_element_type=F32)
         + jnp.dot(c_ref[...], w_ref[A_WIDTH + B_WIDTH:, :].astype(BF16),
                   preferred_element_type=F32))
    o_ref[...] = x_ref[...] + mod_ref[5:6, :] * y


def _outproj(x2, mods, mod_at, layer, a2, b2, c2, w_out, tm):
    m, d = x2.shape
    return pl.pallas_call(
        _outproj_kernel,
        grid=(m // tm, d // TN_OUT),
        in_specs=[
            pl.BlockSpec((tm, TN_OUT), lambda i, j: (i, j)),
            pl.BlockSpec((None, None, N_MOD, TN_OUT), lambda i, j: (layer, mod_at(i), 0, j)),
            pl.BlockSpec((tm, A_WIDTH), lambda i, j: (i, 0)),
            pl.BlockSpec((tm, B_WIDTH), lambda i, j: (i, 0)),
            pl.BlockSpec((tm, C_WIDTH), lambda i, j: (i, 0)),
            pl.BlockSpec((None, MIX_WIDTH, TN_OUT), lambda i, j: (layer, 0, j)),
        ],
        out_specs=pl.BlockSpec((tm, TN_OUT), lambda i, j: (i, j)),
        out_shape=jax.ShapeDtypeStruct((m, d), F32),
        compiler_params=_params("parallel", "arbitrary"),
        name="outproj",
    )(x2, mods, a2, b2, c2, w_out)


def _column_gains(a_q, a_k, b_q, b_k, c_q, c_k):
    a_scale = LOG2E * HEAD_DIM ** -0.5
    b_scale = LOG2E * B_SUB_DIM ** -0.5
    parts = (
        jnp.tile(c_q * a_scale, C_HEADS), jnp.tile(c_k, C_HEADS), jnp.ones((C_WIDTH,), F32),
        jnp.tile(a_q * a_scale, A_HEADS), jnp.tile(a_k, A_KV_HEADS), jnp.ones((A_KV_WIDTH,), F32),
        jnp.tile(b_q * b_scale, 2 * B_HEADS), jnp.tile(b_k, 2 * B_HEADS), jnp.ones((B_WIDTH,), F32),
    )
    return jnp.concatenate(parts).reshape(1, IN_WIDTH)


def kernel(x, c, ctx, c_ctx, w_mod, b_mod, ffn1_w_gu, ffn1_w_down, ffn2_w_gu, ffn2_w_down,
           w_in, w_out, a_q_gain, a_k_gain, a_out_gain, b_q_gain, b_k_gain, b_lambda, b_out_gain,
           c_q_gain, c_k_gain, c_rel_bias, c_out_gain):
    b, s, d = x.shape
    n_ctx = ctx.shape[1]
    assert (b, s, d, n_ctx) == (2, SEQ, D_MODEL, CTX_LEN)

    cvec = jnp.zeros((8, d), F32).at[:b].set(c).at[b].set(c_ctx)
    mods = _modulation(cvec, w_mod, b_mod).reshape(DEPTH, 8, N_MOD, d)
    tables = _rope_tables(HEAD_DIM // 4) + _rope_tables(B_SUB_DIM // 4)

    x2 = x.reshape(b * s, d)
    g2 = ctx.reshape(b * n_ctx, d)
    tm_g = b * n_ctx

    mod_x = lambda i: i // (s // TM)
    mod_x_out = lambda i: i // (s // TM_OUT)
    mod_x_ffn = lambda i: i // (s // TM_FFN)
    mod_g = lambda i: b

    for l in range(DEPTH):
        with_ctx = l < DEPTH - 1
        lam_init = 0.8 - 0.6 * math.exp(-0.3 * l)
        gains = _column_gains(a_q_gain[l], a_k_gain[l], b_q_gain[l], b_k_gain[l],
                              c_q_gain[l], c_k_gain[l])
        a_gain = a_out_gain[l].reshape(1, A_WIDTH)
        b_gain = b_out_gain[l].reshape(1, HEAD_DIM)
        c_gain = c_out_gain[l].reshape(1, C_WIDTH)

        x2 = _ffn(x2, mods, mod_x_ffn, l, ffn1_w_gu, ffn1_w_down, 0, TM_FFN, TF)
        g2 = _ffn(g2, mods, mod_g, l, ffn1_w_gu, ffn1_w_down, 0, tm_g, TF_CTX)

        px = _inproj(x2, mods, mod_x, l, w_in, gains, tables, TM).reshape(b, s, IN_WIDTH)
        pg = _inproj(g2, mods, mod_g, l, w_in, gains, None, tm_g).reshape(b, n_ctx, IN_WIDTH)

        a_lat = _gqa_lat(px, pg, a_gain)
        b_lat = _diff_lat(px, pg, b_lambda[l], b_gain, lam_init)
        c_lat = _nbr(px, pg, _nbr_bias(c_rel_bias[l]), c_gain)

        x2 = _outproj(x2, mods, mod_x_out, l, a_lat.reshape(b * s, A_WIDTH),
                      b_lat.reshape(b * s, B_WIDTH), c_lat.reshape(b * s, C_WIDTH), w_out, TM_OUT)
        x2 = _ffn(x2, mods, mod_x_ffn, l, ffn2_w_gu, ffn2_w_down, 6, TM_FFN, TF)

        if with_ctx:
            a_ctx = _gqa_ctx(pg, a_gain, A_KV_HEADS, A_GROUP, AQ_H, AK_H, AV_H)
            b_ctx = _diff_ctx(pg, b_lambda[l], b_gain, lam_init)
            c_ctx = _gqa_ctx(pg, c_gain, C_HEADS, 1, CQ_H, CK_H, CV_H)
            g2 = _outproj(g2, mods, mod_g, l, a_ctx.reshape(tm_g, A_WIDTH),
                          b_ctx.reshape(tm_g, B_WIDTH), c_ctx.reshape(tm_g, C_WIDTH), w_out, tm_g)
            g2 = _ffn(g2, mods, mod_g, l, ffn2_w_gu, ffn2_w_down, 6, tm_g, TF_CTX)

    return x2.reshape(b, s, d)
```

```python
import functools
import math

import numpy as np
import jax
import jax.numpy as jnp
from jax import lax
from jax.experimental import pallas as pl
from jax.experimental.pallas import tpu as pltpu

F32 = jnp.float32
BF16 = jnp.bfloat16

D_MODEL = 2048
SEQ = 4096
DEPTH = 2
GRID_W = 64
GRID_H = SEQ // GRID_W
CTX_LEN = 256
HEAD_DIM = 128
N_MOD = 9
FFN_HIDDEN = 5632
RMS_EPS = 1e-6
ROPE_THETA = 10000.0

A_HEADS = 6
A_KV_HEADS = 2
A_GROUP = A_HEADS // A_KV_HEADS
B_HEADS = 4
C_HEADS = 6
B_SUB_DIM = HEAD_DIM // 2
A_WIDTH = A_HEADS * HEAD_DIM
A_KV_WIDTH = A_KV_HEADS * HEAD_DIM
B_WIDTH = B_HEADS * HEAD_DIM
C_WIDTH = C_HEADS * HEAD_DIM
MIX_WIDTH = A_WIDTH + B_WIDTH + C_WIDTH
IN_WIDTH = A_WIDTH + 2 * A_KV_WIDTH + 3 * B_WIDTH + 3 * C_WIDTH
NA_ROWS = 8
NA_COLS = 16

CQ_H = 0
CK_H = CQ_H + C_HEADS
CV_H = CK_H + C_HEADS
AQ_H = CV_H + C_HEADS
AK_H = AQ_H + A_HEADS
AV_H = AK_H + A_KV_HEADS
BQ_H = AV_H + A_KV_HEADS
BK_H = BQ_H + B_HEADS
BV_H = BK_H + B_HEADS

VMEM_LIMIT = 56 * 1024 * 1024
FFN_VMEM_LIMIT = 61 * 1024 * 1024
MASK_VALUE = -1e30
LOG2E = math.log2(math.e)

TM = 1024
TM_OUT = 2048
TM_FFN = 1024
TF = 512
TF_CTX = 512
TN = 256
TN_OUT = 512
IN_TILES_PER_STEP = 2
ROW_GROUPS = 4
EDGE_ROWS = 256
TQ_A = 256
TQ_B = 256
KV_CHUNK = 2048
NB_ROWS = 4
NB_Q = NB_ROWS * GRID_W
NB_KROWS = 12
NB_K = NB_KROWS * GRID_W

N_IN_TILES = IN_WIDTH // TN
SRC_SHIFT = (IN_WIDTH - 3 * C_WIDTH) // TN


def _silu(v):
    return v / (1.0 + jnp.exp(-v))


def _rms(v):
    return v * lax.rsqrt(jnp.mean(v * v, axis=-1, keepdims=True) + RMS_EPS)


def _head(h):
    return slice(h * HEAD_DIM, (h + 1) * HEAD_DIM)


def _params(*sem, vmem_limit=VMEM_LIMIT):
    return pltpu.CompilerParams(dimension_semantics=sem, vmem_limit_bytes=vmem_limit)


def _mod_kernel(c_ref, w_ref, b_ref, o_ref):
    s = _silu(c_ref[...]).astype(BF16)
    o_ref[...] = jnp.dot(s, w_ref[...].astype(BF16), preferred_element_type=F32) + b_ref[...]


def _modulation(cvec, w_mod, b_mod):
    depth, d, n = w_mod.shape
    tn = 2048
    return pl.pallas_call(
        _mod_kernel,
        grid=(depth, n // tn),
        in_specs=[
            pl.BlockSpec((8, d), lambda l, j: (0, 0)),
            pl.BlockSpec((None, d, tn), lambda l, j: (l, 0, j)),
            pl.BlockSpec((None, 1, tn), lambda l, j: (l, 0, j)),
        ],
        out_specs=pl.BlockSpec((None, 8, tn), lambda l, j: (l, 0, j)),
        out_shape=jax.ShapeDtypeStruct((depth, 8, n), F32),
        compiler_params=_params("parallel", "parallel"),
        name="modulation",
    )(cvec, w_mod, b_mod.reshape(depth, 1, n))


def _ffn_kernel(mod_row, nf, x_ref, mod_ref, wg_ref, wu_ref, wd_ref, o_ref, xn_ref):
    f = pl.program_id(1)
    tm = x_ref.shape[0]

    def step(first, last, groups):
        wg = wg_ref[...].astype(BF16)
        wu = wu_ref[...].astype(BF16)
        wd = wd_ref[...].astype(BF16)
        for r in range(groups):
            rows = slice(r * (tm // groups), (r + 1) * (tm // groups))
            if first:
                shift = mod_ref[mod_row:mod_row + 1, :]
                scale = mod_ref[mod_row + 1:mod_row + 2, :]
                xn = (_rms(x_ref[rows, :]) * (1.0 + scale) + shift).astype(BF16)
                xn_ref[rows, :] = xn
            else:
                xn = xn_ref[rows, :]
            g = jnp.dot(xn, wg, preferred_element_type=F32)
            u = jnp.dot(xn, wu, preferred_element_type=F32)
            a = (_silu(g) * u).astype(BF16)
            y = jnp.dot(a, wd, preferred_element_type=F32)
            if not first:
                y = o_ref[rows, :] + y
            if last:
                gate = mod_ref[mod_row + 2:mod_row + 3, :]
                y = x_ref[rows, :] + (0.5 * gate) * y
            o_ref[rows, :] = y

    edge_groups = tm // EDGE_ROWS
    pl.when(f == 0)(lambda: step(True, False, edge_groups))
    pl.when((f > 0) & (f < nf - 1))(lambda: step(False, False, 1))
    pl.when(f == nf - 1)(lambda: step(False, True, edge_groups))


def _ffn(x2, mods, mod_at, layer, w_gu, w_down, mod_row, tm, tf):
    m, d = x2.shape
    hidden = w_down.shape[1]
    nf = hidden // tf
    return pl.pallas_call(
        functools.partial(_ffn_kernel, mod_row, nf),
        grid=(m // tm, nf),
        in_specs=[
            pl.BlockSpec((tm, d), lambda i, f: (i, 0), pipeline_mode=pl.Buffered(1)),
            pl.BlockSpec((None, None, N_MOD, d), lambda i, f: (layer, mod_at(i), 0, 0)),
            pl.BlockSpec((None, d, tf), lambda i, f: (layer, 0, f)),
            pl.BlockSpec((None, d, tf), lambda i, f: (layer, 0, nf + f)),
            pl.BlockSpec((None, tf, d), lambda i, f: (layer, f, 0)),
        ],
        out_specs=pl.BlockSpec((tm, d), lambda i, f: (i, 0)),
        out_shape=jax.ShapeDtypeStruct((m, d), F32),
        scratch_shapes=[pltpu.VMEM((tm, d), BF16)],
        compiler_params=_params("parallel", "arbitrary", vmem_limit=FFN_VMEM_LIMIT),
        name="ffn",
    )(x2, mods, w_gu, w_gu, w_down)


def _norm_lanes(y, width):
    y2 = y * y
    if width == HEAD_DIM:
        ms = jnp.mean(y2, axis=-1, keepdims=True)
    else:
        lane = lax.broadcasted_iota(jnp.int32, y.shape, 1)
        low = lane < width
        s_lo = jnp.sum(jnp.where(low, y2, 0.0), axis=-1, keepdims=True)
        s_hi = jnp.sum(jnp.where(low, 0.0, y2), axis=-1, keepdims=True)
        ms = jnp.where(low, s_lo, s_hi) * (1.0 / width)
    return y * lax.rsqrt(ms + RMS_EPS)


def _tile_modes():
    per_head = (['c'] * (2 * C_HEADS) + ['p'] * C_HEADS + ['a'] * (A_HEADS + A_KV_HEADS)
                + ['p'] * A_KV_HEADS + ['b'] * (2 * B_HEADS) + ['p'] * B_HEADS)
    hpt = TN // HEAD_DIM
    tiles = [set(per_head[t * hpt:(t + 1) * hpt]) for t in range(N_IN_TILES)]
    assert all(len(t) == 1 for t in tiles)
    return [t.pop() for t in tiles]


def _inproj_kernel(rope, x_ref, mod_ref, *rest):
    w_refs = rest[:IN_TILES_PER_STEP]
    gain_ref = rest[IN_TILES_PER_STEP]
    rest = rest[IN_TILES_PER_STEP + 1:]
    if rope:
        ca_ref, sa_ref, cb_ref, sb_ref, o_ref, xn_ref = rest
    else:
        o_ref, xn_ref = rest
    j = pl.program_id(1)
    heads_per_tile = TN // HEAD_DIM
    group_rows = x_ref.shape[0] // ROW_GROUPS
    rope_tables = {'a': (ca_ref, sa_ref, HEAD_DIM // 4),
                   'b': (cb_ref, sb_ref, B_SUB_DIM // 4)} if rope else {}
    lane = lax.broadcasted_iota(jnp.int32, (group_rows, HEAD_DIM), 1)
    norm_width = {'a': HEAD_DIM, 'b': B_SUB_DIM, 'c': HEAD_DIM, 'p': 0}

    def emit(modes, first=False):
        ws = [w_ref[...].astype(BF16) for w_ref in w_refs]
        for r in range(ROW_GROUPS):
            rows = slice(r * group_rows, (r + 1) * group_rows)
            if first:
                shift = mod_ref[3:4, :]
                scale = mod_ref[4:5, :]
                xn = (_rms(x_ref[rows, :]) * (1.0 + scale) + shift).astype(BF16)
                xn_ref[rows, :] = xn
            else:
                xn = xn_ref[rows, :]
            for t, mode in enumerate(modes):
                y = jnp.dot(xn, ws[t], preferred_element_type=F32)
                for hh in range(heads_per_tile):
                    cols = _head(t * heads_per_tile + hh)
                    v = y[:, _head(hh)]
                    if norm_width[mode]:
                        v = _norm_lanes(v, norm_width[mode]) * gain_ref[:, cols]
                    if mode in rope_tables:
                        c_ref, s_ref, half = rope_tables[mode]
                        partner = jnp.where((lane & (2 * half - 1)) < half,
                                            pltpu.roll(v, HEAD_DIM - half, 1), pltpu.roll(v, half, 1))
                        v = v * c_ref[rows, :] + partner * s_ref[rows, :]
                    o_ref[rows, cols] = v.astype(o_ref.dtype)

    tile_modes = _tile_modes()
    steps = {}
    for step in range(N_IN_TILES // IN_TILES_PER_STEP):
        modes = tuple(tile_modes[step * IN_TILES_PER_STEP:(step + 1) * IN_TILES_PER_STEP])
        steps.setdefault(modes, []).append(step)

    pl.when(j == 0)(functools.partial(emit, tuple(tile_modes[:IN_TILES_PER_STEP]), first=True))
    for modes, where in steps.items():
        later = [t for t in where if t > 0]
        if later:
            cond = functools.reduce(lambda a, b: a | b, [j == t for t in later])
            pl.when(cond)(functools.partial(emit, modes))


def _inproj(x2, mods, mod_at, layer, w_in, gains, tables, tm):
    m, d = x2.shape
    rope = tables is not None
    tn = TN * IN_TILES_PER_STEP
    src = lambda t: lax.rem(t + SRC_SHIFT, N_IN_TILES)
    w_spec = lambda k: pl.BlockSpec((None, d, TN),
                                    lambda i, j: (layer, 0, src(j * IN_TILES_PER_STEP + k)))
    in_specs = [
        pl.BlockSpec((tm, d), lambda i, j: (i, 0), pipeline_mode=pl.Buffered(1)),
        pl.BlockSpec((None, None, N_MOD, d), lambda i, j: (layer, mod_at(i), 0, 0)),
        *[w_spec(k) for k in range(IN_TILES_PER_STEP)],
        pl.BlockSpec((1, tn), lambda i, j: (0, j)),
    ]
    args = [x2, mods] + [w_in] * IN_TILES_PER_STEP + [gains]
    if rope:
        pos_tiles = SEQ // tm
        for t in tables:
            in_specs.append(pl.BlockSpec((tm, HEAD_DIM), lambda i, j: (i % pos_tiles, 0)))
            args.append(t)
    return pl.pallas_call(
        functools.partial(_inproj_kernel, rope),
        grid=(m // tm, N_IN_TILES // IN_TILES_PER_STEP),
        in_specs=in_specs,
        out_specs=pl.BlockSpec((tm, tn), lambda i, j: (i, j)),
        out_shape=jax.ShapeDtypeStruct((m, IN_WIDTH), BF16),
        scratch_shapes=[pltpu.VMEM((tm, d), BF16)],
        compiler_params=_params("parallel", "arbitrary"),
        name="inproj",
    )(*args)


def _rope_tables(half):
    t = jnp.arange(SEQ, dtype=jnp.int32)
    row = (t // GRID_W).astype(F32)[:, None]
    col = (t % GRID_W).astype(F32)[:, None]
    lane = np.arange(HEAD_DIM)
    freqs = ROPE_THETA ** (-jnp.asarray(lane % half, dtype=F32) / half)
    is_col = jnp.asarray((lane % (4 * half)) >= 2 * half)[None, :]
    ang = jnp.where(is_col, col, row) * freqs[None, :]
    first = jnp.asarray((lane % (2 * half)) < half)[None, :]
    cos = jnp.cos(ang)
    sin = jnp.sin(ang)
    return cos, jnp.where(first, -sin, sin)


_NT = (((1,), (1,)), ((), ()))


def _softmax_step(s, v, state):
    m_cur = jnp.max(s, axis=-1, keepdims=True)
    if state is None:
        m_new = m_cur
        p = jnp.exp2(s - m_new)
        l_new = jnp.sum(p, axis=-1, keepdims=True)
        acc = jnp.dot(p.astype(BF16), v, preferred_element_type=F32)
    else:
        m_old, l_old, acc_old = state
        m_new = jnp.maximum(m_old, m_cur)
        alpha = jnp.exp2(m_old - m_new)
        p = jnp.exp2(s - m_new)
        l_new = alpha * l_old + jnp.sum(p, axis=-1, keepdims=True)
        acc = alpha * acc_old + jnp.dot(p.astype(BF16), v, preferred_element_type=F32)
    return m_new, l_new, acc


def _attend(qs, kc, vc, kl_ref, vl_ref):
    def scores(k):
        return lax.dot_general(qs, k, _NT, preferred_element_type=F32)

    state = _softmax_step(scores(kc), vc, None)
    if kl_ref is not None:
        for start in range(0, kl_ref.shape[0], KV_CHUNK):
            state = _softmax_step(scores(kl_ref[start:start + KV_CHUNK, :]),
                                  vl_ref[start:start + KV_CHUNK, :], state)
    _, l, acc = state
    return acc / l


def _store_normed(heads, gain_ref, o_ref):
    ms = sum(jnp.sum(o * o, axis=-1, keepdims=True) for o in heads) * (1.0 / (len(heads) * HEAD_DIM))
    r = lax.rsqrt(ms + RMS_EPS)
    for h, o in enumerate(heads):
        o_ref[:, _head(h)] = (o * r * gain_ref[:, _head(h)]).astype(o_ref.dtype)


def _gqa_heads(q, n_kv, group, pg_ref, k_h, v_h, kl_refs, vl_refs):
    tq = q.shape[0]
    heads = []
    for g in range(n_kv):
        qs = jnp.concatenate([q[:, _head(g * group + i)] for i in range(group)], axis=0)
        o = _attend(qs, pg_ref[:, _head(k_h + g)], pg_ref[:, _head(v_h + g)],
                    kl_refs[g] if kl_refs else None, vl_refs[g] if vl_refs else None)
        heads += [o[i * tq:(i + 1) * tq] for i in range(group)]
    return heads


def _gqa_lat_kernel(q_ref, pg_ref, *rest):
    kl_refs = rest[:A_KV_HEADS]
    vl_refs = rest[A_KV_HEADS:2 * A_KV_HEADS]
    gain_ref, o_ref = rest[2 * A_KV_HEADS:]
    heads = _gqa_heads(q_ref[...], A_KV_HEADS, A_GROUP, pg_ref, AK_H, AV_H, kl_refs, vl_refs)
    _store_normed(heads, gain_ref, o_ref)


def _gqa_ctx_kernel(n_kv, group, q_h, k_h, v_h, pg_ref, gain_ref, o_ref):
    q = pg_ref[:, q_h * HEAD_DIM:(q_h + n_kv * group) * HEAD_DIM]
    _store_normed(_gqa_heads(q, n_kv, group, pg_ref, k_h, v_h, None, None), gain_ref, o_ref)


def _kv_specs(k_h, v_h, n_heads):
    heads = [k_h + g for g in range(n_heads)] + [v_h + g for g in range(n_heads)]
    return [pl.BlockSpec((None, SEQ, HEAD_DIM), lambda bi, qi, h=h: (bi, 0, h)) for h in heads]


def _gqa_lat(px, pg, gain):
    b = px.shape[0]
    return pl.pallas_call(
        _gqa_lat_kernel,
        grid=(b, SEQ // TQ_A),
        in_specs=[
            pl.BlockSpec((None, TQ_A, A_WIDTH), lambda bi, qi: (bi, qi, AQ_H * HEAD_DIM // A_WIDTH)),
            pl.BlockSpec((None, CTX_LEN, IN_WIDTH), lambda bi, qi: (bi, 0, 0)),
            *_kv_specs(AK_H, AV_H, A_KV_HEADS),
            pl.BlockSpec((1, A_WIDTH), lambda bi, qi: (0, 0)),
        ],
        out_specs=pl.BlockSpec((None, TQ_A, A_WIDTH), lambda bi, qi: (bi, qi, 0)),
        out_shape=jax.ShapeDtypeStruct((b, SEQ, A_WIDTH), BF16),
        compiler_params=_params("parallel", "arbitrary"),
        name="gqa_attention",
    )(px, pg, *([px] * (2 * A_KV_HEADS)), gain)


def _gqa_ctx(pg, gain, n_kv, group, q_h, k_h, v_h):
    b = pg.shape[0]
    width = n_kv * group * HEAD_DIM
    return pl.pallas_call(
        functools.partial(_gqa_ctx_kernel, n_kv, group, q_h, k_h, v_h),
        grid=(b,),
        in_specs=[
            pl.BlockSpec((None, CTX_LEN, IN_WIDTH), lambda bi: (bi, 0, 0)),
            pl.BlockSpec((1, width), lambda bi: (0, 0)),
        ],
        out_specs=pl.BlockSpec((None, CTX_LEN, width), lambda bi: (bi, 0, 0)),
        out_shape=jax.ShapeDtypeStruct((b, CTX_LEN, width), BF16),
        compiler_params=_params("parallel"),
        name="ctx_attention",
    )(pg, gain)


def _diff_heads(q, lam_init, lam_ref, gain_ref, pg_ref, kl_refs, vl_refs, o_ref):
    tq = q.shape[0]
    lv = lam_ref[...]
    lam = (jnp.exp(jnp.sum(lv[0:1] * lv[1:2], axis=-1, keepdims=True))
           - jnp.exp(jnp.sum(lv[2:3] * lv[3:4], axis=-1, keepdims=True)) + lam_init)
    low = lax.broadcasted_iota(jnp.int32, (tq, HEAD_DIM), 1) < B_SUB_DIM
    zero = jnp.zeros((tq, HEAD_DIM), q.dtype)
    for h in range(B_HEADS):
        qh = q[:, _head(h)]
        qs = jnp.concatenate([jnp.where(low, qh, zero), jnp.where(low, zero, qh)], axis=0)
        o = _attend(qs, pg_ref[:, _head(BK_H + h)], pg_ref[:, _head(BV_H + h)],
                    kl_refs[h] if kl_refs else None, vl_refs[h] if vl_refs else None)
        o = _rms(o[:tq] - lam * o[tq:]) * gain_ref[...] * (1.0 - lam_init)
        o_ref[:, _head(h)] = o.astype(o_ref.dtype)


def _diff_lat_kernel(lam_init, q_ref, lam_ref, gain_ref, pg_ref, *rest):
    kl_refs = rest[:B_HEADS]
    vl_refs = rest[B_HEADS:2 * B_HEADS]
    o_ref = rest[2 * B_HEADS]
    _diff_heads(q_ref[...], lam_init, lam_ref, gain_ref, pg_ref, kl_refs, vl_refs, o_ref)


def _diff_ctx_kernel(lam_init, lam_ref, gain_ref, pg_ref, o_ref):
    q = pg_ref[:, BQ_H * HEAD_DIM:BK_H * HEAD_DIM]
    _diff_heads(q, lam_init, lam_ref, gain_ref, pg_ref, None, None, o_ref)


def _diff_lat(px, pg, lam_vecs, gain, lam_init):
    b = px.shape[0]
    return pl.pallas_call(
        functools.partial(_diff_lat_kernel, lam_init),
        grid=(b, SEQ // TQ_B),
        in_specs=[
            pl.BlockSpec((None, TQ_B, B_WIDTH), lambda bi, qi: (bi, qi, BQ_H * HEAD_DIM // B_WIDTH)),
            pl.BlockSpec((4, B_SUB_DIM), lambda bi, qi: (0, 0)),
            pl.BlockSpec((1, HEAD_DIM), lambda bi, qi: (0, 0)),
            pl.BlockSpec((None, CTX_LEN, IN_WIDTH), lambda bi, qi: (bi, 0, 0)),
            *_kv_specs(BK_H, BV_H, B_HEADS),
        ],
        out_specs=pl.BlockSpec((None, TQ_B, B_WIDTH), lambda bi, qi: (bi, qi, 0)),
        out_shape=jax.ShapeDtypeStruct((b, SEQ, B_WIDTH), BF16),
        compiler_params=_params("parallel", "arbitrary"),
        name="diff_attention",
    )(px, lam_vecs, gain, pg, *([px] * (2 * B_HEADS)))


def _diff_ctx(pg, lam_vecs, gain, lam_init):
    b = pg.shape[0]
    return pl.pallas_call(
        functools.partial(_diff_ctx_kernel, lam_init),
        grid=(b,),
        in_specs=[
            pl.BlockSpec((4, B_SUB_DIM), lambda bi: (0, 0)),
            pl.BlockSpec((1, HEAD_DIM), lambda bi: (0, 0)),
            pl.BlockSpec((None, CTX_LEN, IN_WIDTH), lambda bi: (bi, 0, 0)),
        ],
        out_specs=pl.BlockSpec((None, CTX_LEN, B_WIDTH), lambda bi: (bi, 0, 0)),
        out_shape=jax.ShapeDtypeStruct((b, CTX_LEN, B_WIDTH), BF16),
        compiler_params=_params("parallel"),
        name="diff_ctx_attention",
    )(lam_vecs, gain, pg)


def _nbr_block_class(j):
    return jnp.where(j == 0, 0, jnp.where(j == GRID_H // NB_ROWS - 1, 2, 1))


def _nbr_fill_bias(pairs_ref, bias_ref):
    n_blocks = GRID_H // NB_ROWS
    lane = lax.broadcasted_iota(jnp.int32, (GRID_W, 2 * GRID_W), 1)
    masked = jnp.full((GRID_W, 2 * GRID_W), MASK_VALUE, F32)
    for cls, j in enumerate((0, 1, n_blocks - 1)):
        key_row0 = int(np.clip(NB_ROWS * j - NA_ROWS // 2, 0, GRID_H - NB_KROWS))
        for i in range(NB_ROWS):
            r = NB_ROWS * j + i
            rs = int(np.clip(r - NA_ROWS // 2, 0, GRID_H - NA_ROWS))
            for pair in range(NB_KROWS // 2):
                kr = key_row0 + 2 * pair
                left_in = rs <= kr < rs + NA_ROWS
                right_in = rs <= kr + 1 < rs + NA_ROWS
                offset = kr - r + NA_ROWS - 1
                for h in range(C_HEADS):
                    if left_in or right_in:
                        tile = pairs_ref[h, offset + 1]
                        if not right_in:
                            tile = jnp.where(lane < GRID_W, tile, MASK_VALUE)
                        if not left_in:
                            tile = jnp.where(lane < GRID_W, MASK_VALUE, tile)
                    else:
                        tile = masked
                    bias_ref[cls, h, i * GRID_W:(i + 1) * GRID_W,
                             pair * 2 * GRID_W:(pair + 1) * 2 * GRID_W] = tile


def _nbr_kernel(q_ref, k_ref, v_ref, pg_ref, pairs_ref, gain_ref, o_ref, bias_ref):
    j = pl.program_id(1)

    @pl.when((pl.program_id(0) == 0) & (j == 0))
    def _():
        _nbr_fill_bias(pairs_ref, bias_ref)

    cls = _nbr_block_class(j)
    key_row0 = jnp.clip(NB_ROWS * j - NA_ROWS // 2, 0, GRID_H - NB_KROWS)
    start = pl.multiple_of(key_row0 * GRID_W, NB_ROWS * GRID_W)
    heads = []
    for h in range(C_HEADS):
        q = q_ref[:, _head(h)]
        kw = k_ref[pl.ds(start, NB_K), _head(h)]
        vw = v_ref[pl.ds(start, NB_K), _head(h)]
        kc = pg_ref[:, _head(CK_H + h)]
        vc = pg_ref[:, _head(CV_H + h)]
        s_w = lax.dot_general(q, kw, _NT, preferred_element_type=F32) + bias_ref[cls, h]
        s_c = lax.dot_general(q, kc, _NT, preferred_element_type=F32)
        m = jnp.maximum(jnp.max(s_w, axis=-1, keepdims=True), jnp.max(s_c, axis=-1, keepdims=True))
        p_w = jnp.exp2(s_w - m)
        p_c = jnp.exp2(s_c - m)
        l = jnp.sum(p_w, axis=-1, keepdims=True) + jnp.sum(p_c, axis=-1, keepdims=True)
        o = (jnp.dot(p_w.astype(BF16), vw, preferred_element_type=F32)
             + jnp.dot(p_c.astype(BF16), vc, preferred_element_type=F32))
        heads.append(o / l)
    _store_normed(heads, gain_ref, o_ref)


def _nbr_pair_table(rel_bias):
    col = np.arange(GRID_W)
    cs = np.clip(col - NA_COLS // 2, 0, GRID_W - NA_COLS)
    col_ok = (col[None, :] >= cs[:, None]) & (col[None, :] < cs[:, None] + NA_COLS)
    dc = col[None, :] - col[:, None] + NA_COLS - 1
    onehot = (dc[None] == np.arange(2 * NA_COLS - 1)[:, None, None]) & col_ok[None]
    per_row = jnp.einsum('hrt,tck->hrck', rel_bias.astype(F32) * LOG2E, jnp.asarray(onehot, F32),
                         precision=lax.Precision.HIGHEST)
    per_row = jnp.where(col_ok, per_row, MASK_VALUE)
    per_row = jnp.pad(per_row, ((0, 0), (1, 1), (0, 0), (0, 0)), constant_values=MASK_VALUE)
    return jnp.concatenate([per_row[:, :-1], per_row[:, 1:]], axis=-1)


def _nbr(px, pg, pairs, gain):
    b = px.shape[0]
    n_blocks = GRID_H // NB_ROWS
    blk = lambda h: h * HEAD_DIM // C_WIDTH
    once = pl.Buffered(1)
    return pl.pallas_call(
        _nbr_kernel,
        grid=(b, n_blocks),
        in_specs=[
            pl.BlockSpec((None, NB_Q, C_WIDTH), lambda bi, j: (bi, j, blk(CQ_H))),
            pl.BlockSpec((None, SEQ, C_WIDTH), lambda bi, j: (bi, 0, blk(CK_H)), pipeline_mode=once),
            pl.BlockSpec((None, SEQ, C_WIDTH), lambda bi, j: (bi, 0, blk(CV_H)), pipeline_mode=once),
            pl.BlockSpec((None, CTX_LEN, IN_WIDTH), lambda bi, j: (bi, 0, 0)),
            pl.BlockSpec(pairs.shape, lambda bi, j: (0, 0, 0, 0), pipeline_mode=once),
            pl.BlockSpec((1, C_WIDTH), lambda bi, j: (0, 0)),
        ],
        out_specs=pl.BlockSpec((None, NB_Q, C_WIDTH), lambda bi, j: (bi, j, 0)),
        out_shape=jax.ShapeDtypeStruct((b, SEQ, C_WIDTH), BF16),
        scratch_shapes=[pltpu.VMEM((3, C_HEADS, NB_Q, NB_K), F32)],
        compiler_params=_params("arbitrary", "arbitrary"),
        name="nbr_attention",
    )(px, px, px, pg, pairs, gain)


def _outproj_kernel(x_ref, mod_ref, a_ref, b_ref, c_ref, w_ref, o_ref):
    y = (jnp.dot(a_ref[...], w_ref[:A_WIDTH, :].astype(BF16), preferred_element_type=F32)
         + jnp.dot(b_ref[...], w_ref[A_WIDTH:A_WIDTH + B_WIDTH, :].astype(BF16),
                   preferred_element_type=F32)
         + jnp.dot(c_ref[...], w_ref[A_WIDTH + B_WIDTH:, :].astype(BF16),
                   preferred_element_type=F32))
    o_ref[...] = x_ref[...] + mod_ref[5:6, :] * y


def _outproj(x2, mods, mod_at, layer, a2, b2, c2, w_out, tm):
    m, d = x2.shape
    return pl.pallas_call(
        _outproj_kernel,
        grid=(m // tm, d // TN_OUT),
        in_specs=[
            pl.BlockSpec((tm, TN_OUT), lambda i, j: (i, j)),
            pl.BlockSpec((None, None, N_MOD, TN_OUT), lambda i, j: (layer, mod_at(i), 0, j)),
            pl.BlockSpec((tm, A_WIDTH), lambda i, j: (i, 0)),
            pl.BlockSpec((tm, B_WIDTH), lambda i, j: (i, 0)),
            pl.BlockSpec((tm, C_WIDTH), lambda i, j: (i, 0)),
            pl.BlockSpec((None, MIX_WIDTH, TN_OUT), lambda i, j: (layer, 0, j)),
        ],
        out_specs=pl.BlockSpec((tm, TN_OUT), lambda i, j: (i, j)),
        out_shape=jax.ShapeDtypeStruct((m, d), F32),
        compiler_params=_params("parallel", "arbitrary"),
        name="outproj",
    )(x2, mods, a2, b2, c2, w_out)


def _column_gains(a_q, a_k, b_q, b_k, c_q, c_k):
    a_scale = LOG2E * HEAD_DIM ** -0.5
    b_scale = LOG2E * B_SUB_DIM ** -0.5
    parts = (
        jnp.tile(c_q * a_scale, C_HEADS), jnp.tile(c_k, C_HEADS), jnp.ones((C_WIDTH,), F32),
        jnp.tile(a_q * a_scale, A_HEADS), jnp.tile(a_k, A_KV_HEADS), jnp.ones((A_KV_WIDTH,), F32),
        jnp.tile(b_q * b_scale, 2 * B_HEADS), jnp.tile(b_k, 2 * B_HEADS), jnp.ones((B_WIDTH,), F32),
    )
    return jnp.concatenate(parts).reshape(1, IN_WIDTH)


def kernel(x, c, ctx, c_ctx, w_mod, b_mod, ffn1_w_gu, ffn1_w_down, ffn2_w_gu, ffn2_w_down,
           w_in, w_out, a_q_gain, a_k_gain, a_out_gain, b_q_gain, b_k_gain, b_lambda, b_out_gain,
           c_q_gain, c_k_gain, c_rel_bias, c_out_gain):
    b, s, d = x.shape
    n_ctx = ctx.shape[1]
    assert (b, s, d, n_ctx) == (2, SEQ, D_MODEL, CTX_LEN)

    cvec = jnp.zeros((8, d), F32).at[:b].set(c).at[b].set(c_ctx)
    mods = _modulation(cvec, w_mod, b_mod).reshape(DEPTH, 8, N_MOD, d)
    tables = _rope_tables(HEAD_DIM // 4) + _rope_tables(B_SUB_DIM // 4)

    x2 = x.reshape(b * s, d)
    g2 = ctx.reshape(b * n_ctx, d)
    tm_g = b * n_ctx

    mod_x = lambda i: i // (s // TM)
    mod_x_out = lambda i: i // (s // TM_OUT)
    mod_x_ffn = lambda i: i // (s // TM_FFN)
    mod_g = lambda i: b

    for l in range(DEPTH):
        with_ctx = l < DEPTH - 1
        lam_init = 0.8 - 0.6 * math.exp(-0.3 * l)
        gains = _column_gains(a_q_gain[l], a_k_gain[l], b_q_gain[l], b_k_gain[l],
                              c_q_gain[l], c_k_gain[l])
        a_gain = a_out_gain[l].reshape(1, A_WIDTH)
        b_gain = b_out_gain[l].reshape(1, HEAD_DIM)
        c_gain = c_out_gain[l].reshape(1, C_WIDTH)

        x2 = _ffn(x2, mods, mod_x_ffn, l, ffn1_w_gu, ffn1_w_down, 0, TM_FFN, TF)
        g2 = _ffn(g2, mods, mod_g, l, ffn1_w_gu, ffn1_w_down, 0, tm_g, TF_CTX)

        px = _inproj(x2, mods, mod_x, l, w_in, gains, tables, TM).reshape(b, s, IN_WIDTH)
        pg = _inproj(g2, mods, mod_g, l, w_in, gains, None, tm_g).reshape(b, n_ctx, IN_WIDTH)

        a_lat = _gqa_lat(px, pg, a_gain)
        b_lat = _diff_lat(px, pg, b_lambda[l], b_gain, lam_init)
        c_lat = _nbr(px, pg, _nbr_pair_table(c_rel_bias[l]), c_gain)

        x2 = _outproj(x2, mods, mod_x_out, l, a_lat.reshape(b * s, A_WIDTH),
                      b_lat.reshape(b * s, B_WIDTH), c_lat.reshape(b * s, C_WIDTH), w_out, TM_OUT)
        x2 = _ffn(x2, mods, mod_x_ffn, l, ffn2_w_gu, ffn2_w_down, 6, TM_FFN, TF)

        if with_ctx:
            a_ctx = _gqa_ctx(pg, a_gain, A_KV_HEADS, A_GROUP, AQ_H, AK_H, AV_H)
            b_ctx = _diff_ctx(pg, b_lambda[l], b_gain, lam_init)
            c_ctx = _gqa_ctx(pg, c_gain, C_HEADS, 1, CQ_H, CK_H, CV_H)
            g2 = _outproj(g2, mods, mod_g, l, a_ctx.reshape(tm_g, A_WIDTH),
                          b_ctx.reshape(tm_g, B_WIDTH), c_ctx.reshape(tm_g, C_WIDTH), w_out, tm_g)
            g2 = _ffn(g2, mods, mod_g, l, ffn2_w_gu, ffn2_w_down, 6, tm_g, TF_CTX)

    return x2.reshape(b, s, d)
```

```python
import functools
import math

import numpy as np
import jax
import jax.numpy as jnp
from jax import lax
from jax.experimental import pallas as pl
from jax.experimental.pallas import tpu as pltpu

F32 = jnp.float32
BF16 = jnp.bfloat16

D_MODEL = 2048
SEQ = 4096
DEPTH = 2
GRID_W = 64
GRID_H = SEQ // GRID_W
CTX_LEN = 256
HEAD_DIM = 128
N_MOD = 9
FFN_HIDDEN = 5632
RMS_EPS = 1e-6
ROPE_THETA = 10000.0

A_HEADS = 6
A_KV_HEADS = 2
A_GROUP = A_HEADS // A_KV_HEADS
B_HEADS = 4
C_HEADS = 6
B_SUB_DIM = HEAD_DIM // 2
A_WIDTH = A_HEADS * HEAD_DIM
A_KV_WIDTH = A_KV_HEADS * HEAD_DIM
B_WIDTH = B_HEADS * HEAD_DIM
C_WIDTH = C_HEADS * HEAD_DIM
MIX_WIDTH = A_WIDTH + B_WIDTH + C_WIDTH
IN_WIDTH = A_WIDTH + 2 * A_KV_WIDTH + 3 * B_WIDTH + 3 * C_WIDTH
NA_ROWS = 8
NA_COLS = 16

CQ_H = 0
CK_H = CQ_H + C_HEADS
CV_H = CK_H + C_HEADS
AQ_H = CV_H + C_HEADS
AK_H = AQ_H + A_HEADS
AV_H = AK_H + A_KV_HEADS
BQ_H = AV_H + A_KV_HEADS
BK_H = BQ_H + B_HEADS
BV_H = BK_H + B_HEADS

VMEM_LIMIT = 56 * 1024 * 1024
FFN_VMEM_LIMIT = 61 * 1024 * 1024
MASK_VALUE = -1e30
LOG2E = math.log2(math.e)

TM = 1024
TM_OUT = 2048
TM_FFN = 1024
TF = 512
TF_CTX = 512
TN = 256
TN_OUT = 512
IN_TILES_PER_STEP = 2
ROW_GROUPS = 4
EDGE_ROWS = 256
TQ_A = 256
TQ_B = 256
KV_CHUNK = 2048
NB_ROWS = 4
NB_Q = NB_ROWS * GRID_W
NB_KROWS = 12
NB_K = NB_KROWS * GRID_W

N_IN_TILES = IN_WIDTH // TN
SRC_SHIFT = (IN_WIDTH - 3 * C_WIDTH) // TN


def _silu(v):
    return v / (1.0 + jnp.exp(-v))


def _rms(v):
    return v * lax.rsqrt(jnp.mean(v * v, axis=-1, keepdims=True) + RMS_EPS)


def _head(h):
    return slice(h * HEAD_DIM, (h + 1) * HEAD_DIM)


def _params(*sem, vmem_limit=VMEM_LIMIT):
    return pltpu.CompilerParams(dimension_semantics=sem, vmem_limit_bytes=vmem_limit)


def _mod_kernel(c_ref, w_ref, b_ref, o_ref):
    s = _silu(c_ref[...]).astype(BF16)
    o_ref[...] = jnp.dot(s, w_ref[...].astype(BF16), preferred_element_type=F32) + b_ref[...]


def _modulation(cvec, w_mod, b_mod):
    depth, d, n = w_mod.shape
    tn = 2048
    return pl.pallas_call(
        _mod_kernel,
        grid=(depth, n // tn),
        in_specs=[
            pl.BlockSpec((8, d), lambda l, j: (0, 0)),
            pl.BlockSpec((None, d, tn), lambda l, j: (l, 0, j)),
            pl.BlockSpec((None, 1, tn), lambda l, j: (l, 0, j)),
        ],
        out_specs=pl.BlockSpec((None, 8, tn), lambda l, j: (l, 0, j)),
        out_shape=jax.ShapeDtypeStruct((depth, 8, n), F32),
        compiler_params=_params("parallel", "parallel"),
        name="modulation",
    )(cvec, w_mod, b_mod.reshape(depth, 1, n))


def _ffn_kernel(mod_row, nf, x_ref, mod_ref, wg_ref, wu_ref, wd_ref, o_ref, xn_ref):
    f = pl.program_id(1)
    tm = x_ref.shape[0]

    def step(first, last, groups):
        wg = wg_ref[...].astype(BF16)
        wu = wu_ref[...].astype(BF16)
        wd = wd_ref[...].astype(BF16)
        for r in range(groups):
            rows = slice(r * (tm // groups), (r + 1) * (tm // groups))
            if first:
                shift = mod_ref[mod_row:mod_row + 1, :]
                scale = mod_ref[mod_row + 1:mod_row + 2, :]
                xn = (_rms(x_ref[rows, :]) * (1.0 + scale) + shift).astype(BF16)
                xn_ref[rows, :] = xn
            else:
                xn = xn_ref[rows, :]
            g = jnp.dot(xn, wg, preferred_element_type=F32)
            u = jnp.dot(xn, wu, preferred_element_type=F32)
            a = (_silu(g) * u).astype(BF16)
            y = jnp.dot(a, wd, preferred_element_type=F32)
            if not first:
                y = o_ref[rows, :] + y
            if last:
                gate = mod_ref[mod_row + 2:mod_row + 3, :]
                y = x_ref[rows, :] + (0.5 * gate) * y
            o_ref[rows, :] = y

    edge_groups = tm // EDGE_ROWS
    pl.when(f == 0)(lambda: step(True, False, edge_groups))
    pl.when((f > 0) & (f < nf - 1))(lambda: step(False, False, 1))
    pl.when(f == nf - 1)(lambda: step(False, True, edge_groups))


def _ffn(x2, mods, mod_at, layer, w_gu, w_down, mod_row, tm, tf):
    m, d = x2.shape
    hidden = w_down.shape[1]
    nf = hidden // tf
    return pl.pallas_call(
        functools.partial(_ffn_kernel, mod_row, nf),
        grid=(m // tm, nf),
        in_specs=[
            pl.BlockSpec((tm, d), lambda i, f: (i, 0), pipeline_mode=pl.Buffered(1)),
            pl.BlockSpec((None, None, N_MOD, d), lambda i, f: (layer, mod_at(i), 0, 0)),
            pl.BlockSpec((None, d, tf), lambda i, f: (layer, 0, f)),
            pl.BlockSpec((None, d, tf), lambda i, f: (layer, 0, nf + f)),
            pl.BlockSpec((None, tf, d), lambda i, f: (layer, f, 0)),
        ],
        out_specs=pl.BlockSpec((tm, d), lambda i, f: (i, 0)),
        out_shape=jax.ShapeDtypeStruct((m, d), F32),
        scratch_shapes=[pltpu.VMEM((tm, d), BF16)],
        compiler_params=_params("parallel", "arbitrary", vmem_limit=FFN_VMEM_LIMIT),
        name="ffn",
    )(x2, mods, w_gu, w_gu, w_down)


def _norm_lanes(y, width):
    y2 = y * y
    if width == HEAD_DIM:
        ms = jnp.mean(y2, axis=-1, keepdims=True)
    else:
        lane = lax.broadcasted_iota(jnp.int32, y.shape, 1)
        low = lane < width
        s_lo = jnp.sum(jnp.where(low, y2, 0.0), axis=-1, keepdims=True)
        s_hi = jnp.sum(jnp.where(low, 0.0, y2), axis=-1, keepdims=True)
        ms = jnp.where(low, s_lo, s_hi) * (1.0 / width)
    return y * lax.rsqrt(ms + RMS_EPS)


def _tile_modes():
    per_head = (['c'] * (2 * C_HEADS) + ['p'] * C_HEADS + ['a'] * (A_HEADS + A_KV_HEADS)
                + ['p'] * A_KV_HEADS + ['b'] * (2 * B_HEADS) + ['p'] * B_HEADS)
    hpt = TN // HEAD_DIM
    tiles = [set(per_head[t * hpt:(t + 1) * hpt]) for t in range(N_IN_TILES)]
    assert all(len(t) == 1 for t in tiles)
    return [t.pop() for t in tiles]


def _inproj_kernel(rope, x_ref, mod_ref, *rest):
    w_refs = rest[:IN_TILES_PER_STEP]
    gain_ref = rest[IN_TILES_PER_STEP]
    rest = rest[IN_TILES_PER_STEP + 1:]
    if rope:
        ca_ref, sa_ref, cb_ref, sb_ref, o_ref, xn_ref = rest
    else:
        o_ref, xn_ref = rest
    j = pl.program_id(1)
    heads_per_tile = TN // HEAD_DIM
    group_rows = x_ref.shape[0] // ROW_GROUPS
    rope_tables = {'a': (ca_ref, sa_ref, HEAD_DIM // 4),
                   'b': (cb_ref, sb_ref, B_SUB_DIM // 4)} if rope else {}
    lane = lax.broadcasted_iota(jnp.int32, (group_rows, HEAD_DIM), 1)
    norm_width = {'a': HEAD_DIM, 'b': B_SUB_DIM, 'c': HEAD_DIM, 'p': 0}

    def emit(modes, first=False):
        ws = [w_ref[...].astype(BF16) for w_ref in w_refs]
        for r in range(ROW_GROUPS):
            rows = slice(r * group_rows, (r + 1) * group_rows)
            if first:
                shift = mod_ref[3:4, :]
                scale = mod_ref[4:5, :]
                xn = (_rms(x_ref[rows, :]) * (1.0 + scale) + shift).astype(BF16)
                xn_ref[rows, :] = xn
            else:
                xn = xn_ref[rows, :]
            for t, mode in enumerate(modes):
                y = jnp.dot(xn, ws[t], preferred_element_type=F32)
                for hh in range(heads_per_tile):
                    cols = _head(t * heads_per_tile + hh)
                    v = y[:, _head(hh)]
                    if norm_width[mode]:
                        v = _norm_lanes(v, norm_width[mode]) * gain_ref[:, cols]
                    if mode in rope_tables:
                        c_ref, s_ref, half = rope_tables[mode]
                        partner = jnp.where((lane & (2 * half - 1)) < half,
                                            pltpu.roll(v, HEAD_DIM - half, 1), pltpu.roll(v, half, 1))
                        v = v * c_ref[rows, :] + partner * s_ref[rows, :]
                    o_ref[rows, cols] = v.astype(o_ref.dtype)

    tile_modes = _tile_modes()
    steps = {}
    for step in range(N_IN_TILES // IN_TILES_PER_STEP):
        modes = tuple(tile_modes[step * IN_TILES_PER_STEP:(step + 1) * IN_TILES_PER_STEP])
        steps.setdefault(modes, []).append(step)

    pl.when(j == 0)(functools.partial(emit, tuple(tile_modes[:IN_TILES_PER_STEP]), first=True))
    for modes, where in steps.items():
        later = [t for t in where if t > 0]
        if later:
            cond = functools.reduce(lambda a, b: a | b, [j == t for t in later])
            pl.when(cond)(functools.partial(emit, modes))


def _inproj(x2, mods, mod_at, layer, w_in, gains, tables, tm):
    m, d = x2.shape
    rope = tables is not None
    tn = TN * IN_TILES_PER_STEP
    src = lambda t: lax.rem(t + SRC_SHIFT, N_IN_TILES)
    w_spec = lambda k: pl.BlockSpec((None, d, TN),
                                    lambda i, j: (layer, 0, src(j * IN_TILES_PER_STEP + k)))
    in_specs = [
        pl.BlockSpec((tm, d), lambda i, j: (i, 0), pipeline_mode=pl.Buffered(1)),
        pl.BlockSpec((None, None, N_MOD, d), lambda i, j: (layer, mod_at(i), 0, 0)),
        *[w_spec(k) for k in range(IN_TILES_PER_STEP)],
        pl.BlockSpec((1, tn), lambda i, j: (0, j)),
    ]
    args = [x2, mods] + [w_in] * IN_TILES_PER_STEP + [gains]
    if rope:
        pos_tiles = SEQ // tm
        for t in tables:
            in_specs.append(pl.BlockSpec((tm, HEAD_DIM), lambda i, j: (i % pos_tiles, 0)))
            args.append(t)
    return pl.pallas_call(
        functools.partial(_inproj_kernel, rope),
        grid=(m // tm, N_IN_TILES // IN_TILES_PER_STEP),
        in_specs=in_specs,
        out_specs=pl.BlockSpec((tm, tn), lambda i, j: (i, j)),
        out_shape=jax.ShapeDtypeStruct((m, IN_WIDTH), BF16),
        scratch_shapes=[pltpu.VMEM((tm, d), BF16)],
        compiler_params=_params("parallel", "arbitrary"),
        name="inproj",
    )(*args)


def _rope_tables(half):
    t = jnp.arange(SEQ, dtype=jnp.int32)
    row = (t // GRID_W).astype(F32)[:, None]
    col = (t % GRID_W).astype(F32)[:, None]
    lane = np.arange(HEAD_DIM)
    freqs = ROPE_THETA ** (-jnp.asarray(lane % half, dtype=F32) / half)
    is_col = jnp.asarray((lane % (4 * half)) >= 2 * half)[None, :]
    ang = jnp.where(is_col, col, row) * freqs[None, :]
    first = jnp.asarray((lane % (2 * half)) < half)[None, :]
    cos = jnp.cos(ang)
    sin = jnp.sin(ang)
    return cos, jnp.where(first, -sin, sin)


_NT = (((1,), (1,)), ((), ()))


def _softmax_step(s, v, state):
    m_cur = jnp.max(s, axis=-1, keepdims=True)
    if state is None:
        m_new = m_cur
        p = jnp.exp2(s - m_new)
        l_new = jnp.sum(p, axis=-1, keepdims=True)
        acc = jnp.dot(p.astype(BF16), v, preferred_element_type=F32)
    else:
        m_old, l_old, acc_old = state
        m_new = jnp.maximum(m_old, m_cur)
        alpha = jnp.exp2(m_old - m_new)
        p = jnp.exp2(s - m_new)
        l_new = alpha * l_old + jnp.sum(p, axis=-1, keepdims=True)
        acc = alpha * acc_old + jnp.dot(p.astype(BF16), v, preferred_element_type=F32)
    return m_new, l_new, acc


def _attend(qs, kc, vc, kl_ref, vl_ref):
    def scores(k):
        return lax.dot_general(qs, k, _NT, preferred_element_type=F32)

    if kl_ref is None:
        _, l, acc = _softmax_step(scores(kc), vc, None)
        return acc / l
    k0 = jnp.concatenate([kc, kl_ref[:KV_CHUNK, :]], axis=0)
    v0 = jnp.concatenate([vc, vl_ref[:KV_CHUNK, :]], axis=0)
    state = _softmax_step(scores(k0), v0, None)
    for start in range(KV_CHUNK, kl_ref.shape[0], KV_CHUNK):
        state = _softmax_step(scores(kl_ref[start:start + KV_CHUNK, :]),
                              vl_ref[start:start + KV_CHUNK, :], state)
    _, l, acc = state
    return acc / l


def _store_normed(heads, gain_ref, o_ref):
    ms = sum(jnp.sum(o * o, axis=-1, keepdims=True) for o in heads) * (1.0 / (len(heads) * HEAD_DIM))
    r = lax.rsqrt(ms + RMS_EPS)
    for h, o in enumerate(heads):
        o_ref[:, _head(h)] = (o * r * gain_ref[:, _head(h)]).astype(o_ref.dtype)


def _gqa_heads(q, n_kv, group, pg_ref, k_h, v_h, kl_refs, vl_refs):
    tq = q.shape[0]
    heads = []
    for g in range(n_kv):
        qs = jnp.concatenate([q[:, _head(g * group + i)] for i in range(group)], axis=0)
        o = _attend(qs, pg_ref[:, _head(k_h + g)], pg_ref[:, _head(v_h + g)],
                    kl_refs[g] if kl_refs else None, vl_refs[g] if vl_refs else None)
        heads += [o[i * tq:(i + 1) * tq] for i in range(group)]
    return heads


def _gqa_lat_kernel(q_ref, pg_ref, *rest):
    kl_refs = rest[:A_KV_HEADS]
    vl_refs = rest[A_KV_HEADS:2 * A_KV_HEADS]
    gain_ref, o_ref = rest[2 * A_KV_HEADS:]
    heads = _gqa_heads(q_ref[...], A_KV_HEADS, A_GROUP, pg_ref, AK_H, AV_H, kl_refs, vl_refs)
    _store_normed(heads, gain_ref, o_ref)


def _gqa_ctx_kernel(n_kv, group, q_h, k_h, v_h, pg_ref, gain_ref, o_ref):
    q = pg_ref[:, q_h * HEAD_DIM:(q_h + n_kv * group) * HEAD_DIM]
    _store_normed(_gqa_heads(q, n_kv, group, pg_ref, k_h, v_h, None, None), gain_ref, o_ref)


def _kv_specs(k_h, v_h, n_heads):
    heads = [k_h + g for g in range(n_heads)] + [v_h + g for g in range(n_heads)]
    return [pl.BlockSpec((None, SEQ, HEAD_DIM), lambda bi, qi, h=h: (bi, 0, h)) for h in heads]


def _gqa_lat(px, pg, gain):
    b = px.shape[0]
    return pl.pallas_call(
        _gqa_lat_kernel,
        grid=(b, SEQ // TQ_A),
        in_specs=[
            pl.BlockSpec((None, TQ_A, A_WIDTH), lambda bi, qi: (bi, qi, AQ_H * HEAD_DIM // A_WIDTH)),
            pl.BlockSpec((None, CTX_LEN, IN_WIDTH), lambda bi, qi: (bi, 0, 0)),
            *_kv_specs(AK_H, AV_H, A_KV_HEADS),
            pl.BlockSpec((1, A_WIDTH), lambda bi, qi: (0, 0)),
        ],
        out_specs=pl.BlockSpec((None, TQ_A, A_WIDTH), lambda bi, qi: (bi, qi, 0)),
        out_shape=jax.ShapeDtypeStruct((b, SEQ, A_WIDTH), BF16),
        compiler_params=_params("parallel", "arbitrary"),
        name="gqa_attention",
    )(px, pg, *([px] * (2 * A_KV_HEADS)), gain)


def _gqa_ctx(pg, gain, n_kv, group, q_h, k_h, v_h):
    b = pg.shape[0]
    width = n_kv * group * HEAD_DIM
    return pl.pallas_call(
        functools.partial(_gqa_ctx_kernel, n_kv, group, q_h, k_h, v_h),
        grid=(b,),
        in_specs=[
            pl.BlockSpec((None, CTX_LEN, IN_WIDTH), lambda bi: (bi, 0, 0)),
            pl.BlockSpec((1, width), lambda bi: (0, 0)),
        ],
        out_specs=pl.BlockSpec((None, CTX_LEN, width), lambda bi: (bi, 0, 0)),
        out_shape=jax.ShapeDtypeStruct((b, CTX_LEN, width), BF16),
        compiler_params=_params("parallel"),
        name="ctx_attention",
    )(pg, gain)


def _diff_heads(q, lam_init, lam_ref, gain_ref, pg_ref, kl_refs, vl_refs, o_ref):
    tq = q.shape[0]
    lv = lam_ref[...]
    lam = (jnp.exp(jnp.sum(lv[0:1] * lv[1:2], axis=-1, keepdims=True))
           - jnp.exp(jnp.sum(lv[2:3] * lv[3:4], axis=-1, keepdims=True)) + lam_init)
    low = lax.broadcasted_iota(jnp.int32, (tq, HEAD_DIM), 1) < B_SUB_DIM
    zero = jnp.zeros((tq, HEAD_DIM), q.dtype)
    for h in range(B_HEADS):
        qh = q[:, _head(h)]
        qs = jnp.concatenate([jnp.where(low, qh, zero), jnp.where(low, zero, qh)], axis=0)
        o = _attend(qs, pg_ref[:, _head(BK_H + h)], pg_ref[:, _head(BV_H + h)],
                    kl_refs[h] if kl_refs else None, vl_refs[h] if vl_refs else None)
        o = _rms(o[:tq] - lam * o[tq:]) * gain_ref[...] * (1.0 - lam_init)
        o_ref[:, _head(h)] = o.astype(o_ref.dtype)


def _diff_lat_kernel(lam_init, q_ref, lam_ref, gain_ref, pg_ref, *rest):
    kl_refs = rest[:B_HEADS]
    vl_refs = rest[B_HEADS:2 * B_HEADS]
    o_ref = rest[2 * B_HEADS]
    _diff_heads(q_ref[...], lam_init, lam_ref, gain_ref, pg_ref, kl_refs, vl_refs, o_ref)


def _diff_ctx_kernel(lam_init, lam_ref, gain_ref, pg_ref, o_ref):
    q = pg_ref[:, BQ_H * HEAD_DIM:BK_H * HEAD_DIM]
    _diff_heads(q, lam_init, lam_ref, gain_ref, pg_ref, None, None, o_ref)


def _diff_lat(px, pg, lam_vecs, gain, lam_init):
    b = px.shape[0]
    return pl.pallas_call(
        functools.partial(_diff_lat_kernel, lam_init),
        grid=(b, SEQ // TQ_B),
        in_specs=[
            pl.BlockSpec((None, TQ_B, B_WIDTH), lambda bi, qi: (bi, qi, BQ_H * HEAD_DIM // B_WIDTH)),
            pl.BlockSpec((4, B_SUB_DIM), lambda bi, qi: (0, 0)),
            pl.BlockSpec((1, HEAD_DIM), lambda bi, qi: (0, 0)),
            pl.BlockSpec((None, CTX_LEN, IN_WIDTH), lambda bi, qi: (bi, 0, 0)),
            *_kv_specs(BK_H, BV_H, B_HEADS),
        ],
        out_specs=pl.BlockSpec((None, TQ_B, B_WIDTH), lambda bi, qi: (bi, qi, 0)),
        out_shape=jax.ShapeDtypeStruct((b, SEQ, B_WIDTH), BF16),
        compiler_params=_params("parallel", "arbitrary"),
        name="diff_attention",
    )(px, lam_vecs, gain, pg, *([px] * (2 * B_HEADS)))


def _diff_ctx(pg, lam_vecs, gain, lam_init):
    b = pg.shape[0]
    return pl.pallas_call(
        functools.partial(_diff_ctx_kernel, lam_init),
        grid=(b,),
        in_specs=[
            pl.BlockSpec((4, B_SUB_DIM), lambda bi: (0, 0)),
            pl.BlockSpec((1, HEAD_DIM), lambda bi: (0, 0)),
            pl.BlockSpec((None, CTX_LEN, IN_WIDTH), lambda bi: (bi, 0, 0)),
        ],
        out_specs=pl.BlockSpec((None, CTX_LEN, B_WIDTH), lambda bi: (bi, 0, 0)),
        out_shape=jax.ShapeDtypeStruct((b, CTX_LEN, B_WIDTH), BF16),
        compiler_params=_params("parallel"),
        name="diff_ctx_attention",
    )(lam_vecs, gain, pg)


def _nbr_block_class(j):
    return jnp.where(j == 0, 0, jnp.where(j == GRID_H // NB_ROWS - 1, 2, 1))


def _nbr_fill_bias(pairs_ref, bias_ref):
    n_blocks = GRID_H // NB_ROWS
    lane = lax.broadcasted_iota(jnp.int32, (GRID_W, 2 * GRID_W), 1)
    masked = jnp.full((GRID_W, 2 * GRID_W), MASK_VALUE, F32)
    for cls, j in enumerate((0, 1, n_blocks - 1)):
        key_row0 = int(np.clip(NB_ROWS * j - NA_ROWS // 2, 0, GRID_H - NB_KROWS))
        for i in range(NB_ROWS):
            r = NB_ROWS * j + i
            rs = int(np.clip(r - NA_ROWS // 2, 0, GRID_H - NA_ROWS))
            for pair in range(NB_KROWS // 2):
                kr = key_row0 + 2 * pair
                left_in = rs <= kr < rs + NA_ROWS
                right_in = rs <= kr + 1 < rs + NA_ROWS
                offset = kr - r + NA_ROWS - 1
                for h in range(C_HEADS):
                    if left_in or right_in:
                        tile = pairs_ref[h, offset + 1]
                        if not right_in:
                            tile = jnp.where(lane < GRID_W, tile, MASK_VALUE)
                        if not left_in:
                            tile = jnp.where(lane < GRID_W, MASK_VALUE, tile)
                    else:
                        tile = masked
                    bias_ref[cls, h, i * GRID_W:(i + 1) * GRID_W,
                             pair * 2 * GRID_W:(pair + 1) * 2 * GRID_W] = tile


def _nbr_kernel(q_ref, k_ref, v_ref, pg_ref, pairs_ref, gain_ref, o_ref, bias_ref):
    j = pl.program_id(1)

    @pl.when((pl.program_id(0) == 0) & (j == 0))
    def _():
        _nbr_fill_bias(pairs_ref, bias_ref)

    cls = _nbr_block_class(j)
    key_row0 = jnp.clip(NB_ROWS * j - NA_ROWS // 2, 0, GRID_H - NB_KROWS)
    start = pl.multiple_of(key_row0 * GRID_W, NB_ROWS * GRID_W)
    heads = []
    for h in range(C_HEADS):
        q = q_ref[:, _head(h)]
        kw = k_ref[pl.ds(start, NB_K), _head(h)]
        vw = v_ref[pl.ds(start, NB_K), _head(h)]
        kc = pg_ref[:, _head(CK_H + h)]
        vc = pg_ref[:, _head(CV_H + h)]
        s_w = lax.dot_general(q, kw, _NT, preferred_element_type=F32) + bias_ref[cls, h]
        s_c = lax.dot_general(q, kc, _NT, preferred_element_type=F32)
        m = jnp.maximum(jnp.max(s_w, axis=-1, keepdims=True), jnp.max(s_c, axis=-1, keepdims=True))
        p_w = jnp.exp2(s_w - m)
        p_c = jnp.exp2(s_c - m)
        l = jnp.sum(p_w, axis=-1, keepdims=True) + jnp.sum(p_c, axis=-1, keepdims=True)
        o = (jnp.dot(p_w.astype(BF16), vw, preferred_element_type=F32)
             + jnp.dot(p_c.astype(BF16), vc, preferred_element_type=F32))
        heads.append(o / l)
    _store_normed(heads, gain_ref, o_ref)


def _nbr_pair_table(rel_bias):
    col = np.arange(GRID_W)
    cs = np.clip(col - NA_COLS // 2, 0, GRID_W - NA_COLS)
    col_ok = (col[None, :] >= cs[:, None]) & (col[None, :] < cs[:, None] + NA_COLS)
    dc = col[None, :] - col[:, None] + NA_COLS - 1
    onehot = (dc[None] == np.arange(2 * NA_COLS - 1)[:, None, None]) & col_ok[None]
    per_row = jnp.einsum('hrt,tck->hrck', rel_bias.astype(F32) * LOG2E, jnp.asarray(onehot, F32),
                         precision=lax.Precision.HIGHEST)
    per_row = jnp.where(col_ok, per_row, MASK_VALUE)
    per_row = jnp.pad(per_row, ((0, 0), (1, 1), (0, 0), (0, 0)), constant_values=MASK_VALUE)
    return jnp.concatenate([per_row[:, :-1], per_row[:, 1:]], axis=-1)


def _nbr(px, pg, pairs, gain):
    b = px.shape[0]
    n_blocks = GRID_H // NB_ROWS
    blk = lambda h: h * HEAD_DIM // C_WIDTH
    once = pl.Buffered(1)
    return pl.pallas_call(
        _nbr_kernel,
        grid=(b, n_blocks),
        in_specs=[
            pl.BlockSpec((None, NB_Q, C_WIDTH), lambda bi, j: (bi, j, blk(CQ_H))),
            pl.BlockSpec((None, SEQ, C_WIDTH), lambda bi, j: (bi, 0, blk(CK_H)), pipeline_mode=once),
            pl.BlockSpec((None, SEQ, C_WIDTH), lambda bi, j: (bi, 0, blk(CV_H)), pipeline_mode=once),
            pl.BlockSpec((None, CTX_LEN, IN_WIDTH), lambda bi, j: (bi, 0, 0)),
            pl.BlockSpec(pairs.shape, lambda bi, j: (0, 0, 0, 0), pipeline_mode=once),
            pl.BlockSpec((1, C_WIDTH), lambda bi, j: (0, 0)),
        ],
        out_specs=pl.BlockSpec((None, NB_Q, C_WIDTH), lambda bi, j: (bi, j, 0)),
        out_shape=jax.ShapeDtypeStruct((b, SEQ, C_WIDTH), BF16),
        scratch_shapes=[pltpu.VMEM((3, C_HEADS, NB_Q, NB_K), F32)],
        compiler_params=_params("arbitrary", "arbitrary"),
        name="nbr_attention",
    )(px, px, px, pg, pairs, gain)


def _outproj_kernel(x_ref, mod_ref, a_ref, b_ref, c_ref, w_ref, o_ref):
    y = (jnp.dot(a_ref[...], w_ref[:A_WIDTH, :].astype(BF16), preferred_element_type=F32)
         + jnp.dot(b_ref[...], w_ref[A_WIDTH:A_WIDTH + B_WIDTH, :].astype(BF16),
                   preferred_element_type=F32)
         + jnp.dot(c_ref[...], w_ref[A_WIDTH + B_WIDTH:, :].astype(BF16),
                   preferred_element_type=F32))
    o_ref[...] = x_ref[...] + mod_ref[5:6, :] * y


def _outproj(x2, mods, mod_at, layer, a2, b2, c2, w_out, tm):
    m, d = x2.shape
    return pl.pallas_call(
        _outproj_kernel,
        grid=(m // tm, d // TN_OUT),
        in_specs=[
            pl.BlockSpec((tm, TN_OUT), lambda i, j: (i, j)),
            pl.BlockSpec((None, None, N_MOD, TN_OUT), lambda i, j: (layer, mod_at(i), 0, j)),
            pl.BlockSpec((tm, A_WIDTH), lambda i, j: (i, 0)),
            pl.BlockSpec((tm, B_WIDTH), lambda i, j: (i, 0)),
            pl.BlockSpec((tm, C_WIDTH), lambda i, j: (i, 0)),
            pl.BlockSpec((None, MIX_WIDTH, TN_OUT), lambda i, j: (layer, 0, j)),
        ],
        out_specs=pl.BlockSpec((tm, TN_OUT), lambda i, j: (i, j)),
        out_shape=jax.ShapeDtypeStruct((m, d), F32),
        compiler_params=_params("parallel", "arbitrary"),
        name="outproj",
    )(x2, mods, a2, b2, c2, w_out)


def _column_gains(a_q, a_k, b_q, b_k, c_q, c_k):
    a_scale = LOG2E * HEAD_DIM ** -0.5
    b_scale = LOG2E * B_SUB_DIM ** -0.5
    parts = (
        jnp.tile(c_q * a_scale, C_HEADS), jnp.tile(c_k, C_HEADS), jnp.ones((C_WIDTH,), F32),
        jnp.tile(a_q * a_scale, A_HEADS), jnp.tile(a_k, A_KV_HEADS), jnp.ones((A_KV_WIDTH,), F32),
        jnp.tile(b_q * b_scale, 2 * B_HEADS), jnp.tile(b_k, 2 * B_HEADS), jnp.ones((B_WIDTH,), F32),
    )
    return jnp.concatenate(parts).reshape(1, IN_WIDTH)


def kernel(x, c, ctx, c_ctx, w_mod, b_mod, ffn1_w_gu, ffn1_w_down, ffn2_w_gu, ffn2_w_down,
           w_in, w_out, a_q_gain, a_k_gain, a_out_gain, b_q_gain, b_k_gain, b_lambda, b_out_gain,
           c_q_gain, c_k_gain, c_rel_bias, c_out_gain):
    b, s, d = x.shape
    n_ctx = ctx.shape[1]
    assert (b, s, d, n_ctx) == (2, SEQ, D_MODEL, CTX_LEN)

    cvec = jnp.zeros((8, d), F32).at[:b].set(c).at[b].set(c_ctx)
    mods = _modulation(cvec, w_mod, b_mod).reshape(DEPTH, 8, N_MOD, d)
    tables = _rope_tables(HEAD_DIM // 4) + _rope_tables(B_SUB_DIM // 4)

    x2 = x.reshape(b * s, d)
    g2 = ctx.reshape(b * n_ctx, d)
    tm_g = b * n_ctx

    mod_x = lambda i: i // (s // TM)
    mod_x_out = lambda i: i // (s // TM_OUT)
    mod_x_ffn = lambda i: i // (s // TM_FFN)
    mod_g = lambda i: b

    for l in range(DEPTH):
        with_ctx = l < DEPTH - 1
        lam_init = 0.8 - 0.6 * math.exp(-0.3 * l)
        gains = _column_gains(a_q_gain[l], a_k_gain[l], b_q_gain[l], b_k_gain[l],
                              c_q_gain[l], c_k_gain[l])
        a_gain = a_out_gain[l].reshape(1, A_WIDTH)
        b_gain = b_out_gain[l].reshape(1, HEAD_DIM)
        c_gain = c_out_gain[l].reshape(1, C_WIDTH)

        x2 = _ffn(x2, mods, mod_x_ffn, l, ffn1_w_gu, ffn1_w_down, 0, TM_FFN, TF)
        g2 = _ffn(g2, mods, mod_g, l, ffn1_w_gu, ffn1_w_down, 0, tm_g, TF_CTX)

        px = _inproj(x2, mods, mod_x, l, w_in, gains, tables, TM).reshape(b, s, IN_WIDTH)
        pg = _inproj(g2, mods, mod_g, l, w_in, gains, None, tm_g).reshape(b, n_ctx, IN_WIDTH)

        a_lat = _gqa_lat(px, pg, a_gain)
        b_lat = _diff_lat(px, pg, b_lambda[l], b_gain, lam_init)
        c_lat = _nbr(px, pg, _nbr_pair_table(c_rel_bias[l]), c_gain)

        x2 = _outproj(x2, mods, mod_x_out, l, a_lat.reshape(b * s, A_WIDTH),
                      b_lat.reshape(b * s, B_WIDTH), c_lat.reshape(b * s, C_WIDTH), w_out, TM_OUT)
        x2 = _ffn(x2, mods, mod_x_ffn, l, ffn2_w_gu, ffn2_w_down, 6, TM_FFN, TF)

        if with_ctx:
            a_ctx = _gqa_ctx(pg, a_gain, A_KV_HEADS, A_GROUP, AQ_H, AK_H, AV_H)
            b_ctx = _diff_ctx(pg, b_lambda[l], b_gain, lam_init)
            c_ctx = _gqa_ctx(pg, c_gain, C_HEADS, 1, CQ_H, CK_H, CV_H)
            g2 = _outproj(g2, mods, mod_g, l, a_ctx.reshape(tm_g, A_WIDTH),
                          b_ctx.reshape(tm_g, B_WIDTH), c_ctx.reshape(tm_g, C_WIDTH), w_out, tm_g)
            g2 = _ffn(g2, mods, mod_g, l, ffn2_w_gu, ffn2_w_down, 6, tm_g, TF_CTX)

    return x2.reshape(b, s, d)
```

```python
import functools
import math

import numpy as np
import jax
import jax.numpy as jnp
from jax import lax
from jax.experimental import pallas as pl
from jax.experimental.pallas import tpu as pltpu

F32 = jnp.float32
BF16 = jnp.bfloat16

D_MODEL = 2048
SEQ = 4096
DEPTH = 2
GRID_W = 64
GRID_H = SEQ // GRID_W
CTX_LEN = 256
HEAD_DIM = 128
N_MOD = 9
FFN_HIDDEN = 5632
RMS_EPS = 1e-6
ROPE_THETA = 10000.0

A_HEADS = 6
A_KV_HEADS = 2
A_GROUP = A_HEADS // A_KV_HEADS
B_HEADS = 4
C_HEADS = 6
B_SUB_DIM = HEAD_DIM // 2
A_WIDTH = A_HEADS * HEAD_DIM
A_KV_WIDTH = A_KV_HEADS * HEAD_DIM
B_WIDTH = B_HEADS * HEAD_DIM
C_WIDTH = C_HEADS * HEAD_DIM
MIX_WIDTH = A_WIDTH + B_WIDTH + C_WIDTH
IN_WIDTH = A_WIDTH + 2 * A_KV_WIDTH + 3 * B_WIDTH + 3 * C_WIDTH
NA_ROWS = 8
NA_COLS = 16

CQ_H = 0
CK_H = CQ_H + C_HEADS
CV_H = CK_H + C_HEADS
AQ_H = CV_H + C_HEADS
AK_H = AQ_H + A_HEADS
AV_H = AK_H + A_KV_HEADS
BQ_H = AV_H + A_KV_HEADS
BK_H = BQ_H + B_HEADS
BV_H = BK_H + B_HEADS

VMEM_LIMIT = 56 * 1024 * 1024
FFN_VMEM_LIMIT = 61 * 1024 * 1024
MASK_VALUE = -1e30
LOG2E = math.log2(math.e)

TM = 1024
TM_OUT = 2048
TM_FFN = 1024
TF = 512
TF_CTX = 512
TN = 256
TN_OUT = 512
IN_TILES_PER_STEP = 2
ROW_GROUPS = 4
EDGE_ROWS = 256
TQ_A = 256
TQ_B = 256
KV_CHUNK = 2048
NB_ROWS = 4
NB_Q = NB_ROWS * GRID_W
NB_KROWS = 12
NB_K = NB_KROWS * GRID_W

N_IN_TILES = IN_WIDTH // TN
SRC_SHIFT = (IN_WIDTH - 3 * C_WIDTH) // TN


def _silu(v):
    return v / (1.0 + jnp.exp(-v))


def _rms(v):
    return v * lax.rsqrt(jnp.mean(v * v, axis=-1, keepdims=True) + RMS_EPS)


def _head(h):
    return slice(h * HEAD_DIM, (h + 1) * HEAD_DIM)


def _params(*sem, vmem_limit=VMEM_LIMIT):
    return pltpu.CompilerParams(dimension_semantics=sem, vmem_limit_bytes=vmem_limit)


def _mod_kernel(c_ref, w_ref, b_ref, o_ref):
    s = _silu(c_ref[...]).astype(BF16)
    o_ref[...] = jnp.dot(s, w_ref[...].astype(BF16), preferred_element_type=F32) + b_ref[...]


def _modulation(cvec, w_mod, b_mod):
    depth, d, n = w_mod.shape
    tn = 2048
    return pl.pallas_call(
        _mod_kernel,
        grid=(depth, n // tn),
        in_specs=[
            pl.BlockSpec((8, d), lambda l, j: (0, 0)),
            pl.BlockSpec((None, d, tn), lambda l, j: (l, 0, j)),
            pl.BlockSpec((None, 1, tn), lambda l, j: (l, 0, j)),
        ],
        out_specs=pl.BlockSpec((None, 8, tn), lambda l, j: (l, 0, j)),
        out_shape=jax.ShapeDtypeStruct((depth, 8, n), F32),
        compiler_params=_params("parallel", "parallel"),
        name="modulation",
    )(cvec, w_mod, b_mod.reshape(depth, 1, n))


def _ffn_kernel(mod_row, nf, x_ref, mod_ref, wg_ref, wu_ref, wd_ref, o_ref, xn_ref):
    f = pl.program_id(1)
    tm = x_ref.shape[0]

    def step(first, last, groups):
        wg = wg_ref[...].astype(BF16)
        wu = wu_ref[...].astype(BF16)
        wd = wd_ref[...].astype(BF16)
        for r in range(groups):
            rows = slice(r * (tm // groups), (r + 1) * (tm // groups))
            if first:
                shift = mod_ref[mod_row:mod_row + 1, :]
                scale = mod_ref[mod_row + 1:mod_row + 2, :]
                xn = (_rms(x_ref[rows, :]) * (1.0 + scale) + shift).astype(BF16)
                xn_ref[rows, :] = xn
            else:
                xn = xn_ref[rows, :]
            g = jnp.dot(xn, wg, preferred_element_type=F32)
            u = jnp.dot(xn, wu, preferred_element_type=F32)
            a = (_silu(g) * u).astype(BF16)
            y = jnp.dot(a, wd, preferred_element_type=F32)
            if not first:
                y = o_ref[rows, :] + y
            if last:
                gate = mod_ref[mod_row + 2:mod_row + 3, :]
                y = x_ref[rows, :] + (0.5 * gate) * y
            o_ref[rows, :] = y

    edge_groups = tm // EDGE_ROWS
    pl.when(f == 0)(lambda: step(True, False, edge_groups))
    pl.when((f > 0) & (f < nf - 1))(lambda: step(False, False, 1))
    pl.when(f == nf - 1)(lambda: step(False, True, edge_groups))


def _ffn(x2, mods, mod_at, layer, w_gu, w_down, mod_row, tm, tf):
    m, d = x2.shape
    hidden = w_down.shape[1]
    nf = hidden // tf
    return pl.pallas_call(
        functools.partial(_ffn_kernel, mod_row, nf),
        grid=(m // tm, nf),
        in_specs=[
            pl.BlockSpec((tm, d), lambda i, f: (i, 0), pipeline_mode=pl.Buffered(1)),
            pl.BlockSpec((None, None, N_MOD, d), lambda i, f: (layer, mod_at(i), 0, 0)),
            pl.BlockSpec((None, d, tf), lambda i, f: (layer, 0, f)),
            pl.BlockSpec((None, d, tf), lambda i, f: (layer, 0, nf + f)),
            pl.BlockSpec((None, tf, d), lambda i, f: (layer, f, 0)),
        ],
        out_specs=pl.BlockSpec((tm, d), lambda i, f: (i, 0)),
        out_shape=jax.ShapeDtypeStruct((m, d), F32),
        scratch_shapes=[pltpu.VMEM((tm, d), BF16)],
        compiler_params=_params("parallel", "arbitrary", vmem_limit=FFN_VMEM_LIMIT),
        name="ffn",
    )(x2, mods, w_gu, w_gu, w_down)


def _norm_lanes(y, width):
    y2 = y * y
    if width == HEAD_DIM:
        ms = jnp.mean(y2, axis=-1, keepdims=True)
    else:
        lane = lax.broadcasted_iota(jnp.int32, y.shape, 1)
        low = lane < width
        s_lo = jnp.sum(jnp.where(low, y2, 0.0), axis=-1, keepdims=True)
        s_hi = jnp.sum(jnp.where(low, 0.0, y2), axis=-1, keepdims=True)
        ms = jnp.where(low, s_lo, s_hi) * (1.0 / width)
    return y * lax.rsqrt(ms + RMS_EPS)


def _tile_modes():
    per_head = (['c'] * (2 * C_HEADS) + ['p'] * C_HEADS + ['a'] * (A_HEADS + A_KV_HEADS)
                + ['p'] * A_KV_HEADS + ['b'] * (2 * B_HEADS) + ['p'] * B_HEADS)
    hpt = TN // HEAD_DIM
    tiles = [set(per_head[t * hpt:(t + 1) * hpt]) for t in range(N_IN_TILES)]
    assert all(len(t) == 1 for t in tiles)
    return [t.pop() for t in tiles]


def _inproj_kernel(rope, x_ref, mod_ref, *rest):
    w_refs = rest[:IN_TILES_PER_STEP]
    gain_ref = rest[IN_TILES_PER_STEP]
    rest = rest[IN_TILES_PER_STEP + 1:]
    if rope:
        ca_ref, sa_ref, cb_ref, sb_ref, o_ref, xn_ref = rest
    else:
        o_ref, xn_ref = rest
    j = pl.program_id(1)
    heads_per_tile = TN // HEAD_DIM
    group_rows = x_ref.shape[0] // ROW_GROUPS
    rope_tables = {'a': (ca_ref, sa_ref, HEAD_DIM // 4),
                   'b': (cb_ref, sb_ref, B_SUB_DIM // 4)} if rope else {}
    lane = lax.broadcasted_iota(jnp.int32, (group_rows, HEAD_DIM), 1)
    norm_width = {'a': HEAD_DIM, 'b': B_SUB_DIM, 'c': HEAD_DIM, 'p': 0}

    def emit(modes, first=False):
        ws = [w_ref[...].astype(BF16) for w_ref in w_refs]
        for r in range(ROW_GROUPS):
            rows = slice(r * group_rows, (r + 1) * group_rows)
            if first:
                shift = mod_ref[3:4, :]
                scale = mod_ref[4:5, :]
                xn = (_rms(x_ref[rows, :]) * (1.0 + scale) + shift).astype(BF16)
                xn_ref[rows, :] = xn
            else:
                xn = xn_ref[rows, :]
            for t, mode in enumerate(modes):
                y = jnp.dot(xn, ws[t], preferred_element_type=F32)
                for hh in range(heads_per_tile):
                    cols = _head(t * heads_per_tile + hh)
                    v = y[:, _head(hh)]
                    if norm_width[mode]:
                        v = _norm_lanes(v, norm_width[mode]) * gain_ref[:, cols]
                    if mode in rope_tables:
                        c_ref, s_ref, half = rope_tables[mode]
                        partner = jnp.where((lane & (2 * half - 1)) < half,
                                            pltpu.roll(v, HEAD_DIM - half, 1), pltpu.roll(v, half, 1))
                        v = v * c_ref[rows, :] + partner * s_ref[rows, :]
                    o_ref[rows, cols] = v.astype(o_ref.dtype)

    tile_modes = _tile_modes()
    steps = {}
    for step in range(N_IN_TILES // IN_TILES_PER_STEP):
        modes = tuple(tile_modes[step * IN_TILES_PER_STEP:(step + 1) * IN_TILES_PER_STEP])
        steps.setdefault(modes, []).append(step)

    pl.when(j == 0)(functools.partial(emit, tuple(tile_modes[:IN_TILES_PER_STEP]), first=True))
    for modes, where in steps.items():
        later = [t for t in where if t > 0]
        if later:
            cond = functools.reduce(lambda a, b: a | b, [j == t for t in later])
            pl.when(cond)(functools.partial(emit, modes))


def _inproj(x2, mods, mod_at, layer, w_in, gains, tables, tm):
    m, d = x2.shape
    rope = tables is not None
    tn = TN * IN_TILES_PER_STEP
    src = lambda t: lax.rem(t + SRC_SHIFT, N_IN_TILES)
    w_spec = lambda k: pl.BlockSpec((None, d, TN),
                                    lambda i, j: (layer, 0, src(j * IN_TILES_PER_STEP + k)))
    in_specs = [
        pl.BlockSpec((tm, d), lambda i, j: (i, 0)),
        pl.BlockSpec((None, None, N_MOD, d), lambda i, j: (layer, mod_at(i), 0, 0)),
        *[w_spec(k) for k in range(IN_TILES_PER_STEP)],
        pl.BlockSpec((1, tn), lambda i, j: (0, j)),
    ]
    args = [x2, mods] + [w_in] * IN_TILES_PER_STEP + [gains]
    if rope:
        pos_tiles = SEQ // tm
        for t in tables:
            in_specs.append(pl.BlockSpec((tm, HEAD_DIM), lambda i, j: (i % pos_tiles, 0)))
            args.append(t)
    return pl.pallas_call(
        functools.partial(_inproj_kernel, rope),
        grid=(m // tm, N_IN_TILES // IN_TILES_PER_STEP),
        in_specs=in_specs,
        out_specs=pl.BlockSpec((tm, tn), lambda i, j: (i, j)),
        out_shape=jax.ShapeDtypeStruct((m, IN_WIDTH), BF16),
        scratch_shapes=[pltpu.VMEM((tm, d), BF16)],
        compiler_params=_params("parallel", "arbitrary"),
        name="inproj",
    )(*args)


def _rope_tables(half):
    t = jnp.arange(SEQ, dtype=jnp.int32)
    row = (t // GRID_W).astype(F32)[:, None]
    col = (t % GRID_W).astype(F32)[:, None]
    lane = np.arange(HEAD_DIM)
    freqs = ROPE_THETA ** (-jnp.asarray(lane % half, dtype=F32) / half)
    is_col = jnp.asarray((lane % (4 * half)) >= 2 * half)[None, :]
    ang = jnp.where(is_col, col, row) * freqs[None, :]
    first = jnp.asarray((lane % (2 * half)) < half)[None, :]
    cos = jnp.cos(ang)
    sin = jnp.sin(ang)
    return cos, jnp.where(first, -sin, sin)


_NT = (((1,), (1,)), ((), ()))


def _softmax_step(s, v, state):
    m_cur = jnp.max(s, axis=-1, keepdims=True)
    if state is None:
        m_new = m_cur
        p = jnp.exp2(s - m_new)
        l_new = jnp.sum(p, axis=-1, keepdims=True)
        acc = jnp.dot(p.astype(BF16), v, preferred_element_type=F32)
    else:
        m_old, l_old, acc_old = state
        m_new = jnp.maximum(m_old, m_cur)
        alpha = jnp.exp2(m_old - m_new)
        p = jnp.exp2(s - m_new)
        l_new = alpha * l_old + jnp.sum(p, axis=-1, keepdims=True)
        acc = alpha * acc_old + jnp.dot(p.astype(BF16), v, preferred_element_type=F32)
    return m_new, l_new, acc


def _attend(qs, kc, vc, kl_ref, vl_ref):
    def scores(k):
        return lax.dot_general(qs, k, _NT, preferred_element_type=F32)

    if kl_ref is None:
        _, l, acc = _softmax_step(scores(kc), vc, None)
        return acc / l
    k0 = jnp.concatenate([kc, kl_ref[:KV_CHUNK, :]], axis=0)
    v0 = jnp.concatenate([vc, vl_ref[:KV_CHUNK, :]], axis=0)
    state = _softmax_step(scores(k0), v0, None)
    for start in range(KV_CHUNK, kl_ref.shape[0], KV_CHUNK):
        state = _softmax_step(scores(kl_ref[start:start + KV_CHUNK, :]),
                              vl_ref[start:start + KV_CHUNK, :], state)
    _, l, acc = state
    return acc / l


def _store_normed(heads, gain_ref, o_ref):
    ms = sum(jnp.sum(o * o, axis=-1, keepdims=True) for o in heads) * (1.0 / (len(heads) * HEAD_DIM))
    r = lax.rsqrt(ms + RMS_EPS)
    for h, o in enumerate(heads):
        o_ref[:, _head(h)] = (o * r * gain_ref[:, _head(h)]).astype(o_ref.dtype)


def _gqa_heads(q, n_kv, group, pg_ref, k_h, v_h, kl_refs, vl_refs):
    tq = q.shape[0]
    heads = []
    for g in range(n_kv):
        qs = jnp.concatenate([q[:, _head(g * group + i)] for i in range(group)], axis=0)
        o = _attend(qs, pg_ref[:, _head(k_h + g)], pg_ref[:, _head(v_h + g)],
                    kl_refs[g] if kl_refs else None, vl_refs[g] if vl_refs else None)
        heads += [o[i * tq:(i + 1) * tq] for i in range(group)]
    return heads


def _gqa_lat_kernel(q_ref, pg_ref, *rest):
    kl_refs = rest[:A_KV_HEADS]
    vl_refs = rest[A_KV_HEADS:2 * A_KV_HEADS]
    gain_ref, o_ref = rest[2 * A_KV_HEADS:]
    heads = _gqa_heads(q_ref[...], A_KV_HEADS, A_GROUP, pg_ref, AK_H, AV_H, kl_refs, vl_refs)
    _store_normed(heads, gain_ref, o_ref)


def _gqa_ctx_kernel(n_kv, group, q_h, k_h, v_h, pg_ref, gain_ref, o_ref):
    q = pg_ref[:, q_h * HEAD_DIM:(q_h + n_kv * group) * HEAD_DIM]
    _store_normed(_gqa_heads(q, n_kv, group, pg_ref, k_h, v_h, None, None), gain_ref, o_ref)


def _kv_specs(k_h, v_h, n_heads):
    heads = [k_h + g for g in range(n_heads)] + [v_h + g for g in range(n_heads)]
    return [pl.BlockSpec((None, SEQ, HEAD_DIM), lambda bi, qi, h=h: (bi, 0, h)) for h in heads]


def _gqa_lat(px, pg, gain):
    b = px.shape[0]
    return pl.pallas_call(
        _gqa_lat_kernel,
        grid=(b, SEQ // TQ_A),
        in_specs=[
            pl.BlockSpec((None, TQ_A, A_WIDTH), lambda bi, qi: (bi, qi, AQ_H * HEAD_DIM // A_WIDTH)),
            pl.BlockSpec((None, CTX_LEN, IN_WIDTH), lambda bi, qi: (bi, 0, 0)),
            *_kv_specs(AK_H, AV_H, A_KV_HEADS),
            pl.BlockSpec((1, A_WIDTH), lambda bi, qi: (0, 0)),
        ],
        out_specs=pl.BlockSpec((None, TQ_A, A_WIDTH), lambda bi, qi: (bi, qi, 0)),
        out_shape=jax.ShapeDtypeStruct((b, SEQ, A_WIDTH), BF16),
        compiler_params=_params("parallel", "arbitrary"),
        name="gqa_attention",
    )(px, pg, *([px] * (2 * A_KV_HEADS)), gain)


def _gqa_ctx(pg, gain, n_kv, group, q_h, k_h, v_h):
    b = pg.shape[0]
    width = n_kv * group * HEAD_DIM
    return pl.pallas_call(
        functools.partial(_gqa_ctx_kernel, n_kv, group, q_h, k_h, v_h),
        grid=(b,),
        in_specs=[
            pl.BlockSpec((None, CTX_LEN, IN_WIDTH), lambda bi: (bi, 0, 0)),
            pl.BlockSpec((1, width), lambda bi: (0, 0)),
        ],
        out_specs=pl.BlockSpec((None, CTX_LEN, width), lambda bi: (bi, 0, 0)),
        out_shape=jax.ShapeDtypeStruct((b, CTX_LEN, width), BF16),
        compiler_params=_params("parallel"),
        name="ctx_attention",
    )(pg, gain)


def _diff_heads(q, lam_init, lam_ref, gain_ref, pg_ref, kl_refs, vl_refs, o_ref):
    tq = q.shape[0]
    lv = lam_ref[...]
    lam = (jnp.exp(jnp.sum(lv[0:1] * lv[1:2], axis=-1, keepdims=True))
           - jnp.exp(jnp.sum(lv[2:3] * lv[3:4], axis=-1, keepdims=True)) + lam_init)
    low = lax.broadcasted_iota(jnp.int32, (tq, HEAD_DIM), 1) < B_SUB_DIM
    zero = jnp.zeros((tq, HEAD_DIM), q.dtype)
    for h in range(B_HEADS):
        qh = q[:, _head(h)]
        qs = jnp.concatenate([jnp.where(low, qh, zero), jnp.where(low, zero, qh)], axis=0)
        o = _attend(qs, pg_ref[:, _head(BK_H + h)], pg_ref[:, _head(BV_H + h)],
                    kl_refs[h] if kl_refs else None, vl_refs[h] if vl_refs else None)
        o = _rms(o[:tq] - lam * o[tq:]) * gain_ref[...] * (1.0 - lam_init)
        o_ref[:, _head(h)] = o.astype(o_ref.dtype)


def _diff_lat_kernel(lam_init, q_ref, lam_ref, gain_ref, pg_ref, *rest):
    kl_refs = rest[:B_HEADS]
    vl_refs = rest[B_HEADS:2 * B_HEADS]
    o_ref = rest[2 * B_HEADS]
    _diff_heads(q_ref[...], lam_init, lam_ref, gain_ref, pg_ref, kl_refs, vl_refs, o_ref)


def _diff_ctx_kernel(lam_init, lam_ref, gain_ref, pg_ref, o_ref):
    q = pg_ref[:, BQ_H * HEAD_DIM:BK_H * HEAD_DIM]
    _diff_heads(q, lam_init, lam_ref, gain_ref, pg_ref, None, None, o_ref)


def _diff_lat(px, pg, lam_vecs, gain, lam_init):
    b = px.shape[0]
    return pl.pallas_call(
        functools.partial(_diff_lat_kernel, lam_init),
        grid=(b, SEQ // TQ_B),
        in_specs=[
            pl.BlockSpec((None, TQ_B, B_WIDTH), lambda bi, qi: (bi, qi, BQ_H * HEAD_DIM // B_WIDTH)),
            pl.BlockSpec((4, B_SUB_DIM), lambda bi, qi: (0, 0)),
            pl.BlockSpec((1, HEAD_DIM), lambda bi, qi: (0, 0)),
            pl.BlockSpec((None, CTX_LEN, IN_WIDTH), lambda bi, qi: (bi, 0, 0)),
            *_kv_specs(BK_H, BV_H, B_HEADS),
        ],
        out_specs=pl.BlockSpec((None, TQ_B, B_WIDTH), lambda bi, qi: (bi, qi, 0)),
        out_shape=jax.ShapeDtypeStruct((b, SEQ, B_WIDTH), BF16),
        compiler_params=_params("parallel", "arbitrary"),
        name="diff_attention",
    )(px, lam_vecs, gain, pg, *([px] * (2 * B_HEADS)))


def _diff_ctx(pg, lam_vecs, gain, lam_init):
    b = pg.shape[0]
    return pl.pallas_call(
        functools.partial(_diff_ctx_kernel, lam_init),
        grid=(b,),
        in_specs=[
            pl.BlockSpec((4, B_SUB_DIM), lambda bi: (0, 0)),
            pl.BlockSpec((1, HEAD_DIM), lambda bi: (0, 0)),
            pl.BlockSpec((None, CTX_LEN, IN_WIDTH), lambda bi: (bi, 0, 0)),
        ],
        out_specs=pl.BlockSpec((None, CTX_LEN, B_WIDTH), lambda bi: (bi, 0, 0)),
        out_shape=jax.ShapeDtypeStruct((b, CTX_LEN, B_WIDTH), BF16),
        compiler_params=_params("parallel"),
        name="diff_ctx_attention",
    )(lam_vecs, gain, pg)


def _nbr_block_class(j):
    return jnp.where(j == 0, 0, jnp.where(j == GRID_H // NB_ROWS - 1, 2, 1))


def _nbr_fill_bias(pairs_ref, bias_ref):
    n_blocks = GRID_H // NB_ROWS
    lane = lax.broadcasted_iota(jnp.int32, (GRID_W, 2 * GRID_W), 1)
    masked = jnp.full((GRID_W, 2 * GRID_W), MASK_VALUE, F32)
    for cls, j in enumerate((0, 1, n_blocks - 1)):
        key_row0 = int(np.clip(NB_ROWS * j - NA_ROWS // 2, 0, GRID_H - NB_KROWS))
        for i in range(NB_ROWS):
            r = NB_ROWS * j + i
            rs = int(np.clip(r - NA_ROWS // 2, 0, GRID_H - NA_ROWS))
            for pair in range(NB_KROWS // 2):
                kr = key_row0 + 2 * pair
                left_in = rs <= kr < rs + NA_ROWS
                right_in = rs <= kr + 1 < rs + NA_ROWS
                offset = kr - r + NA_ROWS - 1
                for h in range(C_HEADS):
                    if left_in or right_in:
                        tile = pairs_ref[h, offset + 1]
                        if not right_in:
                            tile = jnp.where(lane < GRID_W, tile, MASK_VALUE)
                        if not left_in:
                            tile = jnp.where(lane < GRID_W, MASK_VALUE, tile)
                    else:
                        tile = masked
                    bias_ref[cls, h, i * GRID_W:(i + 1) * GRID_W,
                             pair * 2 * GRID_W:(pair + 1) * 2 * GRID_W] = tile


def _nbr_kernel(q_ref, k_ref, v_ref, pg_ref, pairs_ref, gain_ref, o_ref, bias_ref):
    j = pl.program_id(1)

    @pl.when((pl.program_id(0) == 0) & (j == 0))
    def _():
        _nbr_fill_bias(pairs_ref, bias_ref)

    cls = _nbr_block_class(j)
    key_row0 = jnp.clip(NB_ROWS * j - NA_ROWS // 2, 0, GRID_H - NB_KROWS)
    start = pl.multiple_of(key_row0 * GRID_W, NB_ROWS * GRID_W)
    heads = []
    for h in range(C_HEADS):
        q = q_ref[:, _head(h)]
        kw = k_ref[pl.ds(start, NB_K), _head(h)]
        vw = v_ref[pl.ds(start, NB_K), _head(h)]
        kc = pg_ref[:, _head(CK_H + h)]
        vc = pg_ref[:, _head(CV_H + h)]
        s_w = lax.dot_general(q, kw, _NT, preferred_element_type=F32) + bias_ref[cls, h]
        s_c = lax.dot_general(q, kc, _NT, preferred_element_type=F32)
        m = jnp.maximum(jnp.max(s_w, axis=-1, keepdims=True), jnp.max(s_c, axis=-1, keepdims=True))
        p_w = jnp.exp2(s_w - m)
        p_c = jnp.exp2(s_c - m)
        l = jnp.sum(p_w, axis=-1, keepdims=True) + jnp.sum(p_c, axis=-1, keepdims=True)
        o = (jnp.dot(p_w.astype(BF16), vw, preferred_element_type=F32)
             + jnp.dot(p_c.astype(BF16), vc, preferred_element_type=F32))
        heads.append(o / l)
    _store_normed(heads, gain_ref, o_ref)


def _nbr_pair_table(rel_bias):
    col = np.arange(GRID_W)
    cs = np.clip(col - NA_COLS // 2, 0, GRID_W - NA_COLS)
    col_ok = (col[None, :] >= cs[:, None]) & (col[None, :] < cs[:, None] + NA_COLS)
    dc = col[None, :] - col[:, None] + NA_COLS - 1
    onehot = (dc[None] == np.arange(2 * NA_COLS - 1)[:, None, None]) & col_ok[None]
    per_row = jnp.einsum('hrt,tck->hrck', rel_bias.astype(F32) * LOG2E, jnp.asarray(onehot, F32),
                         precision=lax.Precision.HIGHEST)
    per_row = jnp.where(col_ok, per_row, MASK_VALUE)
    per_row = jnp.pad(per_row, ((0, 0), (1, 1), (0, 0), (0, 0)), constant_values=MASK_VALUE)
    return jnp.concatenate([per_row[:, :-1], per_row[:, 1:]], axis=-1)


def _nbr(px, pg, pairs, gain):
    b = px.shape[0]
    n_blocks = GRID_H // NB_ROWS
    blk = lambda h: h * HEAD_DIM // C_WIDTH
    once = pl.Buffered(1)
    return pl.pallas_call(
        _nbr_kernel,
        grid=(b, n_blocks),
        in_specs=[
            pl.BlockSpec((None, NB_Q, C_WIDTH), lambda bi, j: (bi, j, blk(CQ_H))),
            pl.BlockSpec((None, SEQ, C_WIDTH), lambda bi, j: (bi, 0, blk(CK_H)), pipeline_mode=once),
            pl.BlockSpec((None, SEQ, C_WIDTH), lambda bi, j: (bi, 0, blk(CV_H)), pipeline_mode=once),
            pl.BlockSpec((None, CTX_LEN, IN_WIDTH), lambda bi, j: (bi, 0, 0)),
            pl.BlockSpec(pairs.shape, lambda bi, j: (0, 0, 0, 0), pipeline_mode=once),
            pl.BlockSpec((1, C_WIDTH), lambda bi, j: (0, 0)),
        ],
        out_specs=pl.BlockSpec((None, NB_Q, C_WIDTH), lambda bi, j: (bi, j, 0)),
        out_shape=jax.ShapeDtypeStruct((b, SEQ, C_WIDTH), BF16),
        scratch_shapes=[pltpu.VMEM((3, C_HEADS, NB_Q, NB_K), F32)],
        compiler_params=_params("arbitrary", "arbitrary"),
        name="nbr_attention",
    )(px, px, px, pg, pairs, gain)


def _outproj_kernel(x_ref, mod_ref, a_ref, b_ref, c_ref, w_ref, o_ref):
    y = (jnp.dot(a_ref[...], w_ref[:A_WIDTH, :].astype(BF16), preferred_element_type=F32)
         + jnp.dot(b_ref[...], w_ref[A_WIDTH:A_WIDTH + B_WIDTH, :].astype(BF16),
                   preferred_element_type=F32)
         + jnp.dot(c_ref[...], w_ref[A_WIDTH + B_WIDTH:, :].astype(BF16),
                   preferred_element_type=F32))
    o_ref[...] = x_ref[...] + mod_ref[5:6, :] * y


def _outproj(x2, mods, mod_at, layer, a2, b2, c2, w_out, tm):
    m, d = x2.shape
    return pl.pallas_call(
        _outproj_kernel,
        grid=(m // tm, d // TN_OUT),
        in_specs=[
            pl.BlockSpec((tm, TN_OUT), lambda i, j: (i, j)),
            pl.BlockSpec((None, None, N_MOD, TN_OUT), lambda i, j: (layer, mod_at(i), 0, j)),
            pl.BlockSpec((tm, A_WIDTH), lambda i, j: (i, 0)),
            pl.BlockSpec((tm, B_WIDTH), lambda i, j: (i, 0)),
            pl.BlockSpec((tm, C_WIDTH), lambda i, j: (i, 0)),
            pl.BlockSpec((None, MIX_WIDTH, TN_OUT), lambda i, j: (layer, 0, j)),
        ],
        out_specs=pl.BlockSpec((tm, TN_OUT), lambda i, j: (i, j)),
        out_shape=jax.ShapeDtypeStruct((m, d), F32),
        compiler_params=_params("parallel", "arbitrary"),
        name="outproj",
    )(x2, mods, a2, b2, c2, w_out)


def _column_gains(a_q, a_k, b_q, b_k, c_q, c_k):
    a_scale = LOG2E * HEAD_DIM ** -0.5
    b_scale = LOG2E * B_SUB_DIM ** -0.5
    parts = (
        jnp.tile(c_q * a_scale, C_HEADS), jnp.tile(c_k, C_HEADS), jnp.ones((C_WIDTH,), F32),
        jnp.tile(a_q * a_scale, A_HEADS), jnp.tile(a_k, A_KV_HEADS), jnp.ones((A_KV_WIDTH,), F32),
        jnp.tile(b_q * b_scale, 2 * B_HEADS), jnp.tile(b_k, 2 * B_HEADS), jnp.ones((B_WIDTH,), F32),
    )
    return jnp.concatenate(parts).reshape(1, IN_WIDTH)


def kernel(x, c, ctx, c_ctx, w_mod, b_mod, ffn1_w_gu, ffn1_w_down, ffn2_w_gu, ffn2_w_down,
           w_in, w_out, a_q_gain, a_k_gain, a_out_gain, b_q_gain, b_k_gain, b_lambda, b_out_gain,
           c_q_gain, c_k_gain, c_rel_bias, c_out_gain):
    b, s, d = x.shape
    n_ctx = ctx.shape[1]
    assert (b, s, d, n_ctx) == (2, SEQ, D_MODEL, CTX_LEN)

    cvec = jnp.zeros((8, d), F32).at[:b].set(c).at[b].set(c_ctx)
    mods = _modulation(cvec, w_mod, b_mod).reshape(DEPTH, 8, N_MOD, d)
    tables = _rope_tables(HEAD_DIM // 4) + _rope_tables(B_SUB_DIM // 4)

    x2 = x.reshape(b * s, d)
    g2 = ctx.reshape(b * n_ctx, d)
    tm_g = b * n_ctx

    mod_x = lambda i: i // (s // TM)
    mod_x_out = lambda i: i // (s // TM_OUT)
    mod_x_ffn = lambda i: i // (s // TM_FFN)
    mod_g = lambda i: b

    for l in range(DEPTH):
        with_ctx = l < DEPTH - 1
        lam_init = 0.8 - 0.6 * math.exp(-0.3 * l)
        gains = _column_gains(a_q_gain[l], a_k_gain[l], b_q_gain[l], b_k_gain[l],
                              c_q_gain[l], c_k_gain[l])
        a_gain = a_out_gain[l].reshape(1, A_WIDTH)
        b_gain = b_out_gain[l].reshape(1, HEAD_DIM)
        c_gain = c_out_gain[l].reshape(1, C_WIDTH)

        x2 = _ffn(x2, mods, mod_x_ffn, l, ffn1_w_gu, ffn1_w_down, 0, TM_FFN, TF)
        g2 = _ffn(g2, mods, mod_g, l, ffn1_w_gu, ffn1_w_down, 0, tm_g, TF_CTX)

        px = _inproj(x2, mods, mod_x, l, w_in, gains, tables, TM).reshape(b, s, IN_WIDTH)
        pg = _inproj(g2, mods, mod_g, l, w_in, gains, None, tm_g).reshape(b, n_ctx, IN_WIDTH)

        a_lat = _gqa_lat(px, pg, a_gain)
        b_lat = _diff_lat(px, pg, b_lambda[l], b_gain, lam_init)
        c_lat = _nbr(px, pg, _nbr_pair_table(c_rel_bias[l]), c_gain)

        x2 = _outproj(x2, mods, mod_x_out, l, a_lat.reshape(b * s, A_WIDTH),
                      b_lat.reshape(b * s, B_WIDTH), c_lat.reshape(b * s, C_WIDTH), w_out, TM_OUT)
        x2 = _ffn(x2, mods, mod_x_ffn, l, ffn2_w_gu, ffn2_w_down, 6, TM_FFN, TF)

        if with_ctx:
            a_ctx = _gqa_ctx(pg, a_gain, A_KV_HEADS, A_GROUP, AQ_H, AK_H, AV_H)
            b_ctx = _diff_ctx(pg, b_lambda[l], b_gain, lam_init)
            c_ctx = _gqa_ctx(pg, c_gain, C_HEADS, 1, CQ_H, CK_H, CV_H)
            g2 = _outproj(g2, mods, mod_g, l, a_ctx.reshape(tm_g, A_WIDTH),
                          b_ctx.reshape(tm_g, B_WIDTH), c_ctx.reshape(tm_g, C_WIDTH), w_out, tm_g)
            g2 = _ffn(g2, mods, mod_g, l, ffn2_w_gu, ffn2_w_down, 6, tm_g, TF_CTX)

    return x2.reshape(b, s, d)
```

```python
import functools
import math

import numpy as np
import jax
import jax.numpy as jnp
from jax import lax
from jax.experimental import pallas as pl
from jax.experimental.pallas import tpu as pltpu

F32 = jnp.float32
BF16 = jnp.bfloat16

D_MODEL = 2048
SEQ = 4096
DEPTH = 2
GRID_W = 64
GRID_H = SEQ // GRID_W
CTX_LEN = 256
HEAD_DIM = 128
N_MOD = 9
FFN_HIDDEN = 5632
RMS_EPS = 1e-6
ROPE_THETA = 10000.0

A_HEADS = 6
A_KV_HEADS = 2
A_GROUP = A_HEADS // A_KV_HEADS
B_HEADS = 4
C_HEADS = 6
B_SUB_DIM = HEAD_DIM // 2
A_WIDTH = A_HEADS * HEAD_DIM
A_KV_WIDTH = A_KV_HEADS * HEAD_DIM
B_WIDTH = B_HEADS * HEAD_DIM
C_WIDTH = C_HEADS * HEAD_DIM
MIX_WIDTH = A_WIDTH + B_WIDTH + C_WIDTH
IN_WIDTH = A_WIDTH + 2 * A_KV_WIDTH + 3 * B_WIDTH + 3 * C_WIDTH
NA_ROWS = 8
NA_COLS = 16

CQ_H = 0
CK_H = CQ_H + C_HEADS
CV_H = CK_H + C_HEADS
AQ_H = CV_H + C_HEADS
AK_H = AQ_H + A_HEADS
AV_H = AK_H + A_KV_HEADS
BQ_H = AV_H + A_KV_HEADS
BK_H = BQ_H + B_HEADS
BV_H = BK_H + B_HEADS

VMEM_LIMIT = 56 * 1024 * 1024
FFN_VMEM_LIMIT = 61 * 1024 * 1024
MASK_VALUE = -1e30
LOG2E = math.log2(math.e)

TM = 1024
TM_OUT = 2048
TM_FFN = 1024
TF = 512
TF_CTX = 512
TN = 256
TN_OUT = 512
IN_TILES_PER_STEP = 4
ROW_GROUPS = 4
EDGE_ROWS = 256
TQ_A = 256
TQ_B = 256
KV_CHUNK = 2048
NB_ROWS = 4
NB_Q = NB_ROWS * GRID_W
NB_KROWS = 12
NB_K = NB_KROWS * GRID_W

N_IN_TILES = IN_WIDTH // TN
SRC_SHIFT = (IN_WIDTH - 3 * C_WIDTH) // TN


def _silu(v):
    return v / (1.0 + jnp.exp(-v))


def _rms(v):
    return v * lax.rsqrt(jnp.mean(v * v, axis=-1, keepdims=True) + RMS_EPS)


def _head(h):
    return slice(h * HEAD_DIM, (h + 1) * HEAD_DIM)


def _params(*sem, vmem_limit=VMEM_LIMIT):
    return pltpu.CompilerParams(dimension_semantics=sem, vmem_limit_bytes=vmem_limit)


def _mod_kernel(c_ref, w_ref, b_ref, o_ref):
    s = _silu(c_ref[...]).astype(BF16)
    o_ref[...] = jnp.dot(s, w_ref[...].astype(BF16), preferred_element_type=F32) + b_ref[...]


def _modulation(cvec, w_mod, b_mod):
    depth, d, n = w_mod.shape
    tn = 2048
    return pl.pallas_call(
        _mod_kernel,
        grid=(depth, n // tn),
        in_specs=[
            pl.BlockSpec((8, d), lambda l, j: (0, 0)),
            pl.BlockSpec((None, d, tn), lambda l, j: (l, 0, j)),
            pl.BlockSpec((None, 1, tn), lambda l, j: (l, 0, j)),
        ],
        out_specs=pl.BlockSpec((None, 8, tn), lambda l, j: (l, 0, j)),
        out_shape=jax.ShapeDtypeStruct((depth, 8, n), F32),
        compiler_params=_params("parallel", "parallel"),
        name="modulation",
    )(cvec, w_mod, b_mod.reshape(depth, 1, n))


def _ffn_kernel(mod_row, nf, x_ref, mod_ref, wg_ref, wu_ref, wd_ref, o_ref, xn_ref):
    f = pl.program_id(1)
    tm = x_ref.shape[0]

    def step(first, last, groups):
        wg = wg_ref[...].astype(BF16)
        wu = wu_ref[...].astype(BF16)
        wd = wd_ref[...].astype(BF16)
        for r in range(groups):
            rows = slice(r * (tm // groups), (r + 1) * (tm // groups))
            if first:
                shift = mod_ref[mod_row:mod_row + 1, :]
                scale = mod_ref[mod_row + 1:mod_row + 2, :]
                xn = (_rms(x_ref[rows, :]) * (1.0 + scale) + shift).astype(BF16)
                xn_ref[rows, :] = xn
            else:
                xn = xn_ref[rows, :]
            g = jnp.dot(xn, wg, preferred_element_type=F32)
            u = jnp.dot(xn, wu, preferred_element_type=F32)
            a = (_silu(g) * u).astype(BF16)
            y = jnp.dot(a, wd, preferred_element_type=F32)
            if not first:
                y = o_ref[rows, :] + y
            if last:
                gate = mod_ref[mod_row + 2:mod_row + 3, :]
                y = x_ref[rows, :] + (0.5 * gate) * y
            o_ref[rows, :] = y

    edge_groups = tm // EDGE_ROWS
    pl.when(f == 0)(lambda: step(True, False, edge_groups))
    pl.when((f > 0) & (f < nf - 1))(lambda: step(False, False, 1))
    pl.when(f == nf - 1)(lambda: step(False, True, edge_groups))


def _ffn(x2, mods, mod_at, layer, w_gu, w_down, mod_row, tm, tf):
    m, d = x2.shape
    hidden = w_down.shape[1]
    nf = hidden // tf
    return pl.pallas_call(
        functools.partial(_ffn_kernel, mod_row, nf),
        grid=(m // tm, nf),
        in_specs=[
            pl.BlockSpec((tm, d), lambda i, f: (i, 0), pipeline_mode=pl.Buffered(1)),
            pl.BlockSpec((None, None, N_MOD, d), lambda i, f: (layer, mod_at(i), 0, 0)),
            pl.BlockSpec((None, d, tf), lambda i, f: (layer, 0, f)),
            pl.BlockSpec((None, d, tf), lambda i, f: (layer, 0, nf + f)),
            pl.BlockSpec((None, tf, d), lambda i, f: (layer, f, 0)),
        ],
        out_specs=pl.BlockSpec((tm, d), lambda i, f: (i, 0)),
        out_shape=jax.ShapeDtypeStruct((m, d), F32),
        scratch_shapes=[pltpu.VMEM((tm, d), BF16)],
        compiler_params=_params("parallel", "arbitrary", vmem_limit=FFN_VMEM_LIMIT),
        name="ffn",
    )(x2, mods, w_gu, w_gu, w_down)


def _norm_lanes(y, width):
    y2 = y * y
    if width == HEAD_DIM:
        ms = jnp.mean(y2, axis=-1, keepdims=True)
    else:
        lane = lax.broadcasted_iota(jnp.int32, y.shape, 1)
        low = lane < width
        s_lo = jnp.sum(jnp.where(low, y2, 0.0), axis=-1, keepdims=True)
        s_hi = jnp.sum(jnp.where(low, 0.0, y2), axis=-1, keepdims=True)
        ms = jnp.where(low, s_lo, s_hi) * (1.0 / width)
    return y * lax.rsqrt(ms + RMS_EPS)


def _tile_modes():
    per_head = (['c'] * (2 * C_HEADS) + ['p'] * C_HEADS + ['a'] * (A_HEADS + A_KV_HEADS)
                + ['p'] * A_KV_HEADS + ['b'] * (2 * B_HEADS) + ['p'] * B_HEADS)
    hpt = TN // HEAD_DIM
    tiles = [set(per_head[t * hpt:(t + 1) * hpt]) for t in range(N_IN_TILES)]
    assert all(len(t) == 1 for t in tiles)
    return [t.pop() for t in tiles]


def _inproj_kernel(rope, x_ref, mod_ref, *rest):
    w_refs = rest[:IN_TILES_PER_STEP]
    gain_ref = rest[IN_TILES_PER_STEP]
    rest = rest[IN_TILES_PER_STEP + 1:]
    if rope:
        ca_ref, sa_ref, cb_ref, sb_ref, o_ref, xn_ref = rest
    else:
        o_ref, xn_ref = rest
    j = pl.program_id(1)
    heads_per_tile = TN // HEAD_DIM
    group_rows = x_ref.shape[0] // ROW_GROUPS
    rope_tables = {'a': (ca_ref, sa_ref, HEAD_DIM // 4),
                   'b': (cb_ref, sb_ref, B_SUB_DIM // 4)} if rope else {}
    lane = lax.broadcasted_iota(jnp.int32, (group_rows, HEAD_DIM), 1)
    norm_width = {'a': HEAD_DIM, 'b': B_SUB_DIM, 'c': HEAD_DIM, 'p': 0}

    def emit(modes, first=False):
        ws = [w_ref[...].astype(BF16) for w_ref in w_refs]
        for r in range(ROW_GROUPS):
            rows = slice(r * group_rows, (r + 1) * group_rows)
            if first:
                shift = mod_ref[3:4, :]
                scale = mod_ref[4:5, :]
                xn = (_rms(x_ref[rows, :]) * (1.0 + scale) + shift).astype(BF16)
                xn_ref[rows, :] = xn
            else:
                xn = xn_ref[rows, :]
            for t, mode in enumerate(modes):
                y = jnp.dot(xn, ws[t], preferred_element_type=F32)
                for hh in range(heads_per_tile):
                    cols = _head(t * heads_per_tile + hh)
                    v = y[:, _head(hh)]
                    if norm_width[mode]:
                        v = _norm_lanes(v, norm_width[mode]) * gain_ref[:, cols]
                    if mode in rope_tables:
                        c_ref, s_ref, half = rope_tables[mode]
                        partner = jnp.where((lane & (2 * half - 1)) < half,
                                            pltpu.roll(v, HEAD_DIM - half, 1), pltpu.roll(v, half, 1))
                        v = v * c_ref[rows, :] + partner * s_ref[rows, :]
                    o_ref[rows, cols] = v.astype(o_ref.dtype)

    tile_modes = _tile_modes()
    steps = {}
    for step in range(N_IN_TILES // IN_TILES_PER_STEP):
        modes = tuple(tile_modes[step * IN_TILES_PER_STEP:(step + 1) * IN_TILES_PER_STEP])
        steps.setdefault(modes, []).append(step)

    pl.when(j == 0)(functools.partial(emit, tuple(tile_modes[:IN_TILES_PER_STEP]), first=True))
    for modes, where in steps.items():
        later = [t for t in where if t > 0]
        if later:
            cond = functools.reduce(lambda a, b: a | b, [j == t for t in later])
            pl.when(cond)(functools.partial(emit, modes))


def _inproj(x2, mods, mod_at, layer, w_in, gains, tables, tm):
    m, d = x2.shape
    rope = tables is not None
    tn = TN * IN_TILES_PER_STEP
    src = lambda t: lax.rem(t + SRC_SHIFT, N_IN_TILES)
    w_spec = lambda k: pl.BlockSpec((None, d, TN),
                                    lambda i, j: (layer, 0, src(j * IN_TILES_PER_STEP + k)))
    in_specs = [
        pl.BlockSpec((tm, d), lambda i, j: (i, 0)),
        pl.BlockSpec((None, None, N_MOD, d), lambda i, j: (layer, mod_at(i), 0, 0)),
        *[w_spec(k) for k in range(IN_TILES_PER_STEP)],
        pl.BlockSpec((1, tn), lambda i, j: (0, j)),
    ]
    args = [x2, mods] + [w_in] * IN_TILES_PER_STEP + [gains]
    if rope:
        pos_tiles = SEQ // tm
        for t in tables:
            in_specs.append(pl.BlockSpec((tm, HEAD_DIM), lambda i, j: (i % pos_tiles, 0)))
            args.append(t)
    return pl.pallas_call(
        functools.partial(_inproj_kernel, rope),
        grid=(m // tm, N_IN_TILES // IN_TILES_PER_STEP),
        in_specs=in_specs,
        out_specs=pl.BlockSpec((tm, tn), lambda i, j: (i, j)),
        out_shape=jax.ShapeDtypeStruct((m, IN_WIDTH), BF16),
        scratch_shapes=[pltpu.VMEM((tm, d), BF16)],
        compiler_params=_params("parallel", "arbitrary"),
        name="inproj",
    )(*args)


def _rope_tables(half):
    t = jnp.arange(SEQ, dtype=jnp.int32)
    row = (t // GRID_W).astype(F32)[:, None]
    col = (t % GRID_W).astype(F32)[:, None]
    lane = np.arange(HEAD_DIM)
    freqs = ROPE_THETA ** (-jnp.asarray(lane % half, dtype=F32) / half)
    is_col = jnp.asarray((lane % (4 * half)) >= 2 * half)[None, :]
    ang = jnp.where(is_col, col, row) * freqs[None, :]
    first = jnp.asarray((lane % (2 * half)) < half)[None, :]
    cos = jnp.cos(ang)
    sin = jnp.sin(ang)
    return cos, jnp.where(first, -sin, sin)


_NT = (((1,), (1,)), ((), ()))


def _softmax_step(s, v, state):
    m_cur = jnp.max(s, axis=-1, keepdims=True)
    if state is None:
        m_new = m_cur
        p = jnp.exp2(s - m_new)
        l_new = jnp.sum(p, axis=-1, keepdims=True)
        acc = jnp.dot(p.astype(BF16), v, preferred_element_type=F32)
    else:
        m_old, l_old, acc_old = state
        m_new = jnp.maximum(m_old, m_cur)
        alpha = jnp.exp2(m_old - m_new)
        p = jnp.exp2(s - m_new)
        l_new = alpha * l_old + jnp.sum(p, axis=-1, keepdims=True)
        acc = alpha * acc_old + jnp.dot(p.astype(BF16), v, preferred_element_type=F32)
    return m_new, l_new, acc


def _attend(qs, kc, vc, kl_ref, vl_ref):
    def scores(k):
        return lax.dot_general(qs, k, _NT, preferred_element_type=F32)

    if kl_ref is None:
        _, l, acc = _softmax_step(scores(kc), vc, None)
        return acc / l
    k0 = jnp.concatenate([kc, kl_ref[:KV_CHUNK, :]], axis=0)
    v0 = jnp.concatenate([vc, vl_ref[:KV_CHUNK, :]], axis=0)
    state = _softmax_step(scores(k0), v0, None)
    for start in range(KV_CHUNK, kl_ref.shape[0], KV_CHUNK):
        state = _softmax_step(scores(kl_ref[start:start + KV_CHUNK, :]),
                              vl_ref[start:start + KV_CHUNK, :], state)
    _, l, acc = state
    return acc / l


def _store_normed(heads, gain_ref, o_ref):
    ms = sum(jnp.sum(o * o, axis=-1, keepdims=True) for o in heads) * (1.0 / (len(heads) * HEAD_DIM))
    r = lax.rsqrt(ms + RMS_EPS)
    for h, o in enumerate(heads):
        o_ref[:, _head(h)] = (o * r * gain_ref[:, _head(h)]).astype(o_ref.dtype)


def _gqa_heads(q, n_kv, group, pg_ref, k_h, v_h, kl_refs, vl_refs):
    tq = q.shape[0]
    heads = []
    for g in range(n_kv):
        qs = jnp.concatenate([q[:, _head(g * group + i)] for i in range(group)], axis=0)
        o = _attend(qs, pg_ref[:, _head(k_h + g)], pg_ref[:, _head(v_h + g)],
                    kl_refs[g] if kl_refs else None, vl_refs[g] if vl_refs else None)
        heads += [o[i * tq:(i + 1) * tq] for i in range(group)]
    return heads


def _gqa_lat_kernel(q_ref, pg_ref, *rest):
    kl_refs = rest[:A_KV_HEADS]
    vl_refs = rest[A_KV_HEADS:2 * A_KV_HEADS]
    gain_ref, o_ref = rest[2 * A_KV_HEADS:]
    heads = _gqa_heads(q_ref[...], A_KV_HEADS, A_GROUP, pg_ref, AK_H, AV_H, kl_refs, vl_refs)
    _store_normed(heads, gain_ref, o_ref)


def _gqa_ctx_kernel(n_kv, group, q_h, k_h, v_h, pg_ref, gain_ref, o_ref):
    q = pg_ref[:, q_h * HEAD_DIM:(q_h + n_kv * group) * HEAD_DIM]
    _store_normed(_gqa_heads(q, n_kv, group, pg_ref, k_h, v_h, None, None), gain_ref, o_ref)


def _kv_specs(k_h, v_h, n_heads):
    heads = [k_h + g for g in range(n_heads)] + [v_h + g for g in range(n_heads)]
    return [pl.BlockSpec((None, SEQ, HEAD_DIM), lambda bi, qi, h=h: (bi, 0, h)) for h in heads]


def _gqa_lat(px, pg, gain):
    b = px.shape[0]
    return pl.pallas_call(
        _gqa_lat_kernel,
        grid=(b, SEQ // TQ_A),
        in_specs=[
            pl.BlockSpec((None, TQ_A, A_WIDTH), lambda bi, qi: (bi, qi, AQ_H * HEAD_DIM // A_WIDTH)),
            pl.BlockSpec((None, CTX_LEN, IN_WIDTH), lambda bi, qi: (bi, 0, 0)),
            *_kv_specs(AK_H, AV_H, A_KV_HEADS),
            pl.BlockSpec((1, A_WIDTH), lambda bi, qi: (0, 0)),
        ],
        out_specs=pl.BlockSpec((None, TQ_A, A_WIDTH), lambda bi, qi: (bi, qi, 0)),
        out_shape=jax.ShapeDtypeStruct((b, SEQ, A_WIDTH), BF16),
        compiler_params=_params("parallel", "arbitrary"),
        name="gqa_attention",
    )(px, pg, *([px] * (2 * A_KV_HEADS)), gain)


def _gqa_ctx(pg, gain, n_kv, group, q_h, k_h, v_h):
    b = pg.shape[0]
    width = n_kv * group * HEAD_DIM
    return pl.pallas_call(
        functools.partial(_gqa_ctx_kernel, n_kv, group, q_h, k_h, v_h),
        grid=(b,),
        in_specs=[
            pl.BlockSpec((None, CTX_LEN, IN_WIDTH), lambda bi: (bi, 0, 0)),
            pl.BlockSpec((1, width), lambda bi: (0, 0)),
        ],
        out_specs=pl.BlockSpec((None, CTX_LEN, width), lambda bi: (bi, 0, 0)),
        out_shape=jax.ShapeDtypeStruct((b, CTX_LEN, width), BF16),
        compiler_params=_params("parallel"),
        name="ctx_attention",
    )(pg, gain)


def _diff_heads(q, lam_init, lam_ref, gain_ref, pg_ref, kl_refs, vl_refs, o_ref):
    tq = q.shape[0]
    lv = lam_ref[...]
    lam = (jnp.exp(jnp.sum(lv[0:1] * lv[1:2], axis=-1, keepdims=True))
           - jnp.exp(jnp.sum(lv[2:3] * lv[3:4], axis=-1, keepdims=True)) + lam_init)
    low = lax.broadcasted_iota(jnp.int32, (tq, HEAD_DIM), 1) < B_SUB_DIM
    zero = jnp.zeros((tq, HEAD_DIM), q.dtype)
    for h in range(B_HEADS):
        qh = q[:, _head(h)]
        qs = jnp.concatenate([jnp.where(low, qh, zero), jnp.where(low, zero, qh)], axis=0)
        o = _attend(qs, pg_ref[:, _head(BK_H + h)], pg_ref[:, _head(BV_H + h)],
                    kl_refs[h] if kl_refs else None, vl_refs[h] if vl_refs else None)
        o = _rms(o[:tq] - lam * o[tq:]) * gain_ref[...] * (1.0 - lam_init)
        o_ref[:, _head(h)] = o.astype(o_ref.dtype)


def _diff_lat_kernel(lam_init, q_ref, lam_ref, gain_ref, pg_ref, *rest):
    kl_refs = rest[:B_HEADS]
    vl_refs = rest[B_HEADS:2 * B_HEADS]
    o_ref = rest[2 * B_HEADS]
    _diff_heads(q_ref[...], lam_init, lam_ref, gain_ref, pg_ref, kl_refs, vl_refs, o_ref)


def _diff_ctx_kernel(lam_init, lam_ref, gain_ref, pg_ref, o_ref):
    q = pg_ref[:, BQ_H * HEAD_DIM:BK_H * HEAD_DIM]
    _diff_heads(q, lam_init, lam_ref, gain_ref, pg_ref, None, None, o_ref)


def _diff_lat(px, pg, lam_vecs, gain, lam_init):
    b = px.shape[0]
    return pl.pallas_call(
        functools.partial(_diff_lat_kernel, lam_init),
        grid=(b, SEQ // TQ_B),
        in_specs=[
            pl.BlockSpec((None, TQ_B, B_WIDTH), lambda bi, qi: (bi, qi, BQ_H * HEAD_DIM // B_WIDTH)),
            pl.BlockSpec((4, B_SUB_DIM), lambda bi, qi: (0, 0)),
            pl.BlockSpec((1, HEAD_DIM), lambda bi, qi: (0, 0)),
            pl.BlockSpec((None, CTX_LEN, IN_WIDTH), lambda bi, qi: (bi, 0, 0)),
            *_kv_specs(BK_H, BV_H, B_HEADS),
        ],
        out_specs=pl.BlockSpec((None, TQ_B, B_WIDTH), lambda bi, qi: (bi, qi, 0)),
        out_shape=jax.ShapeDtypeStruct((b, SEQ, B_WIDTH), BF16),
        compiler_params=_params("parallel", "arbitrary"),
        name="diff_attention",
    )(px, lam_vecs, gain, pg, *([px] * (2 * B_HEADS)))


def _diff_ctx(pg, lam_vecs, gain, lam_init):
    b = pg.shape[0]
    return pl.pallas_call(
        functools.partial(_diff_ctx_kernel, lam_init),
        grid=(b,),
        in_specs=[
            pl.BlockSpec((4, B_SUB_DIM), lambda bi: (0, 0)),
            pl.BlockSpec((1, HEAD_DIM), lambda bi: (0, 0)),
            pl.BlockSpec((None, CTX_LEN, IN_WIDTH), lambda bi: (bi, 0, 0)),
        ],
        out_specs=pl.BlockSpec((None, CTX_LEN, B_WIDTH), lambda bi: (bi, 0, 0)),
        out_shape=jax.ShapeDtypeStruct((b, CTX_LEN, B_WIDTH), BF16),
        compiler_params=_params("parallel"),
        name="diff_ctx_attention",
    )(lam_vecs, gain, pg)


def _nbr_block_class(j):
    return jnp.where(j == 0, 0, jnp.where(j == GRID_H // NB_ROWS - 1, 2, 1))


def _nbr_fill_bias(pairs_ref, bias_ref):
    n_blocks = GRID_H // NB_ROWS
    lane = lax.broadcasted_iota(jnp.int32, (GRID_W, 2 * GRID_W), 1)
    masked = jnp.full((GRID_W, 2 * GRID_W), MASK_VALUE, F32)
    for cls, j in enumerate((0, 1, n_blocks - 1)):
        key_row0 = int(np.clip(NB_ROWS * j - NA_ROWS // 2, 0, GRID_H - NB_KROWS))
        for i in range(NB_ROWS):
            r = NB_ROWS * j + i
            rs = int(np.clip(r - NA_ROWS // 2, 0, GRID_H - NA_ROWS))
            for pair in range(NB_KROWS // 2):
                kr = key_row0 + 2 * pair
                left_in = rs <= kr < rs + NA_ROWS
                right_in = rs <= kr + 1 < rs + NA_ROWS
                offset = kr - r + NA_ROWS - 1
                for h in range(C_HEADS):
                    if left_in or right_in:
                        tile = pairs_ref[h, offset + 1]
                        if not right_in:
                            tile = jnp.where(lane < GRID_W, tile, MASK_VALUE)
                        if not left_in:
                            tile = jnp.where(lane < GRID_W, MASK_VALUE, tile)
                    else:
                        tile = masked
                    bias_ref[cls, h, i * GRID_W:(i + 1) * GRID_W,
                             pair * 2 * GRID_W:(pair + 1) * 2 * GRID_W] = tile


def _nbr_kernel(q_ref, k_ref, v_ref, pg_ref, pairs_ref, gain_ref, o_ref, bias_ref):
    j = pl.program_id(1)

    @pl.when((pl.program_id(0) == 0) & (j == 0))
    def _():
        _nbr_fill_bias(pairs_ref, bias_ref)

    cls = _nbr_block_class(j)
    key_row0 = jnp.clip(NB_ROWS * j - NA_ROWS // 2, 0, GRID_H - NB_KROWS)
    start = pl.multiple_of(key_row0 * GRID_W, NB_ROWS * GRID_W)
    heads = []
    for h in range(C_HEADS):
        q = q_ref[:, _head(h)]
        kw = k_ref[pl.ds(start, NB_K), _head(h)]
        vw = v_ref[pl.ds(start, NB_K), _head(h)]
        kc = pg_ref[:, _head(CK_H + h)]
        vc = pg_ref[:, _head(CV_H + h)]
        s_w = lax.dot_general(q, kw, _NT, preferred_element_type=F32) + bias_ref[cls, h]
        s_c = lax.dot_general(q, kc, _NT, preferred_element_type=F32)
        m = jnp.maximum(jnp.max(s_w, axis=-1, keepdims=True), jnp.max(s_c, axis=-1, keepdims=True))
        p_w = jnp.exp2(s_w - m)
        p_c = jnp.exp2(s_c - m)
        l = jnp.sum(p_w, axis=-1, keepdims=True) + jnp.sum(p_c, axis=-1, keepdims=True)
        o = (jnp.dot(p_w.astype(BF16), vw, preferred_element_type=F32)
             + jnp.dot(p_c.astype(BF16), vc, preferred_element_type=F32))
        heads.append(o / l)
    _store_normed(heads, gain_ref, o_ref)


def _nbr_pair_table(rel_bias):
    col = np.arange(GRID_W)
    cs = np.clip(col - NA_COLS // 2, 0, GRID_W - NA_COLS)
    col_ok = (col[None, :] >= cs[:, None]) & (col[None, :] < cs[:, None] + NA_COLS)
    dc = col[None, :] - col[:, None] + NA_COLS - 1
    onehot = (dc[None] == np.arange(2 * NA_COLS - 1)[:, None, None]) & col_ok[None]
    per_row = jnp.einsum('hrt,tck->hrck', rel_bias.astype(F32) * LOG2E, jnp.asarray(onehot, F32),
                         precision=lax.Precision.HIGHEST)
    per_row = jnp.where(col_ok, per_row, MASK_VALUE)
    per_row = jnp.pad(per_row, ((0, 0), (1, 1), (0, 0), (0, 0)), constant_values=MASK_VALUE)
    return jnp.concatenate([per_row[:, :-1], per_row[:, 1:]], axis=-1)


def _nbr(px, pg, pairs, gain):
    b = px.shape[0]
    n_blocks = GRID_H // NB_ROWS
    blk = lambda h: h * HEAD_DIM // C_WIDTH
    once = pl.Buffered(1)
    return pl.pallas_call(
        _nbr_kernel,
        grid=(b, n_blocks),
        in_specs=[
            pl.BlockSpec((None, NB_Q, C_WIDTH), lambda bi, j: (bi, j, blk(CQ_H))),
            pl.BlockSpec((None, SEQ, C_WIDTH), lambda bi, j: (bi, 0, blk(CK_H)), pipeline_mode=once),
            pl.BlockSpec((None, SEQ, C_WIDTH), lambda bi, j: (bi, 0, blk(CV_H)), pipeline_mode=once),
            pl.BlockSpec((None, CTX_LEN, IN_WIDTH), lambda bi, j: (bi, 0, 0)),
            pl.BlockSpec(pairs.shape, lambda bi, j: (0, 0, 0, 0), pipeline_mode=once),
            pl.BlockSpec((1, C_WIDTH), lambda bi, j: (0, 0)),
        ],
        out_specs=pl.BlockSpec((None, NB_Q, C_WIDTH), lambda bi, j: (bi, j, 0)),
        out_shape=jax.ShapeDtypeStruct((b, SEQ, C_WIDTH), BF16),
        scratch_shapes=[pltpu.VMEM((3, C_HEADS, NB_Q, NB_K), F32)],
        compiler_params=_params("arbitrary", "arbitrary"),
        name="nbr_attention",
    )(px, px, px, pg, pairs, gain)


def _outproj_kernel(x_ref, mod_ref, a_ref, b_ref, c_ref, w_ref, o_ref):
    y = (jnp.dot(a_ref[...], w_ref[:A_WIDTH, :].astype(BF16), preferred_element_type=F32)
         + jnp.dot(b_ref[...], w_ref[A_WIDTH:A_WIDTH + B_WIDTH, :].astype(BF16),
                   preferred_element_type=F32)
         + jnp.dot(c_ref[...], w_ref[A_WIDTH + B_WIDTH:, :].astype(BF16),
                   preferred_element_type=F32))
    o_ref[...] = x_ref[...] + mod_ref[5:6, :] * y


def _outproj(x2, mods, mod_at, layer, a2, b2, c2, w_out, tm):
    m, d = x2.shape
    return pl.pallas_call(
        _outproj_kernel,
        grid=(m // tm, d // TN_OUT),
        in_specs=[
            pl.BlockSpec((tm, TN_OUT), lambda i, j: (i, j)),
            pl.BlockSpec((None, None, N_MOD, TN_OUT), lambda i, j: (layer, mod_at(i), 0, j)),
            pl.BlockSpec((tm, A_WIDTH), lambda i, j: (i, 0)),
            pl.BlockSpec((tm, B_WIDTH), lambda i, j: (i, 0)),
            pl.BlockSpec((tm, C_WIDTH), lambda i, j: (i, 0)),
            pl.BlockSpec((None, MIX_WIDTH, TN_OUT), lambda i, j: (layer, 0, j)),
        ],
        out_specs=pl.BlockSpec((tm, TN_OUT), lambda i, j: (i, j)),
        out_shape=jax.ShapeDtypeStruct((m, d), F32),
        compiler_params=_params("parallel", "arbitrary"),
        name="outproj",
    )(x2, mods, a2, b2, c2, w_out)


def _column_gains(a_q, a_k, b_q, b_k, c_q, c_k):
    a_scale = LOG2E * HEAD_DIM ** -0.5
    b_scale = LOG2E * B_SUB_DIM ** -0.5
    parts = (
        jnp.tile(c_q * a_scale, C_HEADS), jnp.tile(c_k, C_HEADS), jnp.ones((C_WIDTH,), F32),
        jnp.tile(a_q * a_scale, A_HEADS), jnp.tile(a_k, A_KV_HEADS), jnp.ones((A_KV_WIDTH,), F32),
        jnp.tile(b_q * b_scale, 2 * B_HEADS), jnp.tile(b_k, 2 * B_HEADS), jnp.ones((B_WIDTH,), F32),
    )
    return jnp.concatenate(parts).reshape(1, IN_WIDTH)


def kernel(x, c, ctx, c_ctx, w_mod, b_mod, ffn1_w_gu, ffn1_w_down, ffn2_w_gu, ffn2_w_down,
           w_in, w_out, a_q_gain, a_k_gain, a_out_gain, b_q_gain, b_k_gain, b_lambda, b_out_gain,
           c_q_gain, c_k_gain, c_rel_bias, c_out_gain):
    b, s, d = x.shape
    n_ctx = ctx.shape[1]
    assert (b, s, d, n_ctx) == (2, SEQ, D_MODEL, CTX_LEN)

    cvec = jnp.zeros((8, d), F32).at[:b].set(c).at[b].set(c_ctx)
    mods = _modulation(cvec, w_mod, b_mod).reshape(DEPTH, 8, N_MOD, d)
    tables = _rope_tables(HEAD_DIM // 4) + _rope_tables(B_SUB_DIM // 4)

    x2 = x.reshape(b * s, d)
    g2 = ctx.reshape(b * n_ctx, d)
    tm_g = b * n_ctx

    mod_x = lambda i: i // (s // TM)
    mod_x_out = lambda i: i // (s // TM_OUT)
    mod_x_ffn = lambda i: i // (s // TM_FFN)
    mod_g = lambda i: b

    for l in range(DEPTH):
        with_ctx = l < DEPTH - 1
        lam_init = 0.8 - 0.6 * math.exp(-0.3 * l)
        gains = _column_gains(a_q_gain[l], a_k_gain[l], b_q_gain[l], b_k_gain[l],
                              c_q_gain[l], c_k_gain[l])
        a_gain = a_out_gain[l].reshape(1, A_WIDTH)
        b_gain = b_out_gain[l].reshape(1, HEAD_DIM)
        c_gain = c_out_gain[l].reshape(1, C_WIDTH)

        x2 = _ffn(x2, mods, mod_x_ffn, l, ffn1_w_gu, ffn1_w_down, 0, TM_FFN, TF)
        g2 = _ffn(g2, mods, mod_g, l, ffn1_w_gu, ffn1_w_down, 0, tm_g, TF_CTX)

        px = _inproj(x2, mods, mod_x, l, w_in, gains, tables, TM).reshape(b, s, IN_WIDTH)
        pg = _inproj(g2, mods, mod_g, l, w_in, gains, None, tm_g).reshape(b, n_ctx, IN_WIDTH)

        a_lat = _gqa_lat(px, pg, a_gain)
        b_lat = _diff_lat(px, pg, b_lambda[l], b_gain, lam_init)
        c_lat = _nbr(px, pg, _nbr_pair_table(c_rel_bias[l]), c_gain)

        x2 = _outproj(x2, mods, mod_x_out, l, a_lat.reshape(b * s, A_WIDTH),
                      b_lat.reshape(b * s, B_WIDTH), c_lat.reshape(b * s, C_WIDTH), w_out, TM_OUT)
        x2 = _ffn(x2, mods, mod_x_ffn, l, ffn2_w_gu, ffn2_w_down, 6, TM_FFN, TF)

        if with_ctx:
            a_ctx = _gqa_ctx(pg, a_gain, A_KV_HEADS, A_GROUP, AQ_H, AK_H, AV_H)
            b_ctx = _diff_ctx(pg, b_lambda[l], b_gain, lam_init)
            c_ctx = _gqa_ctx(pg, c_gain, C_HEADS, 1, CQ_H, CK_H, CV_H)
            g2 = _outproj(g2, mods, mod_g, l, a_ctx.reshape(tm_g, A_WIDTH),
                          b_ctx.reshape(tm_g, B_WIDTH), c_ctx.reshape(tm_g, C_WIDTH), w_out, tm_g)
            g2 = _ffn(g2, mods, mod_g, l, ffn2_w_gu, ffn2_w_down, 6, tm_g, TF_CTX)

    return x2.reshape(b, s, d)
```
